```python
import jax, jax.numpy as jnp
from jax import lax
import numpy as np

D_MODEL = 1024
BATCH = 16
SEQ = 2048
DEPTH = 1
DEC_BATCH = 128
DEC_SEQ = 4
PAST_LEN = 16384
PAGE_SIZE = 128

S5_WIDTH = D_MODEL // 2
S5_GROUP = 16
S5_GROUPS = S5_WIDTH // S5_GROUP
S5_STATE = 64
MLA_HEADS = 8
QK_NOPE = 64
QK_ROPE = 32
V_DIM = 64
Q_LORA = 384
KV_LORA = 256
ROPE_BASE = 10000.0
Q_BLOCK = 128
D_FF = 2816
NORM_EPS = 1e-6
IN_WIDTH = S5_WIDTH + Q_LORA + KV_LORA + QK_ROPE + 2 * D_MODEL
SPLITS = (S5_WIDTH, S5_WIDTH + Q_LORA, S5_WIDTH + Q_LORA + KV_LORA,
          S5_WIDTH + Q_LORA + KV_LORA + QK_ROPE, S5_WIDTH + Q_LORA + KV_LORA + QK_ROPE + D_MODEL)

kernel_name = "hybrid_s5_mla_macaron_step"


def _rms(x, g):
    xf = x.astype(jnp.float32)
    inv = lax.rsqrt(jnp.mean(xf * xf, axis=-1, keepdims=True) + NORM_EPS)
    return (xf * inv * g.astype(jnp.float32)).astype(x.dtype)


def _swiglu(x, w_gate, w_up, w_down):
    return (jax.nn.silu(x @ w_gate) * (x @ w_up)) @ w_down


def _rope_angles(pos):
    inv_freq = 1.0 / (ROPE_BASE ** (jnp.arange(0, QK_ROPE, 2, dtype=jnp.float32) / QK_ROPE))
    ang = pos.astype(jnp.float32)[:, None] * inv_freq[None, :]
    return jnp.cos(ang), jnp.sin(ang)


def _apply_rope(x, cos, sin):
    xf = x.astype(jnp.float32)
    x1, x2 = xf[..., :QK_ROPE // 2], xf[..., QK_ROPE // 2:]
    return jnp.concatenate([x1 * cos - x2 * sin, x1 * sin + x2 * cos], axis=-1).astype(x.dtype)


def _cplx_combine(e1, e2):
    a1r, a1i, b1r, b1i = e1
    a2r, a2i, b2r, b2i = e2
    return (a2r * a1r - a2i * a1i,
            a2r * a1i + a2i * a1r,
            a2r * b1r - a2i * b1i + b2r,
            a2r * b1i + a2i * b1r + b2i)


def _s5(u, h0_re, h0_im, a_re, a_im, log_dt, b_re, b_im, c_re, c_im, d_skip):
    f32 = jnp.float32
    bsz, L, _ = u.shape
    uf = u.astype(f32).reshape(bsz, L, S5_GROUPS, S5_GROUP)
    lam_re, lam_im = a_re.astype(f32), a_im.astype(f32)
    dt = jnp.exp(log_dt.astype(f32))[:, None]
    mag = jnp.exp(lam_re * dt)
    ab_re, ab_im = mag * jnp.cos(lam_im * dt), mag * jnp.sin(lam_im * dt)
    den = lam_re * lam_re + lam_im * lam_im
    num_re, num_im = ab_re - 1.0, ab_im
    k_re = (num_re * lam_re + num_im * lam_im) / den
    k_im = (num_im * lam_re - num_re * lam_im) / den
    bu_re = jnp.einsum('blgh,gph->blgp', uf, b_re.astype(f32))
    bu_im = jnp.einsum('blgh,gph->blgp', uf, b_im.astype(f32))
    x_re = k_re * bu_re - k_im * bu_im
    x_im = k_re * bu_im + k_im * bu_re
    h0r, h0i = h0_re.astype(f32), h0_im.astype(f32)
    x_re = x_re.at[:, 0].add(ab_re * h0r - ab_im * h0i)
    x_im = x_im.at[:, 0].add(ab_re * h0i + ab_im * h0r)
    a_seq_re = jnp.broadcast_to(ab_re, x_re.shape)
    a_seq_im = jnp.broadcast_to(ab_im, x_re.shape)
    _, _, h_re, h_im = lax.associative_scan(_cplx_combine, (a_seq_re, a_seq_im, x_re, x_im), axis=1)
    y = (jnp.einsum('blgp,ghp->blgh', h_re, c_re.astype(f32))
         - jnp.einsum('blgp,ghp->blgh', h_im, c_im.astype(f32))
         + d_skip.astype(f32) * uf)
    return y.reshape(bsz, L, S5_WIDTH).astype(u.dtype), h_re[:, -1], h_im[:, -1]


def _mla_attend_prompt(q_lat, q_rope, ckv, krope):
    bsz, L, H, C = q_lat.shape
    n_blk = L // Q_BLOCK
    scale = (QK_NOPE + QK_ROPE) ** -0.5
    k_pos = jnp.arange(L)

    def block(args):
        i, ql, qr = args
        s = (jnp.einsum('bqhc,bkc->bhqk', ql, ckv)
             + jnp.einsum('bqhr,bkr->bhqk', qr, krope)).astype(jnp.float32) * scale
        q_pos = i * Q_BLOCK + jnp.arange(Q_BLOCK)
        s = jnp.where(k_pos[None, :] <= q_pos[:, None], s, -jnp.inf)
        p = jax.nn.softmax(s, axis=-1).astype(ckv.dtype)
        return jnp.einsum('bhqk,bkc->bqhc', p, ckv)

    ql = jnp.moveaxis(q_lat.reshape(bsz, n_blk, Q_BLOCK, H, C), 1, 0)
    qr = jnp.moveaxis(q_rope.reshape(bsz, n_blk, Q_BLOCK, H, QK_ROPE), 1, 0)
    o = lax.map(block, (jnp.arange(n_blk), ql, qr))
    return jnp.moveaxis(o, 0, 1).reshape(bsz, L, H, C)


def _mla_attend_sample(q_lat, q_rope, ckv_new, krope_new, ckv_past, krope_past):
    T = q_lat.shape[1]
    P = ckv_past.shape[1]
    scale = (QK_NOPE + QK_ROPE) ** -0.5
    s_past = (jnp.einsum('bqhc,bkc->bhqk', q_lat, ckv_past)
              + jnp.einsum('bqhr,bkr->bhqk', q_rope, krope_past)).astype(jnp.float32)
    s_new = (jnp.einsum('bqhc,bkc->bhqk', q_lat, ckv_new)
             + jnp.einsum('bqhr,bkr->bhqk', q_rope, krope_new)).astype(jnp.float32)
    s_new = jnp.where(jnp.tril(jnp.ones((T, T), dtype=bool)), s_new, -jnp.inf)
    p = jax.nn.softmax(jnp.concatenate([s_past, s_new], axis=-1) * scale, axis=-1).astype(ckv_new.dtype)
    return (jnp.einsum('bhqk,bkc->bqhc', p[..., :P], ckv_past)
            + jnp.einsum('bhqk,bkc->bqhc', p[..., P:], ckv_new))


def _mixer(u, pos, h0_re, h0_im, ckv_past, krope_past, p):
    bsz, L, _ = u.shape
    proj = u @ p['w_in']
    a_in, c_q, c_kv, k_r, g_a, g_b = jnp.split(proj, SPLITS, axis=-1)
    ys, hT_re, hT_im = _s5(a_in, h0_re, h0_im, p['s5_a_re'], p['s5_a_im'], p['s5_log_dt'],
                           p['s5_b_re'], p['s5_b_im'], p['s5_c_re'], p['s5_c_im'], p['s5_d'])
    z = jax.nn.gelu(ys)
    y_a = (z * jax.nn.sigmoid(z @ p['w_glu'] + p['b_glu'])) @ p['w_a_out']
    cos, sin = _rope_angles(pos)
    ckv = _rms(c_kv, p['g_kv'])
    krope = _apply_rope(k_r, cos, sin)
    q = jnp.einsum('blc,chd->blhd', _rms(c_q, p['g_q']), p['w_uq'])
    q_nope = q[..., :QK_NOPE]
    q_rope = _apply_rope(q[..., QK_NOPE:], cos[:, None, :], sin[:, None, :])
    q_lat = jnp.einsum('blhd,chd->blhc', q_nope, p['w_uk'])
    if ckv_past is None:
        o_lat = _mla_attend_prompt(q_lat, q_rope, ckv, krope)
    else:
        o_lat = _mla_attend_sample(q_lat, q_rope, ckv, krope, ckv_past, krope_past)
    o = jnp.einsum('blhc,chd->blhd', o_lat, p['w_uv']).reshape(bsz, L, MLA_HEADS * V_DIM)
    y_b = o @ p['w_b_out']
    merged = jax.nn.sigmoid(g_a) * y_a + jax.nn.sigmoid(g_b) * y_b
    return merged @ p['w_out'], ckv, krope, hT_re, hT_im


def _layer(h, pos, h0_re, h0_im, ckv_past, krope_past, p):
    h = h + 0.5 * _swiglu(_rms(h, p['g_ffn1']), p['w_ffn1_gate'], p['w_ffn1_up'], p['w_ffn1_down'])
    mix, ckv, krope, hT_re, hT_im = _mixer(_rms(h, p['g_mix']), pos, h0_re, h0_im, ckv_past, krope_past, p)
    h = h + mix
    h = h + 0.5 * _swiglu(_rms(h, p['g_ffn2']), p['w_ffn2_gate'], p['w_ffn2_up'], p['w_ffn2_down'])
    return h, (ckv, krope, hT_re, hT_im)


def _trunk(x, pos, h0_re, h0_im, cache_latent, cache_k_rope, page_table, params, g_final):
    h = x
    outs = []
    for l in range(DEPTH):
        p = jax.tree_util.tree_map(lambda a: a[l], params)
        if page_table is None:
            ckv_past, krope_past = None, None
        else:
            nb, npg = page_table.shape
            ckv_past = cache_latent[l, page_table].reshape(nb, npg * PAGE_SIZE, KV_LORA)
            krope_past = cache_k_rope[l, page_table].reshape(nb, npg * PAGE_SIZE, QK_ROPE)
        h, st = _layer(h, pos, h0_re[l], h0_im[l], ckv_past, krope_past, p)
        outs.append(st)
    ckv = jnp.stack([o[0] for o in outs])
    krope = jnp.stack([o[1] for o in outs])
    s_re = jnp.stack([o[2] for o in outs])
    s_im = jnp.stack([o[3] for o in outs])
    return _rms(h, g_final), ckv, krope, s_re, s_im


def setup_inputs(seed: int = 0) -> dict:
    key = jax.random.key(seed)
    ks = jax.random.split(key, 40)
    f32 = jnp.float32
    n_pages = PAST_LEN // PAGE_SIZE
    n_used = DEC_BATCH * n_pages
    n_pool = (n_used * 5) // 4
    Dp = DEPTH

    def nrm(k, shape, scale):
        return jax.random.normal(k, shape, f32) * scale

    def gain(k, shape):
        return 1.0 + 0.05 * jax.random.normal(k, shape, f32)

    page_table = jax.random.permutation(ks[0], n_pool)[:n_used].reshape(DEC_BATCH, n_pages).astype(jnp.int32)
    n_idx = jnp.arange(S5_STATE, dtype=f32)
    return {
        "x_prompt": nrm(ks[1], (BATCH, SEQ, D_MODEL), 1.0),
        "x_sample": nrm(ks[2], (DEC_BATCH, DEC_SEQ, D_MODEL), 1.0),
        "cache_latent": nrm(ks[3], (Dp, n_pool, PAGE_SIZE, KV_LORA), 1.0),
        "cache_k_rope": nrm(ks[4], (Dp, n_pool, PAGE_SIZE, QK_ROPE), 1.0),
        "state_ssm_re": nrm(ks[5], (Dp, DEC_BATCH, S5_GROUPS, S5_STATE), 0.1),
        "state_ssm_im": nrm(ks[6], (Dp, DEC_BATCH, S5_GROUPS, S5_STATE), 0.1),
        "page_table": page_table,
        "g_ffn1": gain(ks[7], (Dp, D_MODEL)),
        "w_ffn1_gate": nrm(ks[8], (Dp, D_MODEL, D_FF), D_MODEL ** -0.5),
        "w_ffn1_up": nrm(ks[9], (Dp, D_MODEL, D_FF), D_MODEL ** -0.5),
        "w_ffn1_down": nrm(ks[10], (Dp, D_FF, D_MODEL), D_FF ** -0.5),
        "g_mix": gain(ks[11], (Dp, D_MODEL)),
        "w_in": nrm(ks[12], (Dp, D_MODEL, IN_WIDTH), D_MODEL ** -0.5),
        "s5_a_re": -0.5 + 0.01 * jax.random.normal(ks[13], (Dp, S5_GROUPS, S5_STATE), f32),
        "s5_a_im": jnp.pi * n_idx + 0.01 * jax.random.normal(ks[14], (Dp, S5_GROUPS, S5_STATE), f32),
        "s5_log_dt": jax.random.uniform(ks[15], (Dp, S5_GROUPS), f32, minval=float(np.log(0.001)), maxval=float(np.log(0.1))),
        "s5_b_re": nrm(ks[16], (Dp, S5_GROUPS, S5_STATE, S5_GROUP), (2.0 * S5_GROUP) ** -0.5),
        "s5_b_im": nrm(ks[17], (Dp, S5_GROUPS, S5_STATE, S5_GROUP), (2.0 * S5_GROUP) ** -0.5),
        "s5_c_re": nrm(ks[18], (Dp, S5_GROUPS, S5_GROUP, S5_STATE), (2.0 * S5_STATE) ** -0.5),
        "s5_c_im": nrm(ks[19], (Dp, S5_GROUPS, S5_GROUP, S5_STATE), (2.0 * S5_STATE) ** -0.5),
        "s5_d": nrm(ks[20], (Dp, S5_GROUPS, S5_GROUP), 1.0),
        "w_glu": nrm(ks[21], (Dp, S5_WIDTH, S5_WIDTH), S5_WIDTH ** -0.5),
        "b_glu": nrm(ks[22], (Dp, S5_WIDTH), 0.01),
        "w_a_out": nrm(ks[23], (Dp, S5_WIDTH, D_MODEL), S5_WIDTH ** -0.5),
        "g_q": gain(ks[24], (Dp, Q_LORA)),
        "w_uq": nrm(ks[25], (Dp, Q_LORA, MLA_HEADS, QK_NOPE + QK_ROPE), Q_LORA ** -0.5),
        "g_kv": gain(ks[26], (Dp, KV_LORA)),
        "w_uk": nrm(ks[27], (Dp, KV_LORA, MLA_HEADS, QK_NOPE), KV_LORA ** -0.5),
        "w_uv": nrm(ks[28], (Dp, KV_LORA, MLA_HEADS, V_DIM), KV_LORA ** -0.5),
        "w_b_out": nrm(ks[29], (Dp, MLA_HEADS * V_DIM, D_MODEL), (MLA_HEADS * V_DIM) ** -0.5),
        "w_out": nrm(ks[30], (Dp, D_MODEL, D_MODEL), D_MODEL ** -0.5),
        "g_ffn2": gain(ks[31], (Dp, D_MODEL)),
        "w_ffn2_gate": nrm(ks[32], (Dp, D_MODEL, D_FF), D_MODEL ** -0.5),
        "w_ffn2_up": nrm(ks[33], (Dp, D_MODEL, D_FF), D_MODEL ** -0.5),
        "w_ffn2_down": nrm(ks[34], (Dp, D_FF, D_MODEL), D_FF ** -0.5),
        "g_final": gain(ks[35], (D_MODEL,)),
    }


def reference(x_prompt, x_sample, cache_latent, cache_k_rope, state_ssm_re, state_ssm_im, page_table,
              g_ffn1, w_ffn1_gate, w_ffn1_up, w_ffn1_down, g_mix, w_in,
              s5_a_re, s5_a_im, s5_log_dt, s5_b_re, s5_b_im, s5_c_re, s5_c_im, s5_d,
              w_glu, b_glu, w_a_out, g_q, w_uq, g_kv, w_uk, w_uv, w_b_out, w_out,
              g_ffn2, w_ffn2_gate, w_ffn2_up, w_ffn2_down, g_final):
    params = {
        'g_ffn1': g_ffn1, 'w_ffn1_gate': w_ffn1_gate, 'w_ffn1_up': w_ffn1_up, 'w_ffn1_down': w_ffn1_down,
        'g_mix': g_mix, 'w_in': w_in,
        's5_a_re': s5_a_re, 's5_a_im': s5_a_im, 's5_log_dt': s5_log_dt,
        's5_b_re': s5_b_re, 's5_b_im': s5_b_im, 's5_c_re': s5_c_re, 's5_c_im': s5_c_im, 's5_d': s5_d,
        'w_glu': w_glu, 'b_glu': b_glu, 'w_a_out': w_a_out,
        'g_q': g_q, 'w_uq': w_uq, 'g_kv': g_kv, 'w_uk': w_uk, 'w_uv': w_uv, 'w_b_out': w_b_out,
        'w_out': w_out,
        'g_ffn2': g_ffn2, 'w_ffn2_gate': w_ffn2_gate, 'w_ffn2_up': w_ffn2_up, 'w_ffn2_down': w_ffn2_down,
    }
    bsz, L, _ = x_prompt.shape
    zeros_state = jnp.zeros((DEPTH, bsz, S5_GROUPS, S5_STATE), jnp.float32)
    pos_prompt = jnp.arange(L)
    y_prompt, ckv_prompt, krope_prompt, ssm_re_prompt, ssm_im_prompt = _trunk(
        x_prompt, pos_prompt, zeros_state, zeros_state, None, None, None, params, g_final)
    pos_sample = PAST_LEN + jnp.arange(x_sample.shape[1])
    y_sample, ckv_sample, krope_sample, ssm_re_sample, ssm_im_sample = _trunk(
        x_sample, pos_sample, state_ssm_re, state_ssm_im, cache_latent, cache_k_rope, page_table, params, g_final)
    return (y_prompt, y_sample, ckv_prompt, krope_prompt, ckv_sample, krope_sample,
            ssm_re_prompt, ssm_im_prompt, ssm_re_sample, ssm_im_sample)
```

```python
import functools

import jax
import jax.numpy as jnp
from jax import lax
from jax.experimental import pallas as pl
from jax.experimental.pallas import tpu as pltpu

F32 = jnp.float32
BF16 = jnp.bfloat16

D_MODEL = 1024
D_FF = 2816
S5_WIDTH = 512
S5_GROUP = 16
S5_GROUPS = 32
S5_STATE = 64
S5_COLS = S5_GROUPS * S5_STATE
MLA_HEADS = 8
QK_NOPE = 64
QK_ROPE = 32
V_DIM = 64
Q_LORA = 384
KV_LORA = 256
ROPE_BASE = 10000.0
NORM_EPS = 1e-6
PAGE_SIZE = 128
ATTN_SCALE = (QK_NOPE + QK_ROPE) ** -0.5
ROPE_LANES = MLA_HEADS * QK_ROPE
LAT_LANES = MLA_HEADS * KV_LORA
IN_EXT = S5_WIDTH + Q_LORA + KV_LORA + 2 * ROPE_LANES
OFF_CQ = S5_WIDTH
OFF_CKV = OFF_CQ + Q_LORA
OFF_KR = OFF_CKV + KV_LORA
OFF_KRS = OFF_KR + ROPE_LANES
NEG_BIG = -1e30

V7X_VMEM_LIMIT_BYTES = 56 * 1024 * 1024
TOKEN_TILE = 256
S5_TIME_TILE = 32
S5_COL_TILE = 512
ATTN_TILE = 256
SAMPLE_PAGES_PER_STEP = 16


def _rms(x, g):
    return x * lax.rsqrt(jnp.mean(x * x, axis=-1, keepdims=True) + NORM_EPS) * g


def _dot(a, b):
    return jnp.dot(a, b, preferred_element_type=F32)


def _dot_nt(a, b):
    return lax.dot_general(a, b, (((1,), (1,)), ((), ())), preferred_element_type=F32)


def _ffn_half_step(x, g_ref, wg_ref, wu_ref, wd_ref):
    xn = _rms(x, g_ref[...]).astype(BF16)
    gate = _dot(xn, wg_ref[...])
    up = _dot(xn, wu_ref[...])
    hid = (jax.nn.silu(gate) * up).astype(BF16)
    return x + 0.5 * _dot(hid, wd_ref[...])


def _resident(shape):
    nd = len(shape)
    return pl.BlockSpec(shape, lambda *_: (0,) * nd, pipeline_mode=pl.Buffered(1))


def _pre_mixer_kernel(x_ref, cos_ref, sin_ref, g1_ref, wg_ref, wu_ref, wd_ref, gm_ref, win_ref,
                      gq_ref, gkv_ref, wuq_ref, wuk_ref,
                      h1_ref, u_ref, ckv_ref, krope_ref, kcat_ref, qlat_ref, qrope_ref):
    h1 = _ffn_half_step(x_ref[...], g1_ref, wg_ref, wu_ref, wd_ref)
    h1_ref[...] = h1
    xm = _rms(h1, gm_ref[...]).astype(BF16)
    proj = _dot(xm, win_ref[...])
    u_ref[...] = proj[:, :S5_WIDTH]
    cqn = _rms(proj[:, OFF_CQ:OFF_CKV], gq_ref[...]).astype(BF16)
    ckv = _rms(proj[:, OFF_CKV:OFF_KR], gkv_ref[...])
    ckv_ref[...] = ckv
    cos = cos_ref[...]
    sin = sin_ref[...]
    kr8 = proj[:, OFF_KR:OFF_KRS] * cos + proj[:, OFF_KRS:IN_EXT] * sin
    krope_ref[...] = kr8[:, :QK_ROPE]
    kcat_ref[:, :KV_LORA] = ckv.astype(BF16)
    kcat_ref[:, KV_LORA:] = kr8.astype(BF16)
    q = _dot(cqn, wuq_ref[...])
    n_nope = MLA_HEADS * QK_NOPE
    qrope_ref[...] = (q[:, n_nope:n_nope + ROPE_LANES] * cos
                      + q[:, n_nope + ROPE_LANES:] * sin).astype(BF16)
    qn = q[:, :n_nope].astype(BF16)
    for h in range(MLA_HEADS):
        pair = h // 2
        qlat_ref[:, KV_LORA * h:KV_LORA * (h + 1)] = _dot(
            qn[:, 128 * pair:128 * (pair + 1)], wuk_ref[h]).astype(BF16)


def _pre_mixer(x, cos_tab, sin_tab, w, *, seq_major_u):
    m = x.shape[0]
    tm = TOKEN_TILE
    nsteps = m // tm
    tab_blocks = cos_tab.shape[0] // tm
    row = lambda i: (i, 0)
    tab = lambda i: (i % tab_blocks, 0)
    if seq_major_u:
        blocks_per_seq = seq_major_u // tm
        u_shape = (seq_major_u, (m // seq_major_u) * S5_WIDTH)
        u_spec = pl.BlockSpec((tm, S5_WIDTH), lambda i: (i % blocks_per_seq, i // blocks_per_seq))
    else:
        u_shape = (m, S5_WIDTH)
        u_spec = pl.BlockSpec((tm, S5_WIDTH), row)
    out_shape = (
        jax.ShapeDtypeStruct((m, D_MODEL), F32),
        jax.ShapeDtypeStruct(u_shape, F32),
        jax.ShapeDtypeStruct((m, KV_LORA), F32),
        jax.ShapeDtypeStruct((m, QK_ROPE), F32),
        jax.ShapeDtypeStruct((m, KV_LORA + ROPE_LANES), BF16),
        jax.ShapeDtypeStruct((m, LAT_LANES), BF16),
        jax.ShapeDtypeStruct((m, ROPE_LANES), BF16),
    )
    out_specs = (
        pl.BlockSpec((tm, D_MODEL), row),
        u_spec,
        pl.BlockSpec((tm, KV_LORA), row),
        pl.BlockSpec((tm, QK_ROPE), row),
        pl.BlockSpec((tm, KV_LORA + ROPE_LANES), row),
        pl.BlockSpec((tm, LAT_LANES), row),
        pl.BlockSpec((tm, ROPE_LANES), row),
    )
    weights = (w['g_ffn1'], w['wg1'], w['wu1'], w['wd1'], w['g_mix'], w['w_in_ext'],
               w['g_q'], w['g_kv'], w['w_uq_ext'], w['w_uk_pad'])
    in_specs = [pl.BlockSpec((tm, D_MODEL), row),
                pl.BlockSpec((tm, ROPE_LANES), tab),
                pl.BlockSpec((tm, ROPE_LANES), tab)] + [_resident(a.shape) for a in weights]
    return pl.pallas_call(
        _pre_mixer_kernel, out_shape=out_shape, grid=(nsteps,), in_specs=in_specs, out_specs=out_specs,
        compiler_params=pltpu.CompilerParams(dimension_semantics=("parallel",),
                                             vmem_limit_bytes=V7X_VMEM_LIMIT_BYTES),
        name="pre_mixer",
    )(x, cos_tab, sin_tab, *weights)


def _s5_readout(h_all, u, cmat_ref, d_ref, wglu_ref, bglu_ref, waout_ref):
    y = _dot(h_all.astype(BF16), cmat_ref[...]) + d_ref[...] * u
    z = jax.nn.gelu(y)
    gate = _dot(z.astype(BF16), wglu_ref[...]) + bglu_ref[...]
    return _dot((z * jax.nn.sigmoid(gate)).astype(BF16), waout_ref[...])


def _s5_prompt_kernel(u_ref, bmat_ref, are_ref, aim_ref, cmat_ref, d_ref, wglu_ref, bglu_ref, waout_ref,
                      ya_ref, sre_ref, sim_ref, xs_ref, hre_ref, him_ref):
    tt, nb, _ = u_ref.shape
    rows = tt * nb

    @pl.when(pl.program_id(0) == 0)
    def _():
        hre_ref[...] = jnp.zeros_like(hre_ref)
        him_ref[...] = jnp.zeros_like(him_ref)

    u = u_ref[...].reshape(rows, S5_WIDTH)
    xs_ref[...] = _dot(u.astype(BF16), bmat_ref[...])
    for cb in range(S5_COLS // S5_COL_TILE):
        lo = cb * S5_COL_TILE
        re_cols = pl.ds(lo, S5_COL_TILE)
        im_cols = pl.ds(S5_COLS + lo, S5_COL_TILE)
        ar = jnp.broadcast_to(are_ref[:, re_cols], (nb, S5_COL_TILE))
        ai = jnp.broadcast_to(aim_ref[:, re_cols], (nb, S5_COL_TILE))

        def body(t, carry):
            hr, hi = carry
            r = pl.ds(pl.multiple_of(t * nb, nb), nb)
            nr = ar * hr - ai * hi + xs_ref[r, re_cols]
            ni = ar * hi + ai * hr + xs_ref[r, im_cols]
            xs_ref[r, re_cols] = nr
            xs_ref[r, im_cols] = ni
            return nr, ni

        hr, hi = lax.fori_loop(0, tt, body, (hre_ref[:, re_cols], him_ref[:, re_cols]), unroll=4)
        hre_ref[:, re_cols] = hr
        him_ref[:, re_cols] = hi
    ya = _s5_readout(xs_ref[...], u, cmat_ref, d_ref, wglu_ref, bglu_ref, waout_ref)
    ya_ref[...] = ya.reshape(tt, nb, D_MODEL)
    sre_ref[...] = hre_ref[...]
    sim_ref[...] = him_ref[...]


def _s5_prompt(u3, w):
    seq, nb, _ = u3.shape
    tt = S5_TIME_TILE
    weights = (w['bmat'], w['a_re'], w['a_im'], w['cmat'], w['s5_d'], w['w_glu'], w['b_glu'], w['w_a_out'])
    state = pl.BlockSpec((nb, S5_COLS), lambda i: (0, 0))
    return pl.pallas_call(
        _s5_prompt_kernel,
        out_shape=(jax.ShapeDtypeStruct((seq, nb, D_MODEL), F32),
                   jax.ShapeDtypeStruct((nb, S5_COLS), F32),
                   jax.ShapeDtypeStruct((nb, S5_COLS), F32)),
        grid=(seq // tt,),
        in_specs=[pl.BlockSpec((tt, nb, S5_WIDTH), lambda i: (i, 0, 0))] + [_resident(a.shape) for a in weights],
        out_specs=(pl.BlockSpec((tt, nb, D_MODEL), lambda i: (i, 0, 0)), state, state),
        scratch_shapes=[pltpu.VMEM((tt * nb, 2 * S5_COLS), F32),
                        pltpu.VMEM((nb, S5_COLS), F32),
                        pltpu.VMEM((nb, S5_COLS), F32)],
        compiler_params=pltpu.CompilerParams(dimension_semantics=("arbitrary",),
                                             vmem_limit_bytes=V7X_VMEM_LIMIT_BYTES),
        name="s5_prompt",
    )(u3, *weights)


def _s5_sample_kernel(u_ref, h0re_ref, h0im_ref, bmat_ref, are_ref, aim_ref, cmat_ref, d_ref, wglu_ref,
                      bglu_ref, waout_ref, ya_ref, sre_ref, sim_ref, hs_ref):
    tt, nb, _ = u_ref.shape
    ar = are_ref[...]
    ai = aim_ref[...]
    hr = h0re_ref[...]
    hi = h0im_ref[...]
    for t in range(tt):
        x = _dot(u_ref[t].astype(BF16), bmat_ref[...])
        hr, hi = (ar * hr - ai * hi + x[:, :S5_COLS], ar * hi + ai * hr + x[:, S5_COLS:])
        hs_ref[t * nb:(t + 1) * nb, :S5_COLS] = hr
        hs_ref[t * nb:(t + 1) * nb, S5_COLS:] = hi
    sre_ref[...] = hr
    sim_ref[...] = hi
    u = u_ref[...].reshape(tt * nb, S5_WIDTH)
    ya = _s5_readout(hs_ref[...], u, cmat_ref, d_ref, wglu_ref, bglu_ref, waout_ref)
    ya_ref[...] = ya.reshape(tt, nb, D_MODEL)


def _s5_sample(u3, h0_re, h0_im, w):
    tt, nb, _ = u3.shape
    weights = (w['bmat'], w['a_re'], w['a_im'], w['cmat'], w['s5_d'], w['w_glu'], w['b_glu'], w['w_a_out'])
    args = (u3, h0_re, h0_im) + weights
    whole = lambda a: pl.BlockSpec(a.shape, lambda i, nd=a.ndim: (0,) * nd)
    outs = (jax.ShapeDtypeStruct((tt, nb, D_MODEL), F32),
            jax.ShapeDtypeStruct((nb, S5_COLS), F32),
            jax.ShapeDtypeStruct((nb, S5_COLS), F32))
    return pl.pallas_call(
        _s5_sample_kernel, out_shape=outs, grid=(1,),
        in_specs=[whole(a) for a in args], out_specs=tuple(whole(o) for o in outs),
        scratch_shapes=[pltpu.VMEM((tt * nb, 2 * S5_COLS), F32)],
        compiler_params=pltpu.CompilerParams(dimension_semantics=("arbitrary",),
                                             vmem_limit_bytes=V7X_VMEM_LIMIT_BYTES),
        name="s5_sample",
    )(*args)


def _softmax_block_update(s, v, m_ref, l_ref, acc_ref):
    m_old = m_ref[...]
    m_new = jnp.maximum(m_old, jnp.max(s, axis=-1, keepdims=True))
    alpha = jnp.exp(m_old - m_new)
    p = jnp.exp(s - m_new)
    l_ref[...] = alpha * l_ref[...] + jnp.sum(p, axis=-1, keepdims=True)
    acc_ref[...] = alpha * acc_ref[...] + _dot(p.astype(BF16), v)
    m_ref[...] = m_new


def _attn_prompt_kernel(qlat_ref, qrope_ref, kcat_ref, o_ref, qs_ref, m_ref, l_ref, acc_ref):
    tq = ATTN_TILE
    qi = pl.program_id(1)
    lane_head = lax.broadcasted_iota(jnp.int32, (tq, ROPE_LANES), 1) // QK_ROPE
    qr = qrope_ref[...]
    for h in range(MLA_HEADS):
        rows = pl.ds(h * tq, tq)
        qs_ref[rows, :KV_LORA] = qlat_ref[:, KV_LORA * h:KV_LORA * (h + 1)]
        qs_ref[rows, KV_LORA:] = jnp.where(lane_head == h, qr, jnp.zeros_like(qr))
    m_ref[...] = jnp.full_like(m_ref, NEG_BIG)
    l_ref[...] = jnp.zeros_like(l_ref)
    acc_ref[...] = jnp.zeros_like(acc_ref)

    def block(k, causal):
        kc = kcat_ref[pl.ds(pl.multiple_of(k * tq, tq), tq), :]
        s = _dot_nt(qs_ref[...], kc) * ATTN_SCALE
        if causal:
            r = lax.broadcasted_iota(jnp.int32, s.shape, 0) & (tq - 1)
            c = lax.broadcasted_iota(jnp.int32, s.shape, 1)
            s = jnp.where(c <= r, s, NEG_BIG)
        _softmax_block_update(s, kc[:, :KV_LORA], m_ref, l_ref, acc_ref)

    def past(k, carry):
        block(k, False)
        return carry

    lax.fori_loop(0, qi, past, 0)
    block(qi, True)
    o = acc_ref[...] / l_ref[...]
    for h in range(MLA_HEADS):
        o_ref[:, KV_LORA * h:KV_LORA * (h + 1)] = o[h * tq:(h + 1) * tq].astype(BF16)


def _attn_prompt(qlat, qrope, kcat, *, batch, seq):
    tq = ATTN_TILE
    nq = seq // tq
    rows = MLA_HEADS * tq
    qmap = lambda b, i: (b * nq + i, 0)
    return pl.pallas_call(
        _attn_prompt_kernel,
        out_shape=jax.ShapeDtypeStruct((batch * seq, LAT_LANES), BF16),
        grid=(batch, nq),
        in_specs=[pl.BlockSpec((tq, LAT_LANES), qmap),
                  pl.BlockSpec((tq, ROPE_LANES), qmap),
                  pl.BlockSpec((seq, KV_LORA + ROPE_LANES), lambda b, i: (b, 0))],
        out_specs=pl.BlockSpec((tq, LAT_LANES), qmap),
        scratch_shapes=[pltpu.VMEM((rows, KV_LORA + ROPE_LANES), BF16),
                        pltpu.VMEM((rows, 1), F32),
                        pltpu.VMEM((rows, 1), F32),
                        pltpu.VMEM((rows, KV_LORA), F32)],
        compiler_params=pltpu.CompilerParams(dimension_semantics=("parallel", "arbitrary"),
                                             vmem_limit_bytes=V7X_VMEM_LIMIT_BYTES),
        name="attn_prompt",
    )(qlat, qrope, kcat)


def _attn_sample_kernel(pt_ref, qlat_ref, qrope_ref, ckvn_ref, kropen_ref, cl_hbm, cr_hbm, o_ref,
                        kbuf, rbuf, sem, qs_ref, qr_ref, kn_ref, rn_ref, m_ref, l_ref, acc_ref,
                        *, n_pages, n_new):
    pg = SAMPLE_PAGES_PER_STEP
    chunks = n_pages // pg
    g = pl.program_id(0)
    n_steps = pl.num_programs(0)
    c = g % chunks
    slot = g % 2

    def page_copies(step, dst_slot):
        first = (step // chunks) * n_pages + (step % chunks) * pg
        copies = []
        for p in range(pg):
            page = pt_ref[first + p]
            rows = pl.ds(p * PAGE_SIZE, PAGE_SIZE)
            copies.append(pltpu.make_async_copy(cl_hbm.at[page], kbuf.at[dst_slot, rows], sem.at[0, dst_slot]))
            copies.append(pltpu.make_async_copy(cr_hbm.at[page], rbuf.at[dst_slot, rows], sem.at[1, dst_slot]))
        return copies

    @pl.when(g == 0)
    def _():
        for cp in page_copies(g, slot):
            cp.start()

    @pl.when(g + 1 < n_steps)
    def _():
        for cp in page_copies(g + 1, 1 - slot):
            cp.start()

    @pl.when(c == 0)
    def _():
        ql = qlat_ref[0].astype(F32)
        qr = qrope_ref[0].astype(F32)
        for h in range(MLA_HEADS):
            rows = pl.ds(h * n_new, n_new)
            qs_ref[rows, :] = ql[:, KV_LORA * h:KV_LORA * (h + 1)]
            qr_ref[rows, :] = qr[:, QK_ROPE * h:QK_ROPE * (h + 1)]
        m_ref[...] = jnp.full_like(m_ref, NEG_BIG)
        l_ref[...] = jnp.zeros_like(l_ref)
        acc_ref[...] = jnp.zeros_like(acc_ref)

    for cp in page_copies(g, slot):
        cp.wait()

    qs = qs_ref[...].astype(BF16)
    qr = qr_ref[...].astype(BF16)
    kc = kbuf[slot].astype(BF16)
    kr = rbuf[slot].astype(BF16)
    s = (_dot_nt(qs, kc) + _dot_nt(qr, kr)) * ATTN_SCALE
    _softmax_block_update(s, kc, m_ref, l_ref, acc_ref)

    @pl.when(c == chunks - 1)
    def _():
        kn_ref[...] = jnp.zeros_like(kn_ref)
        rn_ref[...] = jnp.zeros_like(rn_ref)
        kn_ref[pl.ds(0, n_new), :] = ckvn_ref[0]
        rn_ref[pl.ds(0, n_new), :] = kropen_ref[0]
        kn = kn_ref[...].astype(BF16)
        sn = (_dot_nt(qs, kn) + _dot_nt(qr, rn_ref[...].astype(BF16))) * ATTN_SCALE
        t = lax.broadcasted_iota(jnp.int32, sn.shape, 0) % n_new
        j = lax.broadcasted_iota(jnp.int32, sn.shape, 1)
        sn = jnp.where(j <= t, sn, NEG_BIG)
        _softmax_block_update(sn, kn, m_ref, l_ref, acc_ref)
        o_ref[0] = acc_ref[...] / l_ref[...]


def _attn_sample(page_table, qlat, qrope, ckv_new, krope_new, cache_latent, cache_k_rope):
    nb, n_pages = page_table.shape
    n_new = qlat.shape[0] // nb
    pg = SAMPLE_PAGES_PER_STEP
    chunks = n_pages // pg
    rows = MLA_HEADS * n_new
    new_pad = 8
    per_batch = lambda g, pt: (g // chunks, 0, 0)
    grid_spec = pltpu.PrefetchScalarGridSpec(
        num_scalar_prefetch=1,
        grid=(nb * chunks,),
        in_specs=[pl.BlockSpec((1, n_new, LAT_LANES), per_batch),
                  pl.BlockSpec((1, n_new, ROPE_LANES), per_batch),
                  pl.BlockSpec((1, n_new, KV_LORA), per_batch),
                  pl.BlockSpec((1, n_new, QK_ROPE), per_batch),
                  pl.BlockSpec(memory_space=pl.ANY),
                  pl.BlockSpec(memory_space=pl.ANY)],
        out_specs=pl.BlockSpec((1, rows, KV_LORA), per_batch),
        scratch_shapes=[pltpu.VMEM((2, pg * PAGE_SIZE, KV_LORA), F32),
                        pltpu.VMEM((2, pg * PAGE_SIZE, QK_ROPE), F32),
                        pltpu.SemaphoreType.DMA((2, 2)),
                        pltpu.VMEM((rows, KV_LORA), F32),
                        pltpu.VMEM((rows, QK_ROPE), F32),
                        pltpu.VMEM((new_pad, KV_LORA), F32),
                        pltpu.VMEM((new_pad, QK_ROPE), F32),
                        pltpu.VMEM((rows, 1), F32),
                        pltpu.VMEM((rows, 1), F32),
                        pltpu.VMEM((rows, KV_LORA), F32)])
    return pl.pallas_call(
        functools.partial(_attn_sample_kernel, n_pages=n_pages, n_new=n_new),
        out_shape=jax.ShapeDtypeStruct((nb, rows, KV_LORA), F32),
        grid_spec=grid_spec,
        compiler_params=pltpu.CompilerParams(dimension_semantics=("arbitrary",),
                                             vmem_limit_bytes=V7X_VMEM_LIMIT_BYTES),
        name="attn_sample",
    )(page_table.reshape(-1), qlat.reshape(nb, n_new, LAT_LANES), qrope.reshape(nb, n_new, ROPE_LANES),
      ckv_new.reshape(nb, n_new, KV_LORA), krope_new.reshape(nb, n_new, QK_ROPE), cache_latent, cache_k_rope)


def _post_mixer_kernel(h1_ref, ya_ref, ol_ref, gm_ref, wgt_ref, wuv_ref, wbo_ref, wout_ref,
                       g2_ref, wg_ref, wu_ref, wd_ref, gf_ref, y_ref):
    h1 = h1_ref[...]
    xm = _rms(h1, gm_ref[...]).astype(BF16)
    gates = _dot(xm, wgt_ref[...])
    ol = ol_ref[...]
    o = _dot(ol[:, :KV_LORA], wuv_ref[0])
    for h in range(1, MLA_HEADS):
        o = o + _dot(ol[:, KV_LORA * h:KV_LORA * (h + 1)], wuv_ref[h])
    yb = _dot(o.astype(BF16), wbo_ref[...])
    merged = jax.nn.sigmoid(gates[:, :D_MODEL]) * ya_ref[...] + jax.nn.sigmoid(gates[:, D_MODEL:]) * yb
    h2 = h1 + _dot(merged.astype(BF16), wout_ref[...])
    h3 = _ffn_half_step(h2, g2_ref, wg_ref, wu_ref, wd_ref)
    y_ref[...] = _rms(h3, gf_ref[...])


def _post_mixer(h1, ya, olat, w, *, seq_major_ya):
    m = h1.shape[0]
    tm = TOKEN_TILE
    row = lambda i: (i, 0)
    if seq_major_ya:
        blocks_per_seq = seq_major_ya // tm
        ya_spec = pl.BlockSpec((tm, D_MODEL), lambda i: (i % blocks_per_seq, i // blocks_per_seq))
    else:
        ya_spec = pl.BlockSpec((tm, D_MODEL), row)
    weights = (w['g_mix'], w['w_gates'], w['w_uv_pad'], w['w_b_out'], w['w_out'],
               w['g_ffn2'], w['wg2'], w['wu2'], w['wd2'], w['g_final'])
    return pl.pallas_call(
        _post_mixer_kernel,
        out_shape=jax.ShapeDtypeStruct((m, D_MODEL), F32),
        grid=(m // tm,),
        in_specs=[pl.BlockSpec((tm, D_MODEL), row), ya_spec, pl.BlockSpec((tm, LAT_LANES), row)]
        + [_resident(a.shape) for a in weights],
        out_specs=pl.BlockSpec((tm, D_MODEL), row),
        compiler_params=pltpu.CompilerParams(dimension_semantics=("parallel",),
                                             vmem_limit_bytes=V7X_VMEM_LIMIT_BYTES),
        name="post_mixer",
    )(h1, ya, olat, *weights)


def _rope_tables(pos):
    inv_freq = 1.0 / (ROPE_BASE ** (jnp.arange(0, QK_ROPE, 2, dtype=F32) / QK_ROPE))
    ang = pos.astype(F32)[:, None] * inv_freq[None, :]
    cos, sin = jnp.cos(ang), jnp.sin(ang)
    cos_tab = jnp.tile(jnp.concatenate([cos, cos], axis=-1), (1, MLA_HEADS))
    sin_tab = jnp.tile(jnp.concatenate([-sin, sin], axis=-1), (1, MLA_HEADS))
    return cos_tab, sin_tab


def _swap_halves(a):
    half = QK_ROPE // 2
    return jnp.concatenate([a[..., half:], a[..., :half]], axis=-1)


def _block_diag(blocks):
    g, r, c = blocks.shape
    eye = jnp.eye(g, dtype=blocks.dtype)
    return jnp.einsum('grc,gk->grkc', blocks, eye).reshape(g * r, g * c)


def _prepare_weights(p):
    w = {}
    vec = lambda a: a.reshape(1, -1).astype(F32)
    for name in ('g_ffn1', 'g_mix', 'g_q', 'g_kv', 'g_ffn2', 'g_final', 'b_glu'):
        w[name] = vec(p[name])
    for src, dst in (('w_ffn1_gate', 'wg1'), ('w_ffn1_up', 'wu1'), ('w_ffn1_down', 'wd1'),
                     ('w_ffn2_gate', 'wg2'), ('w_ffn2_up', 'wu2'), ('w_ffn2_down', 'wd2'),
                     ('w_glu', 'w_glu'), ('w_a_out', 'w_a_out'), ('w_b_out', 'w_b_out'), ('w_out', 'w_out')):
        w[dst] = p[src].astype(BF16)
    w_in = p['w_in']
    off_kr = S5_WIDTH + Q_LORA + KV_LORA
    k_r = w_in[:, off_kr:off_kr + QK_ROPE]
    w['w_in_ext'] = jnp.concatenate(
        [w_in[:, :off_kr], jnp.tile(k_r, (1, MLA_HEADS)), jnp.tile(_swap_halves(k_r), (1, MLA_HEADS))],
        axis=1).astype(BF16)
    w['w_gates'] = w_in[:, off_kr + QK_ROPE:].astype(BF16)
    w_uq = p['w_uq']
    uq_rope = w_uq[:, :, QK_NOPE:]
    w['w_uq_ext'] = jnp.concatenate(
        [w_uq[:, :, :QK_NOPE].reshape(Q_LORA, -1), uq_rope.reshape(Q_LORA, -1),
         _swap_halves(uq_rope).reshape(Q_LORA, -1)], axis=1).astype(BF16)
    uk = jnp.transpose(p['w_uk'], (1, 2, 0))
    uk_pad = jnp.zeros((MLA_HEADS, 2, QK_NOPE, KV_LORA), F32)
    uk_pad = uk_pad.at[jnp.arange(MLA_HEADS), jnp.arange(MLA_HEADS) % 2].set(uk)
    w['w_uk_pad'] = uk_pad.reshape(MLA_HEADS, 2 * QK_NOPE, KV_LORA).astype(BF16)
    uv = jnp.transpose(p['w_uv'], (1, 0, 2))
    uv_pad = jnp.zeros((MLA_HEADS, KV_LORA, MLA_HEADS, V_DIM), F32)
    uv_pad = uv_pad.at[jnp.arange(MLA_HEADS), :, jnp.arange(MLA_HEADS)].set(uv)
    w['w_uv_pad'] = uv_pad.reshape(MLA_HEADS, KV_LORA, MLA_HEADS * V_DIM).astype(BF16)
    lam_re, lam_im = p['s5_a_re'].astype(F32), p['s5_a_im'].astype(F32)
    dt = jnp.exp(p['s5_log_dt'].astype(F32))[:, None]
    mag = jnp.exp(lam_re * dt)
    ab_re, ab_im = mag * jnp.cos(lam_im * dt), mag * jnp.sin(lam_im * dt)
    den = lam_re * lam_re + lam_im * lam_im
    num_re, num_im = ab_re - 1.0, ab_im
    k_re = (num_re * lam_re + num_im * lam_im) / den
    k_im = (num_im * lam_re - num_re * lam_im) / den
    w['a_re'] = ab_re.reshape(1, S5_COLS)
    w['a_im'] = ab_im.reshape(1, S5_COLS)
    b_re, b_im = p['s5_b_re'].astype(F32), p['s5_b_im'].astype(F32)
    kb_re = k_re[..., None] * b_re - k_im[..., None] * b_im
    kb_im = k_re[..., None] * b_im + k_im[..., None] * b_re
    to_in = lambda a: _block_diag(jnp.transpose(a, (0, 2, 1)))
    w['bmat'] = jnp.concatenate([to_in(kb_re), to_in(kb_im)], axis=1).astype(BF16)
    to_out = lambda a: _block_diag(jnp.transpose(a, (0, 2, 1)))
    w['cmat'] = jnp.concatenate([to_out(p['s5_c_re'].astype(F32)), -to_out(p['s5_c_im'].astype(F32))],
                                axis=0).astype(BF16)
    w['s5_d'] = vec(p['s5_d'])
    return w


def kernel(x_prompt, x_sample, cache_latent, cache_k_rope, state_ssm_re, state_ssm_im, page_table, g_ffn1, w_ffn1_gate, w_ffn1_up, w_ffn1_down, g_mix, w_in, s5_a_re, s5_a_im, s5_log_dt, s5_b_re, s5_b_im, s5_c_re, s5_c_im, s5_d, w_glu, b_glu, w_a_out, g_q, w_uq, g_kv, w_uk, w_uv, w_b_out, w_out, g_ffn2, w_ffn2_gate, w_ffn2_up, w_ffn2_down, g_final):
    layer = 0
    params = dict(
        g_ffn1=g_ffn1, w_ffn1_gate=w_ffn1_gate, w_ffn1_up=w_ffn1_up, w_ffn1_down=w_ffn1_down,
        g_mix=g_mix, w_in=w_in, s5_a_re=s5_a_re, s5_a_im=s5_a_im, s5_log_dt=s5_log_dt,
        s5_b_re=s5_b_re, s5_b_im=s5_b_im, s5_c_re=s5_c_re, s5_c_im=s5_c_im, s5_d=s5_d,
        w_glu=w_glu, b_glu=b_glu, w_a_out=w_a_out, g_q=g_q, w_uq=w_uq, g_kv=g_kv, w_uk=w_uk,
        w_uv=w_uv, w_b_out=w_b_out, w_out=w_out, g_ffn2=g_ffn2, w_ffn2_gate=w_ffn2_gate,
        w_ffn2_up=w_ffn2_up, w_ffn2_down=w_ffn2_down)
    p = {k: v[layer] for k, v in params.items()}
    p['g_final'] = g_final
    w = _prepare_weights(p)

    nb, seq, _ = x_prompt.shape
    cos_p, sin_p = _rope_tables(jnp.arange(seq))
    h1, u, ckv, krope, kcat, qlat, qrope = _pre_mixer(
        x_prompt.reshape(nb * seq, D_MODEL), cos_p, sin_p, w, seq_major_u=seq)
    ya, sre_p, sim_p = _s5_prompt(u.reshape(seq, nb, S5_WIDTH), w)
    olat = _attn_prompt(qlat, qrope, kcat, batch=nb, seq=seq)
    y_prompt = _post_mixer(h1, ya.reshape(seq, nb * D_MODEL), olat, w, seq_major_ya=seq).reshape(nb, seq, D_MODEL)
    ckv_prompt = ckv.reshape(1, nb, seq, KV_LORA)
    krope_prompt = krope.reshape(1, nb, seq, QK_ROPE)
    group = lambda a: a.reshape(1, a.shape[0], S5_GROUPS, S5_STATE)

    sb, n_new, _ = x_sample.shape
    n_pages = page_table.shape[1]
    past_len = n_pages * PAGE_SIZE
    cos_s, sin_s = _rope_tables(past_len + jnp.arange(n_new))
    cos_s, sin_s = jnp.tile(cos_s, (sb, 1)), jnp.tile(sin_s, (sb, 1))
    h1s, us, ckvs, kropes, _, qlats, qropes = _pre_mixer(
        x_sample.reshape(sb * n_new, D_MODEL), cos_s, sin_s, w, seq_major_u=0)
    us_t = jnp.transpose(us.reshape(sb, n_new, S5_WIDTH), (1, 0, 2))
    yas_t, sre_s, sim_s = _s5_sample(us_t, state_ssm_re[layer].reshape(sb, S5_COLS),
                                     state_ssm_im[layer].reshape(sb, S5_COLS), w)
    yas = jnp.transpose(yas_t, (1, 0, 2)).reshape(sb * n_new, D_MODEL)
    ol = _attn_sample(page_table, qlats, qropes, ckvs, kropes, cache_latent[layer], cache_k_rope[layer])
    olats = jnp.transpose(ol.reshape(sb, MLA_HEADS, n_new, KV_LORA), (0, 2, 1, 3))
    olats = olats.reshape(sb * n_new, LAT_LANES).astype(BF16)
    y_sample = _post_mixer(h1s, yas, olats, w, seq_major_ya=0).reshape(sb, n_new, D_MODEL)

    return (y_prompt, y_sample, ckv_prompt, krope_prompt,
            ckvs.reshape(1, sb, n_new, KV_LORA), kropes.reshape(1, sb, n_new, QK_ROPE),
            group(sre_p), group(sim_p), group(sre_s), group(sim_s))
```

```python
import functools

import jax
import jax.numpy as jnp
from jax import lax
from jax.experimental import pallas as pl
from jax.experimental.pallas import tpu as pltpu

F32 = jnp.float32
BF16 = jnp.bfloat16

D_MODEL = 1024
D_FF = 2816
S5_WIDTH = 512
S5_GROUP = 16
S5_GROUPS = 32
S5_STATE = 64
S5_COLS = S5_GROUPS * S5_STATE
MLA_HEADS = 8
QK_NOPE = 64
QK_ROPE = 32
V_DIM = 64
Q_LORA = 384
KV_LORA = 256
ROPE_BASE = 10000.0
NORM_EPS = 1e-6
PAGE_SIZE = 128
ATTN_SCALE = (QK_NOPE + QK_ROPE) ** -0.5
Q_PRESCALE = ATTN_SCALE * 1.4426950408889634
LANES = 128
ROPE_LANES = MLA_HEADS * QK_ROPE
LAT_LANES = MLA_HEADS * KV_LORA
IN_EXT = S5_WIDTH + Q_LORA + KV_LORA + 2 * ROPE_LANES
OFF_CQ = S5_WIDTH
OFF_CKV = OFF_CQ + Q_LORA
OFF_KR = OFF_CKV + KV_LORA
OFF_KRS = OFF_KR + ROPE_LANES
NEG_BIG = -1e30

V7X_VMEM_LIMIT_BYTES = 56 * 1024 * 1024
TOKEN_TILE = 256
S5_TIME_TILE = 32
S5_COL_TILE = 512
ATTN_TILE = 512
ATTN_ROW_SPLIT = 4
SAMPLE_PAGES_PER_STEP = 32
SAMPLE_SUBCHUNKS = 4
SAMPLE_SLOTS = 3


def _rms(x, g):
    return x * lax.rsqrt(jnp.mean(x * x, axis=-1, keepdims=True) + NORM_EPS) * g


def _dot(a, b):
    return jnp.dot(a, b, preferred_element_type=F32)


def _dot_nt(a, b):
    return lax.dot_general(a, b, (((1,), (1,)), ((), ())), preferred_element_type=F32)


def _ffn_half_step(x, g_ref, wg_ref, wu_ref, wd_ref):
    xn = _rms(x, g_ref[...]).astype(BF16)
    gate = _dot(xn, wg_ref[...])
    up = _dot(xn, wu_ref[...])
    hid = (jax.nn.silu(gate) * up).astype(BF16)
    return x + 0.5 * _dot(hid, wd_ref[...])


def _resident(shape):
    nd = len(shape)
    return pl.BlockSpec(shape, lambda *_: (0,) * nd, pipeline_mode=pl.Buffered(1))


def _pre_mixer_kernel(x_ref, cos_ref, sin_ref, g1_ref, wg_ref, wu_ref, wd_ref, gm_ref, win_ref,
                      gq_ref, gkv_ref, wuq_ref, wuk_ref,
                      h1_ref, u_ref, ckv_ref, krope_ref, kcat_ref, qlat_ref, qrope_ref):
    h1 = _ffn_half_step(x_ref[...], g1_ref, wg_ref, wu_ref, wd_ref)
    h1_ref[...] = h1
    xm = _rms(h1, gm_ref[...]).astype(BF16)
    proj = _dot(xm, win_ref[...])
    u_ref[...] = proj[:, :S5_WIDTH]
    cqn = _rms(proj[:, OFF_CQ:OFF_CKV], gq_ref[...]).astype(BF16)
    ckv = _rms(proj[:, OFF_CKV:OFF_KR], gkv_ref[...])
    ckv_ref[...] = ckv
    cos = cos_ref[...]
    sin = sin_ref[...]
    kr8 = proj[:, OFF_KR:OFF_KRS] * cos + proj[:, OFF_KRS:IN_EXT] * sin
    krope_ref[...] = kr8[:, :QK_ROPE]
    kcat_ref[:, :KV_LORA] = ckv.astype(BF16)
    kcat_ref[:, KV_LORA:] = kr8.astype(BF16)
    q = _dot(cqn, wuq_ref[...])
    n_nope = MLA_HEADS * QK_NOPE
    qrope_ref[...] = ((q[:, n_nope:n_nope + ROPE_LANES] * cos
                       + q[:, n_nope + ROPE_LANES:] * sin) * Q_PRESCALE).astype(BF16)
    qn = q[:, :n_nope].astype(BF16)
    for h in range(MLA_HEADS):
        pair = h // 2
        qlat_ref[:, KV_LORA * h:KV_LORA * (h + 1)] = (_dot(
            qn[:, LANES * pair:LANES * (pair + 1)], wuk_ref[h]) * Q_PRESCALE).astype(BF16)


def _pre_mixer(x, cos_tab, sin_tab, w, *, seq_major_u):
    m = x.shape[0]
    tm = TOKEN_TILE
    nsteps = m // tm
    tab_blocks = cos_tab.shape[0] // tm
    row = lambda i: (i, 0)
    tab = lambda i: (i % tab_blocks, 0)
    if seq_major_u:
        blocks_per_seq = seq_major_u // tm
        u_shape = (seq_major_u, (m // seq_major_u) * S5_WIDTH)
        u_spec = pl.BlockSpec((tm, S5_WIDTH), lambda i: (i % blocks_per_seq, i // blocks_per_seq))
    else:
        u_shape = (m, S5_WIDTH)
        u_spec = pl.BlockSpec((tm, S5_WIDTH), row)
    out_shape = (
        jax.ShapeDtypeStruct((m, D_MODEL), F32),
        jax.ShapeDtypeStruct(u_shape, F32),
        jax.ShapeDtypeStruct((m, KV_LORA), F32),
        jax.ShapeDtypeStruct((m, QK_ROPE), F32),
        jax.ShapeDtypeStruct((m, KV_LORA + ROPE_LANES), BF16),
        jax.ShapeDtypeStruct((m, LAT_LANES), BF16),
        jax.ShapeDtypeStruct((m, ROPE_LANES), BF16),
    )
    out_specs = (
        pl.BlockSpec((tm, D_MODEL), row),
        u_spec,
        pl.BlockSpec((tm, KV_LORA), row),
        pl.BlockSpec((tm, QK_ROPE), row),
        pl.BlockSpec((tm, KV_LORA + ROPE_LANES), row),
        pl.BlockSpec((tm, LAT_LANES), row),
        pl.BlockSpec((tm, ROPE_LANES), row),
    )
    weights = (w['g_ffn1'], w['wg1'], w['wu1'], w['wd1'], w['g_mix'], w['w_in_ext'],
               w['g_q'], w['g_kv'], w['w_uq_ext'], w['w_uk_pad'])
    in_specs = [pl.BlockSpec((tm, D_MODEL), row),
                pl.BlockSpec((tm, ROPE_LANES), tab),
                pl.BlockSpec((tm, ROPE_LANES), tab)] + [_resident(a.shape) for a in weights]
    return pl.pallas_call(
        _pre_mixer_kernel, out_shape=out_shape, grid=(nsteps,), in_specs=in_specs, out_specs=out_specs,
        compiler_params=pltpu.CompilerParams(dimension_semantics=("parallel",),
                                             vmem_limit_bytes=V7X_VMEM_LIMIT_BYTES),
        name="pre_mixer",
    )(x, cos_tab, sin_tab, *weights)


def _s5_readout(h_all, u, cmat_ref, d_ref, wglu_ref, bglu_ref, waout_ref):
    y = _dot(h_all.astype(BF16), cmat_ref[...]) + d_ref[...] * u
    z = jax.nn.gelu(y)
    gate = _dot(z.astype(BF16), wglu_ref[...]) + bglu_ref[...]
    return _dot((z * jax.nn.sigmoid(gate)).astype(BF16), waout_ref[...])


def _s5_prompt_kernel(u_ref, bmat_ref, are_ref, aim_ref, cmat_ref, d_ref, wglu_ref, bglu_ref, waout_ref,
                      ya_ref, sre_ref, sim_ref, xs_ref, hre_ref, him_ref):
    tt, nb, _ = u_ref.shape
    rows = tt * nb

    @pl.when(pl.program_id(0) == 0)
    def _():
        hre_ref[...] = jnp.zeros_like(hre_ref)
        him_ref[...] = jnp.zeros_like(him_ref)

    u = u_ref[...].reshape(rows, S5_WIDTH)
    xs_ref[...] = _dot(u.astype(BF16), bmat_ref[...])
    for cb in range(S5_COLS // S5_COL_TILE):
        lo = cb * S5_COL_TILE
        re_cols = pl.ds(lo, S5_COL_TILE)
        im_cols = pl.ds(S5_COLS + lo, S5_COL_TILE)
        ar = jnp.broadcast_to(are_ref[:, re_cols], (nb, S5_COL_TILE))
        ai = jnp.broadcast_to(aim_ref[:, re_cols], (nb, S5_COL_TILE))

        def body(t, carry):
            hr, hi = carry
            r = pl.ds(pl.multiple_of(t * nb, nb), nb)
            nr = ar * hr - ai * hi + xs_ref[r, re_cols]
            ni = ar * hi + ai * hr + xs_ref[r, im_cols]
            xs_ref[r, re_cols] = nr
            xs_ref[r, im_cols] = ni
            return nr, ni

        hr, hi = lax.fori_loop(0, tt, body, (hre_ref[:, re_cols], him_ref[:, re_cols]), unroll=4)
        hre_ref[:, re_cols] = hr
        him_ref[:, re_cols] = hi
    ya = _s5_readout(xs_ref[...], u, cmat_ref, d_ref, wglu_ref, bglu_ref, waout_ref)
    ya_ref[...] = ya.reshape(tt, nb, D_MODEL)
    sre_ref[...] = hre_ref[...]
    sim_ref[...] = him_ref[...]


def _s5_prompt(u3, w):
    seq, nb, _ = u3.shape
    tt = S5_TIME_TILE
    weights = (w['bmat'], w['a_re'], w['a_im'], w['cmat'], w['s5_d'], w['w_glu'], w['b_glu'], w['w_a_out'])
    state = pl.BlockSpec((nb, S5_COLS), lambda i: (0, 0))
    return pl.pallas_call(
        _s5_prompt_kernel,
        out_shape=(jax.ShapeDtypeStruct((seq, nb, D_MODEL), F32),
                   jax.ShapeDtypeStruct((nb, S5_COLS), F32),
                   jax.ShapeDtypeStruct((nb, S5_COLS), F32)),
        grid=(seq // tt,),
        in_specs=[pl.BlockSpec((tt, nb, S5_WIDTH), lambda i: (i, 0, 0))] + [_resident(a.shape) for a in weights],
        out_specs=(pl.BlockSpec((tt, nb, D_MODEL), lambda i: (i, 0, 0)), state, state),
        scratch_shapes=[pltpu.VMEM((tt * nb, 2 * S5_COLS), F32),
                        pltpu.VMEM((nb, S5_COLS), F32),
                        pltpu.VMEM((nb, S5_COLS), F32)],
        compiler_params=pltpu.CompilerParams(dimension_semantics=("arbitrary",),
                                             vmem_limit_bytes=V7X_VMEM_LIMIT_BYTES),
        name="s5_prompt",
    )(u3, *weights)


def _s5_sample_kernel(u_ref, h0re_ref, h0im_ref, bmat_ref, are_ref, aim_ref, cmat_ref, d_ref, wglu_ref,
                      bglu_ref, waout_ref, ya_ref, sre_ref, sim_ref, hs_ref):
    tt, nb, _ = u_ref.shape
    ar = are_ref[...]
    ai = aim_ref[...]
    hr = h0re_ref[...]
    hi = h0im_ref[...]
    for t in range(tt):
        x = _dot(u_ref[t].astype(BF16), bmat_ref[...])
        hr, hi = (ar * hr - ai * hi + x[:, :S5_COLS], ar * hi + ai * hr + x[:, S5_COLS:])
        hs_ref[t * nb:(t + 1) * nb, :S5_COLS] = hr
        hs_ref[t * nb:(t + 1) * nb, S5_COLS:] = hi
    sre_ref[...] = hr
    sim_ref[...] = hi
    u = u_ref[...].reshape(tt * nb, S5_WIDTH)
    ya = _s5_readout(hs_ref[...], u, cmat_ref, d_ref, wglu_ref, bglu_ref, waout_ref)
    ya_ref[...] = ya.reshape(tt, nb, D_MODEL)


def _s5_sample(u3, h0_re, h0_im, w):
    tt, nb, _ = u3.shape
    weights = (w['bmat'], w['a_re'], w['a_im'], w['cmat'], w['s5_d'], w['w_glu'], w['b_glu'], w['w_a_out'])
    args = (u3, h0_re, h0_im) + weights
    whole = lambda a: pl.BlockSpec(a.shape, lambda i, nd=a.ndim: (0,) * nd)
    outs = (jax.ShapeDtypeStruct((tt, nb, D_MODEL), F32),
            jax.ShapeDtypeStruct((nb, S5_COLS), F32),
            jax.ShapeDtypeStruct((nb, S5_COLS), F32))
    return pl.pallas_call(
        _s5_sample_kernel, out_shape=outs, grid=(1,),
        in_specs=[whole(a) for a in args], out_specs=tuple(whole(o) for o in outs),
        scratch_shapes=[pltpu.VMEM((tt * nb, 2 * S5_COLS), F32)],
        compiler_params=pltpu.CompilerParams(dimension_semantics=("arbitrary",),
                                             vmem_limit_bytes=V7X_VMEM_LIMIT_BYTES),
        name="s5_sample",
    )(*args)


def _lanes(stat, width):
    if width % LANES:
        return stat[:, :width]
    return jnp.tile(stat, (1, width // LANES))


def _softmax_block_update(s, v, rows, m_ref, l_ref, acc_ref):
    m_old = m_ref[rows, :]
    m_new = jnp.maximum(m_old, jnp.max(s, axis=-1, keepdims=True))
    alpha = jnp.exp2(m_old - m_new)
    p = jnp.exp2(s - _lanes(m_new, s.shape[1]))
    l_ref[rows, :] = alpha * l_ref[rows, :] + jnp.sum(p, axis=-1, keepdims=True)
    acc_ref[rows, :] = _lanes(alpha, KV_LORA) * acc_ref[rows, :] + _dot(p.astype(BF16), v)
    m_ref[rows, :] = m_new


def _attn_prompt_kernel(qlat_ref, qrope_ref, kcat_ref, o_ref, qs_ref, m_ref, l_ref, acc_ref):
    tq = ATTN_TILE
    qi = pl.program_id(1)
    lane_head = lax.broadcasted_iota(jnp.int32, (tq, ROPE_LANES), 1) // QK_ROPE
    qr = qrope_ref[...]
    for h in range(MLA_HEADS):
        rows = pl.ds(h * tq, tq)
        qs_ref[rows, :KV_LORA] = qlat_ref[:, KV_LORA * h:KV_LORA * (h + 1)]
        qs_ref[rows, KV_LORA:] = jnp.where(lane_head == h, qr, jnp.zeros_like(qr))
    m_ref[...] = jnp.full_like(m_ref, NEG_BIG)
    l_ref[...] = jnp.zeros_like(l_ref)
    acc_ref[...] = jnp.zeros_like(acc_ref)
    n = MLA_HEADS * tq // ATTN_ROW_SPLIT

    def block(k, causal):
        kc = kcat_ref[pl.ds(pl.multiple_of(k * tq, tq), tq), :]

        def scores(part):
            s = _dot_nt(qs_ref[pl.ds(part * n, n), :], kc)
            if causal:
                r = lax.broadcasted_iota(jnp.int32, s.shape, 0) & (tq - 1)
                c = lax.broadcasted_iota(jnp.int32, s.shape, 1)
                s = jnp.where(c <= r, s, NEG_BIG)
            return s

        s = scores(0)
        for part in range(ATTN_ROW_SPLIT):
            s_next = scores(part + 1) if part + 1 < ATTN_ROW_SPLIT else None
            _softmax_block_update(s, kc[:, :KV_LORA], pl.ds(part * n, n), m_ref, l_ref, acc_ref)
            s = s_next

    def past(k, carry):
        block(k, False)
        return carry

    lax.fori_loop(0, qi, past, 0)
    block(qi, True)
    o = acc_ref[...] / _lanes(l_ref[...], KV_LORA)
    for h in range(MLA_HEADS):
        o_ref[:, KV_LORA * h:KV_LORA * (h + 1)] = o[h * tq:(h + 1) * tq].astype(BF16)


def _attn_prompt(qlat, qrope, kcat, *, batch, seq):
    tq = ATTN_TILE
    nq = seq // tq
    rows = MLA_HEADS * tq
    qmap = lambda b, i: (b * nq + i, 0)
    return pl.pallas_call(
        _attn_prompt_kernel,
        out_shape=jax.ShapeDtypeStruct((batch * seq, LAT_LANES), BF16),
        grid=(batch, nq),
        in_specs=[pl.BlockSpec((tq, LAT_LANES), qmap),
                  pl.BlockSpec((tq, ROPE_LANES), qmap),
                  pl.BlockSpec((seq, KV_LORA + ROPE_LANES), lambda b, i: (b, 0))],
        out_specs=pl.BlockSpec((tq, LAT_LANES), qmap),
        scratch_shapes=[pltpu.VMEM((rows, KV_LORA + ROPE_LANES), BF16),
                        pltpu.VMEM((rows, LANES), F32),
                        pltpu.VMEM((rows, LANES), F32),
                        pltpu.VMEM((rows, KV_LORA), F32)],
        compiler_params=pltpu.CompilerParams(dimension_semantics=("parallel", "arbitrary"),
                                             vmem_limit_bytes=V7X_VMEM_LIMIT_BYTES),
        name="attn_prompt",
    )(qlat, qrope, kcat)


def _softmax_part(s, v):
    m = jnp.max(s, axis=-1, keepdims=True)
    p = jnp.exp2(s - m)
    return m, jnp.sum(p, axis=-1, keepdims=True), _dot(p.astype(BF16), v)


def _merge_softmax_parts(parts, m_ref, l_ref, acc_ref):
    m_old = m_ref[...]
    m_new = m_old
    for m, _, _ in parts:
        m_new = jnp.maximum(m_new, m)
    alpha = jnp.exp2(m_old - m_new)
    l = alpha * l_ref[...]
    acc = _lanes(alpha, KV_LORA) * acc_ref[...]
    for m, psum, pv in parts:
        wgt = jnp.exp2(m - m_new)
        l = l + wgt * psum
        acc = acc + _lanes(wgt, KV_LORA) * pv
    m_ref[...] = m_new
    l_ref[...] = l
    acc_ref[...] = acc


def _attn_sample_kernel(pt_ref, qlat_ref, qrope_ref, ckvn_ref, kropen_ref, cl_hbm, cr_hbm, o_ref,
                        kbuf, rbuf, sem, qs_ref, qr_ref, kn_ref, rn_ref, m_ref, l_ref, acc_ref,
                        *, n_pages, n_new):
    pg = SAMPLE_PAGES_PER_STEP
    chunks = n_pages // pg
    g = pl.program_id(0)
    n_steps = pl.num_programs(0)
    c = g % chunks
    last = n_steps - 1
    slot = lax.rem(g, SAMPLE_SLOTS)
    ahead = SAMPLE_SLOTS - 1

    def page_copies(step, dst_slot):
        first = (step // chunks) * n_pages + (step % chunks) * pg
        copies = []
        for p in range(pg):
            page = pt_ref[first + p]
            keys = pl.ds(p * PAGE_SIZE, PAGE_SIZE)
            copies.append(pltpu.make_async_copy(cl_hbm.at[page], kbuf.at[dst_slot, keys], sem.at[0, dst_slot]))
            copies.append(pltpu.make_async_copy(cr_hbm.at[page], rbuf.at[dst_slot, :, keys], sem.at[1, dst_slot]))
        return copies

    @pl.when(g == 0)
    def _():
        for step in range(ahead):
            for cp in page_copies(step, step):
                cp.start()

    @pl.when(c == 0)
    def _():
        ql = qlat_ref[0].astype(F32)
        qr = qrope_ref[0].astype(F32)
        for h in range(MLA_HEADS):
            rows = pl.ds(h * n_new, n_new)
            qs_ref[rows, :] = ql[:, KV_LORA * h:KV_LORA * (h + 1)]
            qr_ref[rows, :] = qr[:, QK_ROPE * h:QK_ROPE * (h + 1)]
        m_ref[...] = jnp.full_like(m_ref, NEG_BIG)
        l_ref[...] = jnp.zeros_like(l_ref)
        acc_ref[...] = jnp.zeros_like(acc_ref)

    for cp in page_copies(g, slot):
        cp.wait()

    qs = qs_ref[...].astype(BF16)
    qr = qr_ref[...].astype(BF16)
    sub = pg * PAGE_SIZE // SAMPLE_SUBCHUNKS
    def scores(j):
        keys = pl.ds(j * sub, sub)
        kc = kbuf[slot, keys, :].astype(BF16)
        return _dot_nt(qs, kc) + _dot(qr, rbuf[slot, :, keys].astype(BF16)), kc

    parts = []
    cur = scores(0)
    for j in range(SAMPLE_SUBCHUNKS):
        nxt = scores(j + 1) if j + 1 < SAMPLE_SUBCHUNKS else None
        parts.append(_softmax_part(*cur))
        cur = nxt
    _merge_softmax_parts(parts, m_ref, l_ref, acc_ref)

    for cp in page_copies(jnp.minimum(g + ahead, last), lax.rem(g + ahead, SAMPLE_SLOTS)):
        cp.start()

    @pl.when(g == last)
    def _():
        for extra in range(1, ahead + 1):
            for cp in page_copies(last, lax.rem(g + extra, SAMPLE_SLOTS)):
                cp.wait()

    @pl.when(c == chunks - 1)
    def _():
        kn_ref[...] = jnp.zeros_like(kn_ref)
        rn_ref[...] = jnp.zeros_like(rn_ref)
        kn_ref[pl.ds(0, n_new), :] = ckvn_ref[0]
        rn_ref[pl.ds(0, n_new), :] = kropen_ref[0]
        kn = kn_ref[...].astype(BF16)
        sn = _dot_nt(qs, kn) + _dot_nt(qr, rn_ref[...].astype(BF16))
        t = lax.broadcasted_iota(jnp.int32, sn.shape, 0) % n_new
        j = lax.broadcasted_iota(jnp.int32, sn.shape, 1)
        sn = jnp.where(j <= t, sn, NEG_BIG)
        _merge_softmax_parts([_softmax_part(sn, kn)], m_ref, l_ref, acc_ref)
        o_ref[0] = acc_ref[...] / _lanes(l_ref[...], KV_LORA)


def _attn_sample(page_table, qlat, qrope, ckv_new, krope_new, cache_latent, cache_k_rope):
    nb, n_pages = page_table.shape
    n_new = qlat.shape[0] // nb
    pg = SAMPLE_PAGES_PER_STEP
    chunks = n_pages // pg
    rows = MLA_HEADS * n_new
    new_pad = 8
    per_batch = lambda g, pt: (g // chunks, 0, 0)
    grid_spec = pltpu.PrefetchScalarGridSpec(
        num_scalar_prefetch=1,
        grid=(nb * chunks,),
        in_specs=[pl.BlockSpec((1, n_new, LAT_LANES), per_batch),
                  pl.BlockSpec((1, n_new, ROPE_LANES), per_batch),
                  pl.BlockSpec((1, n_new, KV_LORA), per_batch),
                  pl.BlockSpec((1, n_new, QK_ROPE), per_batch),
                  pl.BlockSpec(memory_space=pl.ANY),
                  pl.BlockSpec(memory_space=pl.ANY)],
        out_specs=pl.BlockSpec((1, rows, KV_LORA), per_batch),
        scratch_shapes=[pltpu.VMEM((SAMPLE_SLOTS, pg * PAGE_SIZE, KV_LORA), F32),
                        pltpu.VMEM((SAMPLE_SLOTS, QK_ROPE, pg * PAGE_SIZE), F32),
                        pltpu.SemaphoreType.DMA((2, SAMPLE_SLOTS)),
                        pltpu.VMEM((rows, KV_LORA), F32),
                        pltpu.VMEM((rows, QK_ROPE), F32),
                        pltpu.VMEM((new_pad, KV_LORA), F32),
                        pltpu.VMEM((new_pad, QK_ROPE), F32),
                        pltpu.VMEM((rows, LANES), F32),
                        pltpu.VMEM((rows, LANES), F32),
                        pltpu.VMEM((rows, KV_LORA), F32)])
    return pl.pallas_call(
        functools.partial(_attn_sample_kernel, n_pages=n_pages, n_new=n_new),
        out_shape=jax.ShapeDtypeStruct((nb, rows, KV_LORA), F32),
        grid_spec=grid_spec,
        compiler_params=pltpu.CompilerParams(dimension_semantics=("arbitrary",),
                                             vmem_limit_bytes=V7X_VMEM_LIMIT_BYTES),
        name="attn_sample",
    )(page_table.reshape(-1), qlat.reshape(nb, n_new, LAT_LANES), qrope.reshape(nb, n_new, ROPE_LANES),
      ckv_new.reshape(nb, n_new, KV_LORA), krope_new.reshape(nb, n_new, QK_ROPE), cache_latent, cache_k_rope)


def _post_mixer_kernel(h1_ref, ya_ref, ol_ref, gm_ref, wgt_ref, wuv_ref, wbo_ref, wout_ref,
                       g2_ref, wg_ref, wu_ref, wd_ref, gf_ref, y_ref):
    h1 = h1_ref[...]
    xm = _rms(h1, gm_ref[...]).astype(BF16)
    gates = _dot(xm, wgt_ref[...])
    ol = ol_ref[...]
    o = _dot(ol[:, :KV_LORA], wuv_ref[0])
    for h in range(1, MLA_HEADS):
        o = o + _dot(ol[:, KV_LORA * h:KV_LORA * (h + 1)], wuv_ref[h])
    yb = _dot(o.astype(BF16), wbo_ref[...])
    merged = jax.nn.sigmoid(gates[:, :D_MODEL]) * ya_ref[...] + jax.nn.sigmoid(gates[:, D_MODEL:]) * yb
    h2 = h1 + _dot(merged.astype(BF16), wout_ref[...])
    h3 = _ffn_half_step(h2, g2_ref, wg_ref, wu_ref, wd_ref)
    y_ref[...] = _rms(h3, gf_ref[...])


def _post_mixer(h1, ya, olat, w, *, seq_major_ya):
    m = h1.shape[0]
    tm = TOKEN_TILE
    row = lambda i: (i, 0)
    if seq_major_ya:
        blocks_per_seq = seq_major_ya // tm
        ya_spec = pl.BlockSpec((tm, D_MODEL), lambda i: (i % blocks_per_seq, i // blocks_per_seq))
    else:
        ya_spec = pl.BlockSpec((tm, D_MODEL), row)
    weights = (w['g_mix'], w['w_gates'], w['w_uv_pad'], w['w_b_out'], w['w_out'],
               w['g_ffn2'], w['wg2'], w['wu2'], w['wd2'], w['g_final'])
    return pl.pallas_call(
        _post_mixer_kernel,
        out_shape=jax.ShapeDtypeStruct((m, D_MODEL), F32),
        grid=(m // tm,),
        in_specs=[pl.BlockSpec((tm, D_MODEL), row), ya_spec, pl.BlockSpec((tm, LAT_LANES), row)]
        + [_resident(a.shape) for a in weights],
        out_specs=pl.BlockSpec((tm, D_MODEL), row),
        compiler_params=pltpu.CompilerParams(dimension_semantics=("parallel",),
                                             vmem_limit_bytes=V7X_VMEM_LIMIT_BYTES),
        name="post_mixer",
    )(h1, ya, olat, *weights)


def _rope_tables(pos):
    inv_freq = 1.0 / (ROPE_BASE ** (jnp.arange(0, QK_ROPE, 2, dtype=F32) / QK_ROPE))
    ang = pos.astype(F32)[:, None] * inv_freq[None, :]
    cos, sin = jnp.cos(ang), jnp.sin(ang)
    cos_tab = jnp.tile(jnp.concatenate([cos, cos], axis=-1), (1, MLA_HEADS))
    sin_tab = jnp.tile(jnp.concatenate([-sin, sin], axis=-1), (1, MLA_HEADS))
    return cos_tab, sin_tab


def _swap_halves(a):
    half = QK_ROPE // 2
    return jnp.concatenate([a[..., half:], a[..., :half]], axis=-1)


def _block_diag(blocks):
    g, r, c = blocks.shape
    eye = jnp.eye(g, dtype=blocks.dtype)
    return jnp.einsum('grc,gk->grkc', blocks, eye).reshape(g * r, g * c)


def _prepare_weights(p):
    w = {}
    vec = lambda a: a.reshape(1, -1).astype(F32)
    for name in ('g_ffn1', 'g_mix', 'g_q', 'g_kv', 'g_ffn2', 'g_final', 'b_glu'):
        w[name] = vec(p[name])
    for src, dst in (('w_ffn1_gate', 'wg1'), ('w_ffn1_up', 'wu1'), ('w_ffn1_down', 'wd1'),
                     ('w_ffn2_gate', 'wg2'), ('w_ffn2_up', 'wu2'), ('w_ffn2_down', 'wd2'),
                     ('w_glu', 'w_glu'), ('w_a_out', 'w_a_out'), ('w_b_out', 'w_b_out'), ('w_out', 'w_out')):
        w[dst] = p[src].astype(BF16)
    w_in = p['w_in']
    off_kr = S5_WIDTH + Q_LORA + KV_LORA
    k_r = w_in[:, off_kr:off_kr + QK_ROPE]
    w['w_in_ext'] = jnp.concatenate(
        [w_in[:, :off_kr], jnp.tile(k_r, (1, MLA_HEADS)), jnp.tile(_swap_halves(k_r), (1, MLA_HEADS))],
        axis=1).astype(BF16)
    w['w_gates'] = w_in[:, off_kr + QK_ROPE:].astype(BF16)
    w_uq = p['w_uq']
    uq_rope = w_uq[:, :, QK_NOPE:]
    w['w_uq_ext'] = jnp.concatenate(
        [w_uq[:, :, :QK_NOPE].reshape(Q_LORA, -1), uq_rope.reshape(Q_LORA, -1),
         _swap_halves(uq_rope).reshape(Q_LORA, -1)], axis=1).astype(BF16)
    uk = jnp.transpose(p['w_uk'], (1, 2, 0))
    uk_pad = jnp.zeros((MLA_HEADS, 2, QK_NOPE, KV_LORA), F32)
    uk_pad = uk_pad.at[jnp.arange(MLA_HEADS), jnp.arange(MLA_HEADS) % 2].set(uk)
    w['w_uk_pad'] = uk_pad.reshape(MLA_HEADS, 2 * QK_NOPE, KV_LORA).astype(BF16)
    uv = jnp.transpose(p['w_uv'], (1, 0, 2))
    uv_pad = jnp.zeros((MLA_HEADS, KV_LORA, MLA_HEADS, V_DIM), F32)
    uv_pad = uv_pad.at[jnp.arange(MLA_HEADS), :, jnp.arange(MLA_HEADS)].set(uv)
    w['w_uv_pad'] = uv_pad.reshape(MLA_HEADS, KV_LORA, MLA_HEADS * V_DIM).astype(BF16)
    lam_re, lam_im = p['s5_a_re'].astype(F32), p['s5_a_im'].astype(F32)
    dt = jnp.exp(p['s5_log_dt'].astype(F32))[:, None]
    mag = jnp.exp(lam_re * dt)
    ab_re, ab_im = mag * jnp.cos(lam_im * dt), mag * jnp.sin(lam_im * dt)
    den = lam_re * lam_re + lam_im * lam_im
    num_re, num_im = ab_re - 1.0, ab_im
    k_re = (num_re * lam_re + num_im * lam_im) / den
    k_im = (num_im * lam_re - num_re * lam_im) / den
    w['a_re'] = ab_re.reshape(1, S5_COLS)
    w['a_im'] = ab_im.reshape(1, S5_COLS)
    b_re, b_im = p['s5_b_re'].astype(F32), p['s5_b_im'].astype(F32)
    kb_re = k_re[..., None] * b_re - k_im[..., None] * b_im
    kb_im = k_re[..., None] * b_im + k_im[..., None] * b_re
    to_in = lambda a: _block_diag(jnp.transpose(a, (0, 2, 1)))
    w['bmat'] = jnp.concatenate([to_in(kb_re), to_in(kb_im)], axis=1).astype(BF16)
    to_out = lambda a: _block_diag(jnp.transpose(a, (0, 2, 1)))
    w['cmat'] = jnp.concatenate([to_out(p['s5_c_re'].astype(F32)), -to_out(p['s5_c_im'].astype(F32))],
                                axis=0).astype(BF16)
    w['s5_d'] = vec(p['s5_d'])
    return w


def kernel(x_prompt, x_sample, cache_latent, cache_k_rope, state_ssm_re, state_ssm_im, page_table, g_ffn1, w_ffn1_gate, w_ffn1_up, w_ffn1_down, g_mix, w_in, s5_a_re, s5_a_im, s5_log_dt, s5_b_re, s5_b_im, s5_c_re, s5_c_im, s5_d, w_glu, b_glu, w_a_out, g_q, w_uq, g_kv, w_uk, w_uv, w_b_out, w_out, g_ffn2, w_ffn2_gate, w_ffn2_up, w_ffn2_down, g_final):
    layer = 0
    params = dict(
        g_ffn1=g_ffn1, w_ffn1_gate=w_ffn1_gate, w_ffn1_up=w_ffn1_up, w_ffn1_down=w_ffn1_down,
        g_mix=g_mix, w_in=w_in, s5_a_re=s5_a_re, s5_a_im=s5_a_im, s5_log_dt=s5_log_dt,
        s5_b_re=s5_b_re, s5_b_im=s5_b_im, s5_c_re=s5_c_re, s5_c_im=s5_c_im, s5_d=s5_d,
        w_glu=w_glu, b_glu=b_glu, w_a_out=w_a_out, g_q=g_q, w_uq=w_uq, g_kv=g_kv, w_uk=w_uk,
        w_uv=w_uv, w_b_out=w_b_out, w_out=w_out, g_ffn2=g_ffn2, w_ffn2_gate=w_ffn2_gate,
        w_ffn2_up=w_ffn2_up, w_ffn2_down=w_ffn2_down)
    p = {k: v[layer] for k, v in params.items()}
    p['g_final'] = g_final
    w = _prepare_weights(p)

    nb, seq, _ = x_prompt.shape
    cos_p, sin_p = _rope_tables(jnp.arange(seq))
    h1, u, ckv, krope, kcat, qlat, qrope = _pre_mixer(
        x_prompt.reshape(nb * seq, D_MODEL), cos_p, sin_p, w, seq_major_u=seq)
    ya, sre_p, sim_p = _s5_prompt(u.reshape(seq, nb, S5_WIDTH), w)
    olat = _attn_prompt(qlat, qrope, kcat, batch=nb, seq=seq)
    y_prompt = _post_mixer(h1, ya.reshape(seq, nb * D_MODEL), olat, w, seq_major_ya=seq).reshape(nb, seq, D_MODEL)
    ckv_prompt = ckv.reshape(1, nb, seq, KV_LORA)
    krope_prompt = krope.reshape(1, nb, seq, QK_ROPE)
    group = lambda a: a.reshape(1, a.shape[0], S5_GROUPS, S5_STATE)

    sb, n_new, _ = x_sample.shape
    n_pages = page_table.shape[1]
    past_len = n_pages * PAGE_SIZE
    cos_s, sin_s = _rope_tables(past_len + jnp.arange(n_new))
    cos_s, sin_s = jnp.tile(cos_s, (sb, 1)), jnp.tile(sin_s, (sb, 1))
    h1s, us, ckvs, kropes, _, qlats, qropes = _pre_mixer(
        x_sample.reshape(sb * n_new, D_MODEL), cos_s, sin_s, w, seq_major_u=0)
    us_t = jnp.transpose(us.reshape(sb, n_new, S5_WIDTH), (1, 0, 2))
    yas_t, sre_s, sim_s = _s5_sample(us_t, state_ssm_re[layer].reshape(sb, S5_COLS),
                                     state_ssm_im[layer].reshape(sb, S5_COLS), w)
    yas = jnp.transpose(yas_t, (1, 0, 2)).reshape(sb * n_new, D_MODEL)
    rope_pages = jnp.swapaxes(cache_k_rope[layer], 1, 2)
    ol = _attn_sample(page_table, qlats, qropes, ckvs, kropes, cache_latent[layer], rope_pages)
    olats = jnp.transpose(ol.reshape(sb, MLA_HEADS, n_new, KV_LORA), (0, 2, 1, 3))
    olats = olats.reshape(sb * n_new, LAT_LANES).astype(BF16)
    y_sample = _post_mixer(h1s, yas, olats, w, seq_major_ya=0).reshape(sb, n_new, D_MODEL)

    return (y_prompt, y_sample, ckv_prompt, krope_prompt,
            ckvs.reshape(1, sb, n_new, KV_LORA), kropes.reshape(1, sb, n_new, QK_ROPE),
            group(sre_p), group(sim_p), group(sre_s), group(sim_s))
```

```python
import functools

import jax
import jax.numpy as jnp
from jax import lax
from jax.experimental import pallas as pl
from jax.experimental.pallas import tpu as pltpu

F32 = jnp.float32
BF16 = jnp.bfloat16

D_MODEL = 1024
D_FF = 2816
S5_WIDTH = 512
S5_GROUP = 16
S5_GROUPS = 32
S5_STATE = 64
S5_COLS = S5_GROUPS * S5_STATE
MLA_HEADS = 8
QK_NOPE = 64
QK_ROPE = 32
V_DIM = 64
Q_LORA = 384
KV_LORA = 256
ROPE_BASE = 10000.0
NORM_EPS = 1e-6
PAGE_SIZE = 128
ATTN_SCALE = (QK_NOPE + QK_ROPE) ** -0.5
Q_PRESCALE = ATTN_SCALE * 1.4426950408889634
LANES = 128
MXU_TILE = 256
S5_STATE_TILES = S5_COLS // MXU_TILE
S5_GROUPS_PER_STATE_TILE = MXU_TILE // S5_STATE
S5_TILES_PER_SLAB = LANES // (S5_GROUPS_PER_STATE_TILE * S5_GROUP)
S5_OUT_TILES = S5_WIDTH // MXU_TILE
S5_GROUPS_PER_OUT_TILE = MXU_TILE // S5_GROUP
ROPE_LANES = MLA_HEADS * QK_ROPE
LAT_LANES = MLA_HEADS * KV_LORA
IN_EXT = S5_WIDTH + Q_LORA + KV_LORA + 2 * ROPE_LANES
OFF_CQ = S5_WIDTH
OFF_CKV = OFF_CQ + Q_LORA
OFF_KR = OFF_CKV + KV_LORA
OFF_KRS = OFF_KR + ROPE_LANES
NEG_BIG = -1e30

V7X_VMEM_LIMIT_BYTES = 56 * 1024 * 1024
TOKEN_TILE = 256
S5_TIME_TILE = 32
S5_COL_TILE = 512
ATTN_TILE = 512
ATTN_ROW_SPLIT = 4
ATTN_LOOKAHEAD = 1
SAMPLE_LOOKAHEAD = 3
SAMPLE_PAGES_PER_STEP = 64
SAMPLE_SUBCHUNKS = 8
SAMPLE_SLOTS = 3


def _rms(x, g):
    return x * lax.rsqrt(jnp.mean(x * x, axis=-1, keepdims=True) + NORM_EPS) * g


def _dot(a, b):
    return jnp.dot(a, b, preferred_element_type=F32)


def _dot_nt(a, b):
    return lax.dot_general(a, b, (((1,), (1,)), ((), ())), preferred_element_type=F32)


def _ffn_half_step(x, g_ref, wg_ref, wu_ref, wd_ref):
    xn = _rms(x, g_ref[...]).astype(BF16)
    gate = _dot(xn, wg_ref[...])
    up = _dot(xn, wu_ref[...])
    hid = (jax.nn.silu(gate) * up).astype(BF16)
    return x + 0.5 * _dot(hid, wd_ref[...])


def _resident(shape):
    nd = len(shape)
    return pl.BlockSpec(shape, lambda *_: (0,) * nd, pipeline_mode=pl.Buffered(1))


def _pre_mixer_kernel(x_ref, cos_ref, sin_ref, g1_ref, wg_ref, wu_ref, wd_ref, gm_ref, win_ref,
                      gq_ref, gkv_ref, wuq_ref, wuk_ref,
                      h1_ref, u_ref, ckv_ref, krope_ref, kcat_ref, qlat_ref, qrope_ref):
    h1 = _ffn_half_step(x_ref[...], g1_ref, wg_ref, wu_ref, wd_ref)
    h1_ref[...] = h1
    xm = _rms(h1, gm_ref[...]).astype(BF16)
    proj = _dot(xm, win_ref[...])
    u_ref[...] = proj[:, :S5_WIDTH]
    cqn = _rms(proj[:, OFF_CQ:OFF_CKV], gq_ref[...]).astype(BF16)
    ckv = _rms(proj[:, OFF_CKV:OFF_KR], gkv_ref[...])
    ckv_ref[...] = ckv
    cos = cos_ref[...]
    sin = sin_ref[...]
    kr8 = proj[:, OFF_KR:OFF_KRS] * cos + proj[:, OFF_KRS:IN_EXT] * sin
    krope_ref[...] = kr8[:, :QK_ROPE]
    kcat_ref[:, :KV_LORA] = ckv.astype(BF16)
    kcat_ref[:, KV_LORA:] = kr8.astype(BF16)
    q = _dot(cqn, wuq_ref[...])
    n_nope = MLA_HEADS * QK_NOPE
    qrope_ref[...] = ((q[:, n_nope:n_nope + ROPE_LANES] * cos
                       + q[:, n_nope + ROPE_LANES:] * sin) * Q_PRESCALE).astype(BF16)
    qn = q[:, :n_nope].astype(BF16)
    for h in range(MLA_HEADS):
        pair = h // 2
        qlat_ref[:, KV_LORA * h:KV_LORA * (h + 1)] = (_dot(
            qn[:, LANES * pair:LANES * (pair + 1)], wuk_ref[h]) * Q_PRESCALE).astype(BF16)


def _pre_mixer(x, cos_tab, sin_tab, w, *, seq_major_u):
    m = x.shape[0]
    tm = TOKEN_TILE
    nsteps = m // tm
    tab_blocks = cos_tab.shape[0] // tm
    row = lambda i: (i, 0)
    tab = lambda i: (i % tab_blocks, 0)
    if seq_major_u:
        blocks_per_seq = seq_major_u // tm
        u_shape = (seq_major_u, (m // seq_major_u) * S5_WIDTH)
        u_spec = pl.BlockSpec((tm, S5_WIDTH), lambda i: (i % blocks_per_seq, i // blocks_per_seq))
    else:
        u_shape = (m, S5_WIDTH)
        u_spec = pl.BlockSpec((tm, S5_WIDTH), row)
    out_shape = (
        jax.ShapeDtypeStruct((m, D_MODEL), F32),
        jax.ShapeDtypeStruct(u_shape, F32),
        jax.ShapeDtypeStruct((m, KV_LORA), F32),
        jax.ShapeDtypeStruct((m, QK_ROPE), F32),
        jax.ShapeDtypeStruct((m, KV_LORA + ROPE_LANES), BF16),
        jax.ShapeDtypeStruct((m, LAT_LANES), BF16),
        jax.ShapeDtypeStruct((m, ROPE_LANES), BF16),
    )
    out_specs = (
        pl.BlockSpec((tm, D_MODEL), row),
        u_spec,
        pl.BlockSpec((tm, KV_LORA), row),
        pl.BlockSpec((tm, QK_ROPE), row),
        pl.BlockSpec((tm, KV_LORA + ROPE_LANES), row),
        pl.BlockSpec((tm, LAT_LANES), row),
        pl.BlockSpec((tm, ROPE_LANES), row),
    )
    weights = (w['g_ffn1'], w['wg1'], w['wu1'], w['wd1'], w['g_mix'], w['w_in_ext'],
               w['g_q'], w['g_kv'], w['w_uq_ext'], w['w_uk_pad'])
    in_specs = [pl.BlockSpec((tm, D_MODEL), row),
                pl.BlockSpec((tm, ROPE_LANES), tab),
                pl.BlockSpec((tm, ROPE_LANES), tab)] + [_resident(a.shape) for a in weights]
    return pl.pallas_call(
        _pre_mixer_kernel, out_shape=out_shape, grid=(nsteps,), in_specs=in_specs, out_specs=out_specs,
        compiler_params=pltpu.CompilerParams(dimension_semantics=("parallel",),
                                             vmem_limit_bytes=V7X_VMEM_LIMIT_BYTES),
        name="pre_mixer",
    )(x, cos_tab, sin_tab, *weights)


def _s5_readout(h_ref, u, cpack_ref, d_ref, wglu_ref, bglu_ref, waout_ref):
    k = S5_COLS // S5_OUT_TILES
    tiles = []
    for n in range(S5_OUT_TILES):
        h_re = h_ref[:, k * n:k * (n + 1)].astype(BF16)
        h_im = h_ref[:, S5_COLS + k * n:S5_COLS + k * (n + 1)].astype(BF16)
        tiles.append(_dot(h_re, cpack_ref[n, :k]) + _dot(h_im, cpack_ref[n, k:]))
    y = jnp.concatenate(tiles, axis=1) + d_ref[...] * u
    z = jax.nn.gelu(y)
    gate = _dot(z.astype(BF16), wglu_ref[...]) + bglu_ref[...]
    return _dot((z * jax.nn.sigmoid(gate)).astype(BF16), waout_ref[...])


def _s5_drive(u, bpack_ref, xs_ref):
    u_bf = u.astype(BF16)
    for j in range(2 * S5_STATE_TILES):
        slab = (j % S5_STATE_TILES) // S5_TILES_PER_SLAB
        xs_ref[:, MXU_TILE * j:MXU_TILE * (j + 1)] = _dot(u_bf[:, LANES * slab:LANES * (slab + 1)], bpack_ref[j])


def _s5_prompt_kernel(u_ref, bpack_ref, are_ref, aim_ref, cpack_ref, d_ref, wglu_ref, bglu_ref, waout_ref,
                      ya_ref, sre_ref, sim_ref, xs_ref, hre_ref, him_ref):
    tt, nb, _ = u_ref.shape
    rows = tt * nb

    @pl.when(pl.program_id(0) == 0)
    def _():
        hre_ref[...] = jnp.zeros_like(hre_ref)
        him_ref[...] = jnp.zeros_like(him_ref)

    u = u_ref[...].reshape(rows, S5_WIDTH)
    _s5_drive(u, bpack_ref, xs_ref)
    for cb in range(S5_COLS // S5_COL_TILE):
        lo = cb * S5_COL_TILE
        re_cols = pl.ds(lo, S5_COL_TILE)
        im_cols = pl.ds(S5_COLS + lo, S5_COL_TILE)
        ar = jnp.broadcast_to(are_ref[:, re_cols], (nb, S5_COL_TILE))
        ai = jnp.broadcast_to(aim_ref[:, re_cols], (nb, S5_COL_TILE))

        def body(t, carry):
            hr, hi = carry
            r = pl.ds(pl.multiple_of(t * nb, nb), nb)
            nr = ar * hr - ai * hi + xs_ref[r, re_cols]
            ni = ar * hi + ai * hr + xs_ref[r, im_cols]
            xs_ref[r, re_cols] = nr
            xs_ref[r, im_cols] = ni
            return nr, ni

        hr, hi = lax.fori_loop(0, tt, body, (hre_ref[:, re_cols], him_ref[:, re_cols]), unroll=4)
        hre_ref[:, re_cols] = hr
        him_ref[:, re_cols] = hi
    ya = _s5_readout(xs_ref, u, cpack_ref, d_ref, wglu_ref, bglu_ref, waout_ref)
    ya_ref[...] = ya.reshape(tt, nb, D_MODEL)
    sre_ref[...] = hre_ref[...]
    sim_ref[...] = him_ref[...]


def _s5_prompt(u3, w):
    seq, nb, _ = u3.shape
    tt = S5_TIME_TILE
    weights = (w['bpack'], w['a_re'], w['a_im'], w['cpack'], w['s5_d'], w['w_glu'], w['b_glu'], w['w_a_out'])
    state = pl.BlockSpec((nb, S5_COLS), lambda i: (0, 0))
    return pl.pallas_call(
        _s5_prompt_kernel,
        out_shape=(jax.ShapeDtypeStruct((seq, nb, D_MODEL), F32),
                   jax.ShapeDtypeStruct((nb, S5_COLS), F32),
                   jax.ShapeDtypeStruct((nb, S5_COLS), F32)),
        grid=(seq // tt,),
        in_specs=[pl.BlockSpec((tt, nb, S5_WIDTH), lambda i: (i, 0, 0))] + [_resident(a.shape) for a in weights],
        out_specs=(pl.BlockSpec((tt, nb, D_MODEL), lambda i: (i, 0, 0)), state, state),
        scratch_shapes=[pltpu.VMEM((tt * nb, 2 * S5_COLS), F32),
                        pltpu.VMEM((nb, S5_COLS), F32),
                        pltpu.VMEM((nb, S5_COLS), F32)],
        compiler_params=pltpu.CompilerParams(dimension_semantics=("arbitrary",),
                                             vmem_limit_bytes=V7X_VMEM_LIMIT_BYTES),
        name="s5_prompt",
    )(u3, *weights)


def _s5_sample_kernel(u_ref, h0re_ref, h0im_ref, bpack_ref, are_ref, aim_ref, cpack_ref, d_ref, wglu_ref,
                      bglu_ref, waout_ref, ya_ref, sre_ref, sim_ref, hs_ref):
    tt, nb, _ = u_ref.shape
    u = u_ref[...].reshape(tt * nb, S5_WIDTH)
    _s5_drive(u, bpack_ref, hs_ref)
    ar = are_ref[...]
    ai = aim_ref[...]
    hr = h0re_ref[...]
    hi = h0im_ref[...]
    for t in range(tt):
        r = pl.ds(t * nb, nb)
        hr, hi = (ar * hr - ai * hi + hs_ref[r, :S5_COLS], ar * hi + ai * hr + hs_ref[r, S5_COLS:])
        hs_ref[r, :S5_COLS] = hr
        hs_ref[r, S5_COLS:] = hi
    sre_ref[...] = hr
    sim_ref[...] = hi
    ya = _s5_readout(hs_ref, u, cpack_ref, d_ref, wglu_ref, bglu_ref, waout_ref)
    ya_ref[...] = ya.reshape(tt, nb, D_MODEL)


def _s5_sample(u3, h0_re, h0_im, w):
    tt, nb, _ = u3.shape
    weights = (w['bpack'], w['a_re'], w['a_im'], w['cpack'], w['s5_d'], w['w_glu'], w['b_glu'], w['w_a_out'])
    args = (u3, h0_re, h0_im) + weights
    whole = lambda a: pl.BlockSpec(a.shape, lambda i, nd=a.ndim: (0,) * nd)
    outs = (jax.ShapeDtypeStruct((tt, nb, D_MODEL), F32),
            jax.ShapeDtypeStruct((nb, S5_COLS), F32),
            jax.ShapeDtypeStruct((nb, S5_COLS), F32))
    return pl.pallas_call(
        _s5_sample_kernel, out_shape=outs, grid=(1,),
        in_specs=[whole(a) for a in args], out_specs=tuple(whole(o) for o in outs),
        scratch_shapes=[pltpu.VMEM((tt * nb, 2 * S5_COLS), F32)],
        compiler_params=pltpu.CompilerParams(dimension_semantics=("arbitrary",),
                                             vmem_limit_bytes=V7X_VMEM_LIMIT_BYTES),
        name="s5_sample",
    )(*args)


def _lanes(stat, width):
    if width % LANES:
        return stat[:, :width]
    return jnp.tile(stat, (1, width // LANES))


def _softmax_block_update(s, v, rows, m_ref, l_ref, acc_ref):
    m_old = m_ref[rows, :]
    m_new = jnp.maximum(m_old, jnp.max(s, axis=-1, keepdims=True))
    alpha = jnp.exp2(m_old - m_new)
    p = jnp.exp2(s - _lanes(m_new, s.shape[1]))
    l_ref[rows, :] = alpha * l_ref[rows, :] + jnp.sum(p, axis=-1, keepdims=True)
    acc_ref[rows, :] = _lanes(alpha, KV_LORA) * acc_ref[rows, :] + _dot(p.astype(BF16), v)
    m_ref[rows, :] = m_new


def _attn_prompt_kernel(qlat_ref, qrope_ref, kcat_ref, o_ref, qs_ref, m_ref, l_ref, acc_ref):
    tq = ATTN_TILE
    qi = pl.program_id(1)
    lane_head = lax.broadcasted_iota(jnp.int32, (tq, ROPE_LANES), 1) // QK_ROPE
    qr = qrope_ref[...]
    for h in range(MLA_HEADS):
        rows = pl.ds(h * tq, tq)
        qs_ref[rows, :KV_LORA] = qlat_ref[:, KV_LORA * h:KV_LORA * (h + 1)]
        qs_ref[rows, KV_LORA:] = jnp.where(lane_head == h, qr, jnp.zeros_like(qr))
    m_ref[...] = jnp.full_like(m_ref, NEG_BIG)
    l_ref[...] = jnp.zeros_like(l_ref)
    acc_ref[...] = jnp.zeros_like(acc_ref)
    n = MLA_HEADS * tq // ATTN_ROW_SPLIT

    def block(k, causal):
        kc = kcat_ref[pl.ds(pl.multiple_of(k * tq, tq), tq), :]

        def scores(part):
            s = _dot_nt(qs_ref[pl.ds(part * n, n), :], kc)
            if causal:
                r = lax.broadcasted_iota(jnp.int32, s.shape, 0) & (tq - 1)
                c = lax.broadcasted_iota(jnp.int32, s.shape, 1)
                s = jnp.where(c <= r, s, NEG_BIG)
            return s

        ready = [scores(part) for part in range(min(ATTN_LOOKAHEAD, ATTN_ROW_SPLIT))]
        for part in range(ATTN_ROW_SPLIT):
            if part + ATTN_LOOKAHEAD < ATTN_ROW_SPLIT:
                ready.append(scores(part + ATTN_LOOKAHEAD))
            _softmax_block_update(ready.pop(0), kc[:, :KV_LORA], pl.ds(part * n, n), m_ref, l_ref, acc_ref)

    def past(k, carry):
        block(k, False)
        return carry

    lax.fori_loop(0, qi, past, 0)
    block(qi, True)
    o = acc_ref[...] / _lanes(l_ref[...], KV_LORA)
    for h in range(MLA_HEADS):
        o_ref[:, KV_LORA * h:KV_LORA * (h + 1)] = o[h * tq:(h + 1) * tq].astype(BF16)


def _attn_prompt(qlat, qrope, kcat, *, batch, seq):
    tq = ATTN_TILE
    nq = seq // tq
    rows = MLA_HEADS * tq
    qmap = lambda b, i: (b * nq + i, 0)
    return pl.pallas_call(
        _attn_prompt_kernel,
        out_shape=jax.ShapeDtypeStruct((batch * seq, LAT_LANES), BF16),
        grid=(batch, nq),
        in_specs=[pl.BlockSpec((tq, LAT_LANES), qmap),
                  pl.BlockSpec((tq, ROPE_LANES), qmap),
                  pl.BlockSpec((seq, KV_LORA + ROPE_LANES), lambda b, i: (b, 0))],
        out_specs=pl.BlockSpec((tq, LAT_LANES), qmap),
        scratch_shapes=[pltpu.VMEM((rows, KV_LORA + ROPE_LANES), BF16),
                        pltpu.VMEM((rows, LANES), F32),
                        pltpu.VMEM((rows, LANES), F32),
                        pltpu.VMEM((rows, KV_LORA), F32)],
        compiler_params=pltpu.CompilerParams(dimension_semantics=("parallel", "arbitrary"),
                                             vmem_limit_bytes=V7X_VMEM_LIMIT_BYTES),
        name="attn_prompt",
    )(qlat, qrope, kcat)


def _softmax_part(s, v):
    m = jnp.max(s, axis=-1, keepdims=True)
    p = jnp.exp2(s - m)
    return m, jnp.sum(p, axis=-1, keepdims=True), _dot(p.astype(BF16), v)


def _merge_softmax_parts(parts, m_ref, l_ref, acc_ref):
    m_old = m_ref[...]
    m_new = m_old
    for m, _, _ in parts:
        m_new = jnp.maximum(m_new, m)
    alpha = jnp.exp2(m_old - m_new)
    l = alpha * l_ref[...]
    acc = _lanes(alpha, KV_LORA) * acc_ref[...]
    for m, psum, pv in parts:
        wgt = jnp.exp2(m - m_new)
        l = l + wgt * psum
        acc = acc + _lanes(wgt, KV_LORA) * pv
    m_ref[...] = m_new
    l_ref[...] = l
    acc_ref[...] = acc


def _attn_sample_kernel(pt_ref, qlat_ref, qrope_ref, ckvn_ref, kropen_ref, cl_hbm, cr_hbm, o_ref,
                        kbuf, rbuf, sem, qs_ref, qr_ref, kn_ref, rn_ref, m_ref, l_ref, acc_ref,
                        *, n_pages, n_new):
    pg = SAMPLE_PAGES_PER_STEP
    chunks = n_pages // pg
    g = pl.program_id(0)
    n_steps = pl.num_programs(0)
    c = g % chunks
    last = n_steps - 1
    slot = lax.rem(g, SAMPLE_SLOTS)
    ahead = SAMPLE_SLOTS - 1

    def page_copies(step, dst_slot):
        first = (step // chunks) * n_pages + (step % chunks) * pg
        copies = []
        for p in range(pg):
            page = pt_ref[first + p]
            keys = pl.ds(p * PAGE_SIZE, PAGE_SIZE)
            copies.append(pltpu.make_async_copy(cl_hbm.at[page], kbuf.at[dst_slot, keys], sem.at[0, dst_slot]))
            copies.append(pltpu.make_async_copy(cr_hbm.at[page], rbuf.at[dst_slot, :, keys], sem.at[1, dst_slot]))
        return copies

    @pl.when(g == 0)
    def _():
        for step in range(ahead):
            for cp in page_copies(step, step):
                cp.start()

    @pl.when(c == 0)
    def _():
        ql = qlat_ref[0].astype(F32)
        qr = qrope_ref[0].astype(F32)
        for h in range(MLA_HEADS):
            rows = pl.ds(h * n_new, n_new)
            qs_ref[rows, :] = ql[:, KV_LORA * h:KV_LORA * (h + 1)]
            qr_ref[rows, :] = qr[:, QK_ROPE * h:QK_ROPE * (h + 1)]
        m_ref[...] = jnp.full_like(m_ref, NEG_BIG)
        l_ref[...] = jnp.zeros_like(l_ref)
        acc_ref[...] = jnp.zeros_like(acc_ref)

    for cp in page_copies(g, slot):
        cp.wait()

    qs = qs_ref[...].astype(BF16)
    qr = qr_ref[...].astype(BF16)
    sub = pg * PAGE_SIZE // SAMPLE_SUBCHUNKS
    def scores(j):
        keys = pl.ds(j * sub, sub)
        kc = kbuf[slot, keys, :].astype(BF16)
        return _dot_nt(qs, kc) + _dot(qr, rbuf[slot, :, keys].astype(BF16)), kc

    parts = []
    ready = [scores(j) for j in range(min(SAMPLE_LOOKAHEAD, SAMPLE_SUBCHUNKS))]
    for j in range(SAMPLE_SUBCHUNKS):
        if j + SAMPLE_LOOKAHEAD < SAMPLE_SUBCHUNKS:
            ready.append(scores(j + SAMPLE_LOOKAHEAD))
        parts.append(_softmax_part(*ready.pop(0)))
    _merge_softmax_parts(parts, m_ref, l_ref, acc_ref)

    for cp in page_copies(jnp.minimum(g + ahead, last), lax.rem(g + ahead, SAMPLE_SLOTS)):
        cp.start()

    @pl.when(g == last)
    def _():
        for extra in range(1, ahead + 1):
            for cp in page_copies(last, lax.rem(g + extra, SAMPLE_SLOTS)):
                cp.wait()

    @pl.when(c == chunks - 1)
    def _():
        kn_ref[...] = jnp.zeros_like(kn_ref)
        rn_ref[...] = jnp.zeros_like(rn_ref)
        kn_ref[pl.ds(0, n_new), :] = ckvn_ref[0]
        rn_ref[pl.ds(0, n_new), :] = kropen_ref[0]
        kn = kn_ref[...].astype(BF16)
        sn = _dot_nt(qs, kn) + _dot_nt(qr, rn_ref[...].astype(BF16))
        t = lax.broadcasted_iota(jnp.int32, sn.shape, 0) % n_new
        j = lax.broadcasted_iota(jnp.int32, sn.shape, 1)
        sn = jnp.where(j <= t, sn, NEG_BIG)
        _merge_softmax_parts([_softmax_part(sn, kn)], m_ref, l_ref, acc_ref)
        o_ref[0] = acc_ref[...] / _lanes(l_ref[...], KV_LORA)


def _attn_sample(page_table, qlat, qrope, ckv_new, krope_new, cache_latent, cache_k_rope):
    nb, n_pages = page_table.shape
    n_new = qlat.shape[0] // nb
    pg = SAMPLE_PAGES_PER_STEP
    chunks = n_pages // pg
    rows = MLA_HEADS * n_new
    new_pad = 8
    per_batch = lambda g, pt: (g // chunks, 0, 0)
    grid_spec = pltpu.PrefetchScalarGridSpec(
        num_scalar_prefetch=1,
        grid=(nb * chunks,),
        in_specs=[pl.BlockSpec((1, n_new, LAT_LANES), per_batch),
                  pl.BlockSpec((1, n_new, ROPE_LANES), per_batch),
                  pl.BlockSpec((1, n_new, KV_LORA), per_batch),
                  pl.BlockSpec((1, n_new, QK_ROPE), per_batch),
                  pl.BlockSpec(memory_space=pl.ANY),
                  pl.BlockSpec(memory_space=pl.ANY)],
        out_specs=pl.BlockSpec((1, rows, KV_LORA), per_batch),
        scratch_shapes=[pltpu.VMEM((SAMPLE_SLOTS, pg * PAGE_SIZE, KV_LORA), F32),
                        pltpu.VMEM((SAMPLE_SLOTS, QK_ROPE, pg * PAGE_SIZE), F32),
                        pltpu.SemaphoreType.DMA((2, SAMPLE_SLOTS)),
                        pltpu.VMEM((rows, KV_LORA), F32),
                        pltpu.VMEM((rows, QK_ROPE), F32),
                        pltpu.VMEM((new_pad, KV_LORA), F32),
                        pltpu.VMEM((new_pad, QK_ROPE), F32),
                        pltpu.VMEM((rows, LANES), F32),
                        pltpu.VMEM((rows, LANES), F32),
                        pltpu.VMEM((rows, KV_LORA), F32)])
    return pl.pallas_call(
        functools.partial(_attn_sample_kernel, n_pages=n_pages, n_new=n_new),
        out_shape=jax.ShapeDtypeStruct((nb, rows, KV_LORA), F32),
        grid_spec=grid_spec,
        compiler_params=pltpu.CompilerParams(dimension_semantics=("arbitrary",),
                                             vmem_limit_bytes=V7X_VMEM_LIMIT_BYTES),
        name="attn_sample",
    )(page_table.reshape(-1), qlat.reshape(nb, n_new, LAT_LANES), qrope.reshape(nb, n_new, ROPE_LANES),
      ckv_new.reshape(nb, n_new, KV_LORA), krope_new.reshape(nb, n_new, QK_ROPE), cache_latent, cache_k_rope)


def _post_mixer_kernel(h1_ref, ya_ref, ol_ref, gm_ref, wgt_ref, wuv_ref, wbo_ref, wout_ref,
                       g2_ref, wg_ref, wu_ref, wd_ref, gf_ref, y_ref):
    h1 = h1_ref[...]
    xm = _rms(h1, gm_ref[...]).astype(BF16)
    gates = _dot(xm, wgt_ref[...])
    ol = ol_ref[...]
    o = _dot(ol[:, :KV_LORA], wuv_ref[0])
    for h in range(1, MLA_HEADS):
        o = o + _dot(ol[:, KV_LORA * h:KV_LORA * (h + 1)], wuv_ref[h])
    yb = _dot(o.astype(BF16), wbo_ref[...])
    merged = jax.nn.sigmoid(gates[:, :D_MODEL]) * ya_ref[...] + jax.nn.sigmoid(gates[:, D_MODEL:]) * yb
    h2 = h1 + _dot(merged.astype(BF16), wout_ref[...])
    h3 = _ffn_half_step(h2, g2_ref, wg_ref, wu_ref, wd_ref)
    y_ref[...] = _rms(h3, gf_ref[...])


def _post_mixer(h1, ya, olat, w, *, seq_major_ya):
    m = h1.shape[0]
    tm = TOKEN_TILE
    row = lambda i: (i, 0)
    if seq_major_ya:
        blocks_per_seq = seq_major_ya // tm
        ya_spec = pl.BlockSpec((tm, D_MODEL), lambda i: (i % blocks_per_seq, i // blocks_per_seq))
    else:
        ya_spec = pl.BlockSpec((tm, D_MODEL), row)
    weights = (w['g_mix'], w['w_gates'], w['w_uv_pad'], w['w_b_out'], w['w_out'],
               w['g_ffn2'], w['wg2'], w['wu2'], w['wd2'], w['g_final'])
    return pl.pallas_call(
        _post_mixer_kernel,
        out_shape=jax.ShapeDtypeStruct((m, D_MODEL), F32),
        grid=(m // tm,),
        in_specs=[pl.BlockSpec((tm, D_MODEL), row), ya_spec, pl.BlockSpec((tm, LAT_LANES), row)]
        + [_resident(a.shape) for a in weights],
        out_specs=pl.BlockSpec((tm, D_MODEL), row),
        compiler_params=pltpu.CompilerParams(dimension_semantics=("parallel",),
                                             vmem_limit_bytes=V7X_VMEM_LIMIT_BYTES),
        name="post_mixer",
    )(h1, ya, olat, *weights)


def _rope_tables(pos):
    inv_freq = 1.0 / (ROPE_BASE ** (jnp.arange(0, QK_ROPE, 2, dtype=F32) / QK_ROPE))
    ang = pos.astype(F32)[:, None] * inv_freq[None, :]
    cos, sin = jnp.cos(ang), jnp.sin(ang)
    cos_tab = jnp.tile(jnp.concatenate([cos, cos], axis=-1), (1, MLA_HEADS))
    sin_tab = jnp.tile(jnp.concatenate([-sin, sin], axis=-1), (1, MLA_HEADS))
    return cos_tab, sin_tab


def _swap_halves(a):
    half = QK_ROPE // 2
    return jnp.concatenate([a[..., half:], a[..., :half]], axis=-1)


def _block_diag_runs(blocks, run):
    g, r, c = blocks.shape
    eye = jnp.eye(run, dtype=blocks.dtype)
    tiles = jnp.einsum('tgrc,gk->tgrkc', blocks.reshape(g // run, run, r, c), eye)
    return tiles.reshape(g // run, run * r, run * c)


def _prepare_weights(p):
    w = {}
    vec = lambda a: a.reshape(1, -1).astype(F32)
    for name in ('g_ffn1', 'g_mix', 'g_q', 'g_kv', 'g_ffn2', 'g_final', 'b_glu'):
        w[name] = vec(p[name])
    for src, dst in (('w_ffn1_gate', 'wg1'), ('w_ffn1_up', 'wu1'), ('w_ffn1_down', 'wd1'),
                     ('w_ffn2_gate', 'wg2'), ('w_ffn2_up', 'wu2'), ('w_ffn2_down', 'wd2'),
                     ('w_glu', 'w_glu'), ('w_a_out', 'w_a_out'), ('w_b_out', 'w_b_out'), ('w_out', 'w_out')):
        w[dst] = p[src].astype(BF16)
    w_in = p['w_in']
    off_kr = S5_WIDTH + Q_LORA + KV_LORA
    k_r = w_in[:, off_kr:off_kr + QK_ROPE]
    w['w_in_ext'] = jnp.concatenate(
        [w_in[:, :off_kr], jnp.tile(k_r, (1, MLA_HEADS)), jnp.tile(_swap_halves(k_r), (1, MLA_HEADS))],
        axis=1).astype(BF16)
    w['w_gates'] = w_in[:, off_kr + QK_ROPE:].astype(BF16)
    w_uq = p['w_uq']
    uq_rope = w_uq[:, :, QK_NOPE:]
    w['w_uq_ext'] = jnp.concatenate(
        [w_uq[:, :, :QK_NOPE].reshape(Q_LORA, -1), uq_rope.reshape(Q_LORA, -1),
         _swap_halves(uq_rope).reshape(Q_LORA, -1)], axis=1).astype(BF16)
    uk = jnp.transpose(p['w_uk'], (1, 2, 0))
    uk_pad = jnp.zeros((MLA_HEADS, 2, QK_NOPE, KV_LORA), F32)
    uk_pad = uk_pad.at[jnp.arange(MLA_HEADS), jnp.arange(MLA_HEADS) % 2].set(uk)
    w['w_uk_pad'] = uk_pad.reshape(MLA_HEADS, 2 * QK_NOPE, KV_LORA).astype(BF16)
    uv = jnp.transpose(p['w_uv'], (1, 0, 2))
    uv_pad = jnp.zeros((MLA_HEADS, KV_LORA, MLA_HEADS, V_DIM), F32)
    uv_pad = uv_pad.at[jnp.arange(MLA_HEADS), :, jnp.arange(MLA_HEADS)].set(uv)
    w['w_uv_pad'] = uv_pad.reshape(MLA_HEADS, KV_LORA, MLA_HEADS * V_DIM).astype(BF16)
    lam_re, lam_im = p['s5_a_re'].astype(F32), p['s5_a_im'].astype(F32)
    dt = jnp.exp(p['s5_log_dt'].astype(F32))[:, None]
    mag = jnp.exp(lam_re * dt)
    ab_re, ab_im = mag * jnp.cos(lam_im * dt), mag * jnp.sin(lam_im * dt)
    den = lam_re * lam_re + lam_im * lam_im
    num_re, num_im = ab_re - 1.0, ab_im
    k_re = (num_re * lam_re + num_im * lam_im) / den
    k_im = (num_im * lam_re - num_re * lam_im) / den
    w['a_re'] = ab_re.reshape(1, S5_COLS)
    w['a_im'] = ab_im.reshape(1, S5_COLS)
    b_re, b_im = p['s5_b_re'].astype(F32), p['s5_b_im'].astype(F32)
    kb_re = k_re[..., None] * b_re - k_im[..., None] * b_im
    kb_im = k_re[..., None] * b_im + k_im[..., None] * b_re

    def to_in(a):
        tiles = _block_diag_runs(jnp.transpose(a, (0, 2, 1)), S5_GROUPS_PER_STATE_TILE)
        n, r, c = tiles.shape
        pos = jnp.arange(n) % S5_TILES_PER_SLAB
        slab = jnp.zeros((n, S5_TILES_PER_SLAB, r, c), F32).at[jnp.arange(n), pos].set(tiles)
        return slab.reshape(n, S5_TILES_PER_SLAB * r, c)

    w['bpack'] = jnp.concatenate([to_in(kb_re), to_in(kb_im)], axis=0).astype(BF16)
    to_out = lambda a: _block_diag_runs(jnp.transpose(a, (0, 2, 1)), S5_GROUPS_PER_OUT_TILE)
    w['cpack'] = jnp.concatenate([to_out(p['s5_c_re'].astype(F32)), -to_out(p['s5_c_im'].astype(F32))],
                                 axis=1).astype(BF16)
    w['s5_d'] = vec(p['s5_d'])
    return w


def kernel(x_prompt, x_sample, cache_latent, cache_k_rope, state_ssm_re, state_ssm_im, page_table, g_ffn1, w_ffn1_gate, w_ffn1_up, w_ffn1_down, g_mix, w_in, s5_a_re, s5_a_im, s5_log_dt, s5_b_re, s5_b_im, s5_c_re, s5_c_im, s5_d, w_glu, b_glu, w_a_out, g_q, w_uq, g_kv, w_uk, w_uv, w_b_out, w_out, g_ffn2, w_ffn2_gate, w_ffn2_up, w_ffn2_down, g_final):
    layer = 0
    params = dict(
        g_ffn1=g_ffn1, w_ffn1_gate=w_ffn1_gate, w_ffn1_up=w_ffn1_up, w_ffn1_down=w_ffn1_down,
        g_mix=g_mix, w_in=w_in, s5_a_re=s5_a_re, s5_a_im=s5_a_im, s5_log_dt=s5_log_dt,
        s5_b_re=s5_b_re, s5_b_im=s5_b_im, s5_c_re=s5_c_re, s5_c_im=s5_c_im, s5_d=s5_d,
        w_glu=w_glu, b_glu=b_glu, w_a_out=w_a_out, g_q=g_q, w_uq=w_uq, g_kv=g_kv, w_uk=w_uk,
        w_uv=w_uv, w_b_out=w_b_out, w_out=w_out, g_ffn2=g_ffn2, w_ffn2_gate=w_ffn2_gate,
        w_ffn2_up=w_ffn2_up, w_ffn2_down=w_ffn2_down)
    p = {k: v[layer] for k, v in params.items()}
    p['g_final'] = g_final
    w = _prepare_weights(p)

    nb, seq, _ = x_prompt.shape
    cos_p, sin_p = _rope_tables(jnp.arange(seq))
    h1, u, ckv, krope, kcat, qlat, qrope = _pre_mixer(
        x_prompt.reshape(nb * seq, D_MODEL), cos_p, sin_p, w, seq_major_u=seq)
    ya, sre_p, sim_p = _s5_prompt(u.reshape(seq, nb, S5_WIDTH), w)
    olat = _attn_prompt(qlat, qrope, kcat, batch=nb, seq=seq)
    y_prompt = _post_mixer(h1, ya.reshape(seq, nb * D_MODEL), olat, w, seq_major_ya=seq).reshape(nb, seq, D_MODEL)
    ckv_prompt = ckv.reshape(1, nb, seq, KV_LORA)
    krope_prompt = krope.reshape(1, nb, seq, QK_ROPE)
    group = lambda a: a.reshape(1, a.shape[0], S5_GROUPS, S5_STATE)

    sb, n_new, _ = x_sample.shape
    n_pages = page_table.shape[1]
    past_len = n_pages * PAGE_SIZE
    cos_s, sin_s = _rope_tables(past_len + jnp.arange(n_new))
    cos_s, sin_s = jnp.tile(cos_s, (sb, 1)), jnp.tile(sin_s, (sb, 1))
    h1s, us, ckvs, kropes, _, qlats, qropes = _pre_mixer(
        x_sample.reshape(sb * n_new, D_MODEL), cos_s, sin_s, w, seq_major_u=0)
    us_t = jnp.transpose(us.reshape(sb, n_new, S5_WIDTH), (1, 0, 2))
    yas_t, sre_s, sim_s = _s5_sample(us_t, state_ssm_re[layer].reshape(sb, S5_COLS),
                                     state_ssm_im[layer].reshape(sb, S5_COLS), w)
    yas = jnp.transpose(yas_t, (1, 0, 2)).reshape(sb * n_new, D_MODEL)
    rope_pages = jnp.swapaxes(cache_k_rope[layer], 1, 2)
    ol = _attn_sample(page_table, qlats, qropes, ckvs, kropes, cache_latent[layer], rope_pages)
    olats = jnp.transpose(ol.reshape(sb, MLA_HEADS, n_new, KV_LORA), (0, 2, 1, 3))
    olats = olats.reshape(sb * n_new, LAT_LANES).astype(BF16)
    y_sample = _post_mixer(h1s, yas, olats, w, seq_major_ya=0).reshape(sb, n_new, D_MODEL)

    return (y_prompt, y_sample, ckv_prompt, krope_prompt,
            ckvs.reshape(1, sb, n_new, KV_LORA), kropes.reshape(1, sb, n_new, QK_ROPE),
            group(sre_p), group(sim_p), group(sre_s), group(sim_s))
```

```python
import functools

import jax
import jax.numpy as jnp
from jax import lax
from jax.experimental import pallas as pl
from jax.experimental.pallas import tpu as pltpu

F32 = jnp.float32
BF16 = jnp.bfloat16

D_MODEL = 1024
D_FF = 2816
S5_WIDTH = 512
S5_GROUP = 16
S5_GROUPS = 32
S5_STATE = 64
S5_COLS = S5_GROUPS * S5_STATE
MLA_HEADS = 8
QK_NOPE = 64
QK_ROPE = 32
V_DIM = 64
Q_LORA = 384
KV_LORA = 256
ROPE_BASE = 10000.0
NORM_EPS = 1e-6
PAGE_SIZE = 128
ATTN_SCALE = (QK_NOPE + QK_ROPE) ** -0.5
Q_PRESCALE = ATTN_SCALE * 1.4426950408889634
LANES = 128
MXU_TILE = 256
S5_STATE_TILES = S5_COLS // MXU_TILE
S5_GROUPS_PER_STATE_TILE = MXU_TILE // S5_STATE
S5_TILES_PER_SLAB = LANES // (S5_GROUPS_PER_STATE_TILE * S5_GROUP)
S5_OUT_TILES = S5_WIDTH // MXU_TILE
S5_GROUPS_PER_OUT_TILE = MXU_TILE // S5_GROUP
ROPE_LANES = MLA_HEADS * QK_ROPE
LAT_LANES = MLA_HEADS * KV_LORA
IN_EXT = S5_WIDTH + Q_LORA + KV_LORA + 2 * ROPE_LANES
OFF_CQ = S5_WIDTH
OFF_CKV = OFF_CQ + Q_LORA
OFF_KR = OFF_CKV + KV_LORA
OFF_KRS = OFF_KR + ROPE_LANES
NEG_BIG = -1e30

V7X_VMEM_LIMIT_BYTES = 56 * 1024 * 1024
TOKEN_TILE = 512
S5_TIME_TILE = 32
S5_COL_TILE = 512
ATTN_TILE = 512
ATTN_ROW_SPLIT = 4
ATTN_QUERY_RUNS = 2
ATTN_LOOKAHEAD = 1
SAMPLE_LOOKAHEAD = 3
SAMPLE_PAGES_PER_STEP = 64
SAMPLE_SUBCHUNKS = 8
SAMPLE_SLOTS = 3


def _rms(x, g):
    return x * lax.rsqrt(jnp.mean(x * x, axis=-1, keepdims=True) + NORM_EPS) * g


def _dot(a, b):
    return jnp.dot(a, b, preferred_element_type=F32)


def _dot_nt(a, b):
    return lax.dot_general(a, b, (((1,), (1,)), ((), ())), preferred_element_type=F32)


def _ffn_half_step(x, g_ref, wg_ref, wu_ref, wd_ref):
    xn = _rms(x, g_ref[...]).astype(BF16)
    gate = _dot(xn, wg_ref[...])
    up = _dot(xn, wu_ref[...])
    hid = (jax.nn.silu(gate) * up).astype(BF16)
    return x + 0.5 * _dot(hid, wd_ref[...])


def _resident(shape):
    nd = len(shape)
    return pl.BlockSpec(shape, lambda *_: (0,) * nd, pipeline_mode=pl.Buffered(1))


def _pre_mixer_kernel(x_ref, cos_ref, sin_ref, g1_ref, wg_ref, wu_ref, wd_ref, gm_ref, win_ref,
                      gq_ref, gkv_ref, wuq_ref, wuk_ref,
                      h1_ref, u_ref, ckv_ref, krope_ref, kcat_ref, qlat_ref, qrope_ref):
    h1 = _ffn_half_step(x_ref[...], g1_ref, wg_ref, wu_ref, wd_ref)
    h1_ref[...] = h1
    xm = _rms(h1, gm_ref[...]).astype(BF16)
    proj = _dot(xm, win_ref[...])
    u_ref[...] = proj[:, :S5_WIDTH]
    cqn = _rms(proj[:, OFF_CQ:OFF_CKV], gq_ref[...]).astype(BF16)
    ckv = _rms(proj[:, OFF_CKV:OFF_KR], gkv_ref[...])
    ckv_ref[...] = ckv
    cos = cos_ref[...]
    sin = sin_ref[...]
    kr8 = proj[:, OFF_KR:OFF_KRS] * cos + proj[:, OFF_KRS:IN_EXT] * sin
    krope_ref[...] = kr8[:, :QK_ROPE]
    kcat_ref[:, :KV_LORA] = ckv.astype(BF16)
    kcat_ref[:, KV_LORA:] = kr8.astype(BF16)
    q = _dot(cqn, wuq_ref[...])
    n_nope = MLA_HEADS * QK_NOPE
    qrope_ref[...] = ((q[:, n_nope:n_nope + ROPE_LANES] * cos
                       + q[:, n_nope + ROPE_LANES:] * sin) * Q_PRESCALE).astype(BF16)
    qn = q[:, :n_nope].astype(BF16)
    for h in range(MLA_HEADS):
        pair = h // 2
        qlat_ref[:, KV_LORA * h:KV_LORA * (h + 1)] = (_dot(
            qn[:, LANES * pair:LANES * (pair + 1)], wuk_ref[h]) * Q_PRESCALE).astype(BF16)


def _pre_mixer(x, cos_tab, sin_tab, w, *, seq_major_u):
    m = x.shape[0]
    tm = TOKEN_TILE
    nsteps = m // tm
    tab_blocks = cos_tab.shape[0] // tm
    row = lambda i: (i, 0)
    tab = lambda i: (i % tab_blocks, 0)
    if seq_major_u:
        blocks_per_seq = seq_major_u // tm
        u_shape = (seq_major_u, (m // seq_major_u) * S5_WIDTH)
        u_spec = pl.BlockSpec((tm, S5_WIDTH), lambda i: (i % blocks_per_seq, i // blocks_per_seq))
    else:
        u_shape = (m, S5_WIDTH)
        u_spec = pl.BlockSpec((tm, S5_WIDTH), row)
    out_shape = (
        jax.ShapeDtypeStruct((m, D_MODEL), F32),
        jax.ShapeDtypeStruct(u_shape, F32),
        jax.ShapeDtypeStruct((m, KV_LORA), F32),
        jax.ShapeDtypeStruct((m, QK_ROPE), F32),
        jax.ShapeDtypeStruct((m, KV_LORA + ROPE_LANES), BF16),
        jax.ShapeDtypeStruct((m, LAT_LANES), BF16),
        jax.ShapeDtypeStruct((m, ROPE_LANES), BF16),
    )
    out_specs = (
        pl.BlockSpec((tm, D_MODEL), row),
        u_spec,
        pl.BlockSpec((tm, KV_LORA), row),
        pl.BlockSpec((tm, QK_ROPE), row),
        pl.BlockSpec((tm, KV_LORA + ROPE_LANES), row),
        pl.BlockSpec((tm, LAT_LANES), row),
        pl.BlockSpec((tm, ROPE_LANES), row),
    )
    weights = (w['g_ffn1'], w['wg1'], w['wu1'], w['wd1'], w['g_mix'], w['w_in_ext'],
               w['g_q'], w['g_kv'], w['w_uq_ext'], w['w_uk_pad'])
    in_specs = [pl.BlockSpec((tm, D_MODEL), row),
                pl.BlockSpec((tm, ROPE_LANES), tab),
                pl.BlockSpec((tm, ROPE_LANES), tab)] + [_resident(a.shape) for a in weights]
    return pl.pallas_call(
        _pre_mixer_kernel, out_shape=out_shape, grid=(nsteps,), in_specs=in_specs, out_specs=out_specs,
        compiler_params=pltpu.CompilerParams(dimension_semantics=("parallel",),
                                             vmem_limit_bytes=V7X_VMEM_LIMIT_BYTES),
        name="pre_mixer",
    )(x, cos_tab, sin_tab, *weights)


def _s5_readout(h_ref, u, cpack_ref, d_ref, wglu_ref, bglu_ref, waout_ref):
    k = S5_COLS // S5_OUT_TILES
    tiles = []
    for n in range(S5_OUT_TILES):
        h_re = h_ref[:, k * n:k * (n + 1)].astype(BF16)
        h_im = h_ref[:, S5_COLS + k * n:S5_COLS + k * (n + 1)].astype(BF16)
        tiles.append(_dot(h_re, cpack_ref[n, :k]) + _dot(h_im, cpack_ref[n, k:]))
    y = jnp.concatenate(tiles, axis=1) + d_ref[...] * u
    z = jax.nn.gelu(y)
    gate = _dot(z.astype(BF16), wglu_ref[...]) + bglu_ref[...]
    return _dot((z * jax.nn.sigmoid(gate)).astype(BF16), waout_ref[...])


def _s5_drive(u, bpack_ref, xs_ref):
    u_bf = u.astype(BF16)
    for j in range(2 * S5_STATE_TILES):
        slab = (j % S5_STATE_TILES) // S5_TILES_PER_SLAB
        xs_ref[:, MXU_TILE * j:MXU_TILE * (j + 1)] = _dot(u_bf[:, LANES * slab:LANES * (slab + 1)], bpack_ref[j])


def _s5_prompt_kernel(u_ref, bpack_ref, are_ref, aim_ref, cpack_ref, d_ref, wglu_ref, bglu_ref, waout_ref,
                      ya_ref, sre_ref, sim_ref, xs_ref, hre_ref, him_ref):
    tt, nb, _ = u_ref.shape
    rows = tt * nb

    @pl.when(pl.program_id(0) == 0)
    def _():
        hre_ref[...] = jnp.zeros_like(hre_ref)
        him_ref[...] = jnp.zeros_like(him_ref)

    u = u_ref[...].reshape(rows, S5_WIDTH)
    _s5_drive(u, bpack_ref, xs_ref)
    for cb in range(S5_COLS // S5_COL_TILE):
        lo = cb * S5_COL_TILE
        re_cols = pl.ds(lo, S5_COL_TILE)
        im_cols = pl.ds(S5_COLS + lo, S5_COL_TILE)
        ar = jnp.broadcast_to(are_ref[:, re_cols], (nb, S5_COL_TILE))
        ai = jnp.broadcast_to(aim_ref[:, re_cols], (nb, S5_COL_TILE))

        def body(t, carry):
            hr, hi = carry
            r = pl.ds(pl.multiple_of(t * nb, nb), nb)
            nr = ar * hr - ai * hi + xs_ref[r, re_cols]
            ni = ar * hi + ai * hr + xs_ref[r, im_cols]
            xs_ref[r, re_cols] = nr
            xs_ref[r, im_cols] = ni
            return nr, ni

        hr, hi = lax.fori_loop(0, tt, body, (hre_ref[:, re_cols], him_ref[:, re_cols]), unroll=4)
        hre_ref[:, re_cols] = hr
        him_ref[:, re_cols] = hi
    ya = _s5_readout(xs_ref, u, cpack_ref, d_ref, wglu_ref, bglu_ref, waout_ref)
    ya_ref[...] = ya.reshape(tt, nb, D_MODEL)
    sre_ref[...] = hre_ref[...]
    sim_ref[...] = him_ref[...]


def _s5_prompt(u3, w):
    seq, nb, _ = u3.shape
    tt = S5_TIME_TILE
    weights = (w['bpack'], w['a_re'], w['a_im'], w['cpack'], w['s5_d'], w['w_glu'], w['b_glu'], w['w_a_out'])
    state = pl.BlockSpec((nb, S5_COLS), lambda i: (0, 0))
    return pl.pallas_call(
        _s5_prompt_kernel,
        out_shape=(jax.ShapeDtypeStruct((seq, nb, D_MODEL), F32),
                   jax.ShapeDtypeStruct((nb, S5_COLS), F32),
                   jax.ShapeDtypeStruct((nb, S5_COLS), F32)),
        grid=(seq // tt,),
        in_specs=[pl.BlockSpec((tt, nb, S5_WIDTH), lambda i: (i, 0, 0))] + [_resident(a.shape) for a in weights],
        out_specs=(pl.BlockSpec((tt, nb, D_MODEL), lambda i: (i, 0, 0)), state, state),
        scratch_shapes=[pltpu.VMEM((tt * nb, 2 * S5_COLS), F32),
                        pltpu.VMEM((nb, S5_COLS), F32),
                        pltpu.VMEM((nb, S5_COLS), F32)],
        compiler_params=pltpu.CompilerParams(dimension_semantics=("arbitrary",),
                                             vmem_limit_bytes=V7X_VMEM_LIMIT_BYTES),
        name="s5_prompt",
    )(u3, *weights)


def _s5_sample_kernel(u_ref, h0re_ref, h0im_ref, bpack_ref, are_ref, aim_ref, cpack_ref, d_ref, wglu_ref,
                      bglu_ref, waout_ref, ya_ref, sre_ref, sim_ref, hs_ref):
    tt, nb, _ = u_ref.shape
    u = u_ref[...].reshape(tt * nb, S5_WIDTH)
    _s5_drive(u, bpack_ref, hs_ref)
    ar = are_ref[...]
    ai = aim_ref[...]
    hr = h0re_ref[...]
    hi = h0im_ref[...]
    for t in range(tt):
        r = pl.ds(t * nb, nb)
        hr, hi = (ar * hr - ai * hi + hs_ref[r, :S5_COLS], ar * hi + ai * hr + hs_ref[r, S5_COLS:])
        hs_ref[r, :S5_COLS] = hr
        hs_ref[r, S5_COLS:] = hi
    sre_ref[...] = hr
    sim_ref[...] = hi
    ya = _s5_readout(hs_ref, u, cpack_ref, d_ref, wglu_ref, bglu_ref, waout_ref)
    ya_ref[...] = ya.reshape(tt, nb, D_MODEL)


def _s5_sample(u3, h0_re, h0_im, w):
    tt, nb, _ = u3.shape
    weights = (w['bpack'], w['a_re'], w['a_im'], w['cpack'], w['s5_d'], w['w_glu'], w['b_glu'], w['w_a_out'])
    args = (u3, h0_re, h0_im) + weights
    whole = lambda a: pl.BlockSpec(a.shape, lambda i, nd=a.ndim: (0,) * nd)
    outs = (jax.ShapeDtypeStruct((tt, nb, D_MODEL), F32),
            jax.ShapeDtypeStruct((nb, S5_COLS), F32),
            jax.ShapeDtypeStruct((nb, S5_COLS), F32))
    return pl.pallas_call(
        _s5_sample_kernel, out_shape=outs, grid=(1,),
        in_specs=[whole(a) for a in args], out_specs=tuple(whole(o) for o in outs),
        scratch_shapes=[pltpu.VMEM((tt * nb, 2 * S5_COLS), F32)],
        compiler_params=pltpu.CompilerParams(dimension_semantics=("arbitrary",),
                                             vmem_limit_bytes=V7X_VMEM_LIMIT_BYTES),
        name="s5_sample",
    )(*args)


def _lanes(stat, width):
    if width % LANES:
        return stat[:, :width]
    return jnp.tile(stat, (1, width // LANES))


def _softmax_block_update(s, v, rows, m_ref, l_ref, acc_ref):
    m_old = m_ref[rows, :]
    m_new = jnp.maximum(m_old, jnp.max(s, axis=-1, keepdims=True))
    alpha = jnp.exp2(m_old - m_new)
    p = jnp.exp2(s - _lanes(m_new, s.shape[1]))
    l_ref[rows, :] = alpha * l_ref[rows, :] + jnp.sum(p, axis=-1, keepdims=True)
    acc_ref[rows, :] = _lanes(alpha, KV_LORA) * acc_ref[rows, :] + _dot(p.astype(BF16), v)
    m_ref[rows, :] = m_new


def _attn_prompt_kernel(qlat_ref, qrope_ref, kcat_ref, o_ref, qs_ref, m_ref, l_ref, acc_ref):
    tq = ATTN_TILE
    sub = tq // ATTN_QUERY_RUNS
    qi = pl.program_id(1)
    lane_head = lax.broadcasted_iota(jnp.int32, (sub, ROPE_LANES), 1) // QK_ROPE
    for run in range(ATTN_QUERY_RUNS):
        toks = pl.ds(run * sub, sub)
        qr = qrope_ref[toks, :]
        for h in range(MLA_HEADS):
            rows = pl.ds((run * MLA_HEADS + h) * sub, sub)
            qs_ref[rows, :KV_LORA] = qlat_ref[toks, KV_LORA * h:KV_LORA * (h + 1)]
            qs_ref[rows, KV_LORA:] = jnp.where(lane_head == h, qr, jnp.zeros_like(qr))
    m_ref[...] = jnp.full_like(m_ref, NEG_BIG)
    l_ref[...] = jnp.zeros_like(l_ref)
    acc_ref[...] = jnp.zeros_like(acc_ref)
    n = MLA_HEADS * tq // ATTN_ROW_SPLIT
    groups_per_run = ATTN_ROW_SPLIT // ATTN_QUERY_RUNS

    def block(k, causal):
        kc = kcat_ref[pl.ds(pl.multiple_of(k * tq, tq), tq), :]

        def visible(part):
            run = part // groups_per_run
            return kc[:(run + 1) * sub] if causal else kc

        def scores(part):
            s = _dot_nt(qs_ref[pl.ds(part * n, n), :], visible(part))
            if causal:
                run = part // groups_per_run
                r = (lax.broadcasted_iota(jnp.int32, s.shape, 0) & (sub - 1)) + run * sub
                c = lax.broadcasted_iota(jnp.int32, s.shape, 1)
                s = jnp.where(c <= r, s, NEG_BIG)
            return s

        ready = [scores(part) for part in range(min(ATTN_LOOKAHEAD, ATTN_ROW_SPLIT))]
        for part in range(ATTN_ROW_SPLIT):
            if part + ATTN_LOOKAHEAD < ATTN_ROW_SPLIT:
                ready.append(scores(part + ATTN_LOOKAHEAD))
            _softmax_block_update(ready.pop(0), visible(part)[:, :KV_LORA], pl.ds(part * n, n),
                                  m_ref, l_ref, acc_ref)

    def past(k, carry):
        block(k, False)
        return carry

    lax.fori_loop(0, qi, past, 0)
    block(qi, True)
    o = acc_ref[...] / _lanes(l_ref[...], KV_LORA)
    for run in range(ATTN_QUERY_RUNS):
        for h in range(MLA_HEADS):
            seg = run * MLA_HEADS + h
            o_ref[pl.ds(run * sub, sub), KV_LORA * h:KV_LORA * (h + 1)] = o[seg * sub:(seg + 1) * sub].astype(BF16)


def _attn_prompt(qlat, qrope, kcat, *, batch, seq):
    tq = ATTN_TILE
    nq = seq // tq
    rows = MLA_HEADS * tq
    qmap = lambda b, i: (b * nq + i, 0)
    return pl.pallas_call(
        _attn_prompt_kernel,
        out_shape=jax.ShapeDtypeStruct((batch * seq, LAT_LANES), BF16),
        grid=(batch, nq),
        in_specs=[pl.BlockSpec((tq, LAT_LANES), qmap),
                  pl.BlockSpec((tq, ROPE_LANES), qmap),
                  pl.BlockSpec((seq, KV_LORA + ROPE_LANES), lambda b, i: (b, 0))],
        out_specs=pl.BlockSpec((tq, LAT_LANES), qmap),
        scratch_shapes=[pltpu.VMEM((rows, KV_LORA + ROPE_LANES), BF16),
                        pltpu.VMEM((rows, LANES), F32),
                        pltpu.VMEM((rows, LANES), F32),
                        pltpu.VMEM((rows, KV_LORA), F32)],
        compiler_params=pltpu.CompilerParams(dimension_semantics=("parallel", "arbitrary"),
                                             vmem_limit_bytes=V7X_VMEM_LIMIT_BYTES),
        name="attn_prompt",
    )(qlat, qrope, kcat)


def _softmax_part(s, v):
    m = jnp.max(s, axis=-1, keepdims=True)
    p = jnp.exp2(s - m)
    return m, jnp.sum(p, axis=-1, keepdims=True), _dot(p.astype(BF16), v)


def _merge_softmax_parts(parts, m_ref, l_ref, acc_ref):
    m_old = m_ref[...]
    m_new = m_old
    for m, _, _ in parts:
        m_new = jnp.maximum(m_new, m)
    alpha = jnp.exp2(m_old - m_new)
    l = alpha * l_ref[...]
    acc = _lanes(alpha, KV_LORA) * acc_ref[...]
    for m, psum, pv in parts:
        wgt = jnp.exp2(m - m_new)
        l = l + wgt * psum
        acc = acc + _lanes(wgt, KV_LORA) * pv
    m_ref[...] = m_new
    l_ref[...] = l
    acc_ref[...] = acc


def _attn_sample_kernel(pt_ref, qlat_ref, qrope_ref, ckvn_ref, kropen_ref, cl_hbm, cr_hbm, o_ref,
                        kbuf, rbuf, sem, qs_ref, qr_ref, kn_ref, rn_ref, m_ref, l_ref, acc_ref,
                        *, n_pages, n_new):
    pg = SAMPLE_PAGES_PER_STEP
    chunks = n_pages // pg
    g = pl.program_id(0)
    n_steps = pl.num_programs(0)
    c = g % chunks
    last = n_steps - 1
    slot = lax.rem(g, SAMPLE_SLOTS)
    ahead = SAMPLE_SLOTS - 1

    def page_copies(step, dst_slot):
        first = (step // chunks) * n_pages + (step % chunks) * pg
        copies = []
        for p in range(pg):
            page = pt_ref[first + p]
            keys = pl.ds(p * PAGE_SIZE, PAGE_SIZE)
            copies.append(pltpu.make_async_copy(cl_hbm.at[page], kbuf.at[dst_slot, keys], sem.at[0, dst_slot]))
            copies.append(pltpu.make_async_copy(cr_hbm.at[page], rbuf.at[dst_slot, :, keys], sem.at[1, dst_slot]))
        return copies

    @pl.when(g == 0)
    def _():
        for step in range(ahead):
            for cp in page_copies(step, step):
                cp.start()

    @pl.when(c == 0)
    def _():
        ql = qlat_ref[0].astype(F32)
        qr = qrope_ref[0].astype(F32)
        for h in range(MLA_HEADS):
            rows = pl.ds(h * n_new, n_new)
            qs_ref[rows, :] = ql[:, KV_LORA * h:KV_LORA * (h + 1)]
            qr_ref[rows, :] = qr[:, QK_ROPE * h:QK_ROPE * (h + 1)]
        m_ref[...] = jnp.full_like(m_ref, NEG_BIG)
        l_ref[...] = jnp.zeros_like(l_ref)
        acc_ref[...] = jnp.zeros_like(acc_ref)

    for cp in page_copies(g, slot):
        cp.wait()

    qs = qs_ref[...].astype(BF16)
    qr = qr_ref[...].astype(BF16)
    sub = pg * PAGE_SIZE // SAMPLE_SUBCHUNKS
    def scores(j):
        keys = pl.ds(j * sub, sub)
        kc = kbuf[slot, keys, :].astype(BF16)
        return _dot_nt(qs, kc) + _dot(qr, rbuf[slot, :, keys].astype(BF16)), kc

    parts = []
    ready = [scores(j) for j in range(min(SAMPLE_LOOKAHEAD, SAMPLE_SUBCHUNKS))]
    for j in range(SAMPLE_SUBCHUNKS):
        if j + SAMPLE_LOOKAHEAD < SAMPLE_SUBCHUNKS:
            ready.append(scores(j + SAMPLE_LOOKAHEAD))
        parts.append(_softmax_part(*ready.pop(0)))
    _merge_softmax_parts(parts, m_ref, l_ref, acc_ref)

    for cp in page_copies(jnp.minimum(g + ahead, last), lax.rem(g + ahead, SAMPLE_SLOTS)):
        cp.start()

    @pl.when(g == last)
    def _():
        for extra in range(1, ahead + 1):
            for cp in page_copies(last, lax.rem(g + extra, SAMPLE_SLOTS)):
                cp.wait()

    @pl.when(c == chunks - 1)
    def _():
        kn_ref[...] = jnp.zeros_like(kn_ref)
        rn_ref[...] = jnp.zeros_like(rn_ref)
        kn_ref[pl.ds(0, n_new), :] = ckvn_ref[0]
        rn_ref[pl.ds(0, n_new), :] = kropen_ref[0]
        kn = kn_ref[...].astype(BF16)
        sn = _dot_nt(qs, kn) + _dot_nt(qr, rn_ref[...].astype(BF16))
        t = lax.broadcasted_iota(jnp.int32, sn.shape, 0) % n_new
        j = lax.broadcasted_iota(jnp.int32, sn.shape, 1)
        sn = jnp.where(j <= t, sn, NEG_BIG)
        _merge_softmax_parts([_softmax_part(sn, kn)], m_ref, l_ref, acc_ref)
        o_ref[0] = acc_ref[...] / _lanes(l_ref[...], KV_LORA)


def _attn_sample(page_table, qlat, qrope, ckv_new, krope_new, cache_latent, cache_k_rope):
    nb, n_pages = page_table.shape
    n_new = qlat.shape[0] // nb
    pg = SAMPLE_PAGES_PER_STEP
    chunks = n_pages // pg
    rows = MLA_HEADS * n_new
    new_pad = 8
    per_batch = lambda g, pt: (g // chunks, 0, 0)
    grid_spec = pltpu.PrefetchScalarGridSpec(
        num_scalar_prefetch=1,
        grid=(nb * chunks,),
        in_specs=[pl.BlockSpec((1, n_new, LAT_LANES), per_batch),
                  pl.BlockSpec((1, n_new, ROPE_LANES), per_batch),
                  pl.BlockSpec((1, n_new, KV_LORA), per_batch),
                  pl.BlockSpec((1, n_new, QK_ROPE), per_batch),
                  pl.BlockSpec(memory_space=pl.ANY),
                  pl.BlockSpec(memory_space=pl.ANY)],
        out_specs=pl.BlockSpec((1, rows, KV_LORA), per_batch),
        scratch_shapes=[pltpu.VMEM((SAMPLE_SLOTS, pg * PAGE_SIZE, KV_LORA), F32),
                        pltpu.VMEM((SAMPLE_SLOTS, QK_ROPE, pg * PAGE_SIZE), F32),
                        pltpu.SemaphoreType.DMA((2, SAMPLE_SLOTS)),
                        pltpu.VMEM((rows, KV_LORA), F32),
                        pltpu.VMEM((rows, QK_ROPE), F32),
                        pltpu.VMEM((new_pad, KV_LORA), F32),
                        pltpu.VMEM((new_pad, QK_ROPE), F32),
                        pltpu.VMEM((rows, LANES), F32),
                        pltpu.VMEM((rows, LANES), F32),
                        pltpu.VMEM((rows, KV_LORA), F32)])
    return pl.pallas_call(
        functools.partial(_attn_sample_kernel, n_pages=n_pages, n_new=n_new),
        out_shape=jax.ShapeDtypeStruct((nb, rows, KV_LORA), F32),
        grid_spec=grid_spec,
        compiler_params=pltpu.CompilerParams(dimension_semantics=("arbitrary",),
                                             vmem_limit_bytes=V7X_VMEM_LIMIT_BYTES),
        name="attn_sample",
    )(page_table.reshape(-1), qlat.reshape(nb, n_new, LAT_LANES), qrope.reshape(nb, n_new, ROPE_LANES),
      ckv_new.reshape(nb, n_new, KV_LORA), krope_new.reshape(nb, n_new, QK_ROPE), cache_latent, cache_k_rope)


def _post_mixer_kernel(h1_ref, ya_ref, ol_ref, gm_ref, wgt_ref, wuv_ref, wbo_ref, wout_ref,
                       g2_ref, wg_ref, wu_ref, wd_ref, gf_ref, y_ref):
    h1 = h1_ref[...]
    xm = _rms(h1, gm_ref[...]).astype(BF16)
    gates = _dot(xm, wgt_ref[...])
    ol = ol_ref[...]
    o = _dot(ol[:, :KV_LORA], wuv_ref[0])
    for h in range(1, MLA_HEADS):
        o = o + _dot(ol[:, KV_LORA * h:KV_LORA * (h + 1)], wuv_ref[h])
    yb = _dot(o.astype(BF16), wbo_ref[...])
    merged = jax.nn.sigmoid(gates[:, :D_MODEL]) * ya_ref[...] + jax.nn.sigmoid(gates[:, D_MODEL:]) * yb
    h2 = h1 + _dot(merged.astype(BF16), wout_ref[...])
    h3 = _ffn_half_step(h2, g2_ref, wg_ref, wu_ref, wd_ref)
    y_ref[...] = _rms(h3, gf_ref[...])


def _post_mixer(h1, ya, olat, w, *, seq_major_ya):
    m = h1.shape[0]
    tm = TOKEN_TILE
    row = lambda i: (i, 0)
    if seq_major_ya:
        blocks_per_seq = seq_major_ya // tm
        ya_spec = pl.BlockSpec((tm, D_MODEL), lambda i: (i % blocks_per_seq, i // blocks_per_seq))
    else:
        ya_spec = pl.BlockSpec((tm, D_MODEL), row)
    weights = (w['g_mix'], w['w_gates'], w['w_uv_pad'], w['w_b_out'], w['w_out'],
               w['g_ffn2'], w['wg2'], w['wu2'], w['wd2'], w['g_final'])
    return pl.pallas_call(
        _post_mixer_kernel,
        out_shape=jax.ShapeDtypeStruct((m, D_MODEL), F32),
        grid=(m // tm,),
        in_specs=[pl.BlockSpec((tm, D_MODEL), row), ya_spec, pl.BlockSpec((tm, LAT_LANES), row)]
        + [_resident(a.shape) for a in weights],
        out_specs=pl.BlockSpec((tm, D_MODEL), row),
        compiler_params=pltpu.CompilerParams(dimension_semantics=("parallel",),
                                             vmem_limit_bytes=V7X_VMEM_LIMIT_BYTES),
        name="post_mixer",
    )(h1, ya, olat, *weights)


def _rope_tables(pos):
    inv_freq = 1.0 / (ROPE_BASE ** (jnp.arange(0, QK_ROPE, 2, dtype=F32) / QK_ROPE))
    ang = pos.astype(F32)[:, None] * inv_freq[None, :]
    cos, sin = jnp.cos(ang), jnp.sin(ang)
    cos_tab = jnp.tile(jnp.concatenate([cos, cos], axis=-1), (1, MLA_HEADS))
    sin_tab = jnp.tile(jnp.concatenate([-sin, sin], axis=-1), (1, MLA_HEADS))
    return cos_tab, sin_tab


def _swap_halves(a):
    half = QK_ROPE // 2
    return jnp.concatenate([a[..., half:], a[..., :half]], axis=-1)


def _block_diag_runs(blocks, run):
    g, r, c = blocks.shape
    eye = jnp.eye(run, dtype=blocks.dtype)
    tiles = jnp.einsum('tgrc,gk->tgrkc', blocks.reshape(g // run, run, r, c), eye)
    return tiles.reshape(g // run, run * r, run * c)


def _prepare_weights(p):
    w = {}
    vec = lambda a: a.reshape(1, -1).astype(F32)
    for name in ('g_ffn1', 'g_mix', 'g_q', 'g_kv', 'g_ffn2', 'g_final', 'b_glu'):
        w[name] = vec(p[name])
    for src, dst in (('w_ffn1_gate', 'wg1'), ('w_ffn1_up', 'wu1'), ('w_ffn1_down', 'wd1'),
                     ('w_ffn2_gate', 'wg2'), ('w_ffn2_up', 'wu2'), ('w_ffn2_down', 'wd2'),
                     ('w_glu', 'w_glu'), ('w_a_out', 'w_a_out'), ('w_b_out', 'w_b_out'), ('w_out', 'w_out')):
        w[dst] = p[src].astype(BF16)
    w_in = p['w_in']
    off_kr = S5_WIDTH + Q_LORA + KV_LORA
    k_r = w_in[:, off_kr:off_kr + QK_ROPE]
    w['w_in_ext'] = jnp.concatenate(
        [w_in[:, :off_kr], jnp.tile(k_r, (1, MLA_HEADS)), jnp.tile(_swap_halves(k_r), (1, MLA_HEADS))],
        axis=1).astype(BF16)
    w['w_gates'] = w_in[:, off_kr + QK_ROPE:].astype(BF16)
    w_uq = p['w_uq']
    uq_rope = w_uq[:, :, QK_NOPE:]
    w['w_uq_ext'] = jnp.concatenate(
        [w_uq[:, :, :QK_NOPE].reshape(Q_LORA, -1), uq_rope.reshape(Q_LORA, -1),
         _swap_halves(uq_rope).reshape(Q_LORA, -1)], axis=1).astype(BF16)
    uk = jnp.transpose(p['w_uk'], (1, 2, 0))
    uk_pad = jnp.zeros((MLA_HEADS, 2, QK_NOPE, KV_LORA), F32)
    uk_pad = uk_pad.at[jnp.arange(MLA_HEADS), jnp.arange(MLA_HEADS) % 2].set(uk)
    w['w_uk_pad'] = uk_pad.reshape(MLA_HEADS, 2 * QK_NOPE, KV_LORA).astype(BF16)
    uv = jnp.transpose(p['w_uv'], (1, 0, 2))
    uv_pad = jnp.zeros((MLA_HEADS, KV_LORA, MLA_HEADS, V_DIM), F32)
    uv_pad = uv_pad.at[jnp.arange(MLA_HEADS), :, jnp.arange(MLA_HEADS)].set(uv)
    w['w_uv_pad'] = uv_pad.reshape(MLA_HEADS, KV_LORA, MLA_HEADS * V_DIM).astype(BF16)
    lam_re, lam_im = p['s5_a_re'].astype(F32), p['s5_a_im'].astype(F32)
    dt = jnp.exp(p['s5_log_dt'].astype(F32))[:, None]
    mag = jnp.exp(lam_re * dt)
    ab_re, ab_im = mag * jnp.cos(lam_im * dt), mag * jnp.sin(lam_im * dt)
    den = lam_re * lam_re + lam_im * lam_im
    num_re, num_im = ab_re - 1.0, ab_im
    k_re = (num_re * lam_re + num_im * lam_im) / den
    k_im = (num_im * lam_re - num_re * lam_im) / den
    w['a_re'] = ab_re.reshape(1, S5_COLS)
    w['a_im'] = ab_im.reshape(1, S5_COLS)
    b_re, b_im = p['s5_b_re'].astype(F32), p['s5_b_im'].astype(F32)
    kb_re = k_re[..., None] * b_re - k_im[..., None] * b_im
    kb_im = k_re[..., None] * b_im + k_im[..., None] * b_re

    def to_in(a):
        tiles = _block_diag_runs(jnp.transpose(a, (0, 2, 1)), S5_GROUPS_PER_STATE_TILE)
        n, r, c = tiles.shape
        pos = jnp.arange(n) % S5_TILES_PER_SLAB
        slab = jnp.zeros((n, S5_TILES_PER_SLAB, r, c), F32).at[jnp.arange(n), pos].set(tiles)
        return slab.reshape(n, S5_TILES_PER_SLAB * r, c)

    w['bpack'] = jnp.concatenate([to_in(kb_re), to_in(kb_im)], axis=0).astype(BF16)
    to_out = lambda a: _block_diag_runs(jnp.transpose(a, (0, 2, 1)), S5_GROUPS_PER_OUT_TILE)
    w['cpack'] = jnp.concatenate([to_out(p['s5_c_re'].astype(F32)), -to_out(p['s5_c_im'].astype(F32))],
                                 axis=1).astype(BF16)
    w['s5_d'] = vec(p['s5_d'])
    return w


def kernel(x_prompt, x_sample, cache_latent, cache_k_rope, state_ssm_re, state_ssm_im, page_table, g_ffn1, w_ffn1_gate, w_ffn1_up, w_ffn1_down, g_mix, w_in, s5_a_re, s5_a_im, s5_log_dt, s5_b_re, s5_b_im, s5_c_re, s5_c_im, s5_d, w_glu, b_glu, w_a_out, g_q, w_uq, g_kv, w_uk, w_uv, w_b_out, w_out, g_ffn2, w_ffn2_gate, w_ffn2_up, w_ffn2_down, g_final):
    layer = 0
    params = dict(
        g_ffn1=g_ffn1, w_ffn1_gate=w_ffn1_gate, w_ffn1_up=w_ffn1_up, w_ffn1_down=w_ffn1_down,
        g_mix=g_mix, w_in=w_in, s5_a_re=s5_a_re, s5_a_im=s5_a_im, s5_log_dt=s5_log_dt,
        s5_b_re=s5_b_re, s5_b_im=s5_b_im, s5_c_re=s5_c_re, s5_c_im=s5_c_im, s5_d=s5_d,
        w_glu=w_glu, b_glu=b_glu, w_a_out=w_a_out, g_q=g_q, w_uq=w_uq, g_kv=g_kv, w_uk=w_uk,
        w_uv=w_uv, w_b_out=w_b_out, w_out=w_out, g_ffn2=g_ffn2, w_ffn2_gate=w_ffn2_gate,
        w_ffn2_up=w_ffn2_up, w_ffn2_down=w_ffn2_down)
    p = {k: v[layer] for k, v in params.items()}
    p['g_final'] = g_final
    w = _prepare_weights(p)

    nb, seq, _ = x_prompt.shape
    cos_p, sin_p = _rope_tables(jnp.arange(seq))
    h1, u, ckv, krope, kcat, qlat, qrope = _pre_mixer(
        x_prompt.reshape(nb * seq, D_MODEL), cos_p, sin_p, w, seq_major_u=seq)
    ya, sre_p, sim_p = _s5_prompt(u.reshape(seq, nb, S5_WIDTH), w)
    olat = _attn_prompt(qlat, qrope, kcat, batch=nb, seq=seq)
    y_prompt = _post_mixer(h1, ya.reshape(seq, nb * D_MODEL), olat, w, seq_major_ya=seq).reshape(nb, seq, D_MODEL)
    ckv_prompt = ckv.reshape(1, nb, seq, KV_LORA)
    krope_prompt = krope.reshape(1, nb, seq, QK_ROPE)
    group = lambda a: a.reshape(1, a.shape[0], S5_GROUPS, S5_STATE)

    sb, n_new, _ = x_sample.shape
    n_pages = page_table.shape[1]
    past_len = n_pages * PAGE_SIZE
    cos_s, sin_s = _rope_tables(past_len + jnp.arange(n_new))
    cos_s, sin_s = jnp.tile(cos_s, (sb, 1)), jnp.tile(sin_s, (sb, 1))
    h1s, us, ckvs, kropes, _, qlats, qropes = _pre_mixer(
        x_sample.reshape(sb * n_new, D_MODEL), cos_s, sin_s, w, seq_major_u=0)
    us_t = jnp.transpose(us.reshape(sb, n_new, S5_WIDTH), (1, 0, 2))
    yas_t, sre_s, sim_s = _s5_sample(us_t, state_ssm_re[layer].reshape(sb, S5_COLS),
                                     state_ssm_im[layer].reshape(sb, S5_COLS), w)
    yas = jnp.transpose(yas_t, (1, 0, 2)).reshape(sb * n_new, D_MODEL)
    rope_pages = jnp.swapaxes(cache_k_rope[layer], 1, 2)
    ol = _attn_sample(page_table, qlats, qropes, ckvs, kropes, cache_latent[layer], rope_pages)
    olats = jnp.transpose(ol.reshape(sb, MLA_HEADS, n_new, KV_LORA), (0, 2, 1, 3))
    olats = olats.reshape(sb * n_new, LAT_LANES).astype(BF16)
    y_sample = _post_mixer(h1s, yas, olats, w, seq_major_ya=0).reshape(sb, n_new, D_MODEL)

    return (y_prompt, y_sample, ckv_prompt, krope_prompt,
            ckvs.reshape(1, sb, n_new, KV_LORA), kropes.reshape(1, sb, n_new, QK_ROPE),
            group(sre_p), group(sim_p), group(sre_s), group(sim_s))
```

```python
import functools

import jax
import jax.numpy as jnp
from jax import lax
from jax.experimental import pallas as pl
from jax.experimental.pallas import tpu as pltpu

F32 = jnp.float32
BF16 = jnp.bfloat16

D_MODEL = 1024
D_FF = 2816
S5_WIDTH = 512
S5_GROUP = 16
S5_GROUPS = 32
S5_STATE = 64
S5_COLS = S5_GROUPS * S5_STATE
MLA_HEADS = 8
QK_NOPE = 64
QK_ROPE = 32
V_DIM = 64
Q_LORA = 384
KV_LORA = 256
ROPE_BASE = 10000.0
NORM_EPS = 1e-6
PAGE_SIZE = 128
ATTN_SCALE = (QK_NOPE + QK_ROPE) ** -0.5
Q_PRESCALE = ATTN_SCALE * 1.4426950408889634
LANES = 128
MXU_TILE = 256
S5_STATE_TILES = S5_COLS // MXU_TILE
S5_GROUPS_PER_STATE_TILE = MXU_TILE // S5_STATE
S5_TILES_PER_SLAB = LANES // (S5_GROUPS_PER_STATE_TILE * S5_GROUP)
S5_OUT_TILES = S5_WIDTH // MXU_TILE
S5_GROUPS_PER_OUT_TILE = MXU_TILE // S5_GROUP
ROPE_LANES = MLA_HEADS * QK_ROPE
LAT_LANES = MLA_HEADS * KV_LORA
IN_EXT = S5_WIDTH + Q_LORA + KV_LORA + 2 * ROPE_LANES
OFF_CQ = S5_WIDTH
OFF_CKV = OFF_CQ + Q_LORA
OFF_KR = OFF_CKV + KV_LORA
OFF_KRS = OFF_KR + ROPE_LANES
NEG_BIG = -1e30

V7X_VMEM_LIMIT_BYTES = 56 * 1024 * 1024
TOKEN_TILE = 512
S5_TIME_TILE = 64
S5_TIME_SPLIT = 2
S5_COL_TILE = 512
ATTN_TILE = 512
ATTN_ROW_SPLIT = 8
ATTN_QUERY_RUNS = 2
ATTN_LOOKAHEAD = 2
SAMPLE_LOOKAHEAD = 3
SAMPLE_PAGES_PER_STEP = 64
SAMPLE_SUBCHUNKS = 8
SAMPLE_SLOTS = 3


def _rms(x, g):
    return x * lax.rsqrt(jnp.mean(x * x, axis=-1, keepdims=True) + NORM_EPS) * g


def _dot(a, b):
    return jnp.dot(a, b, preferred_element_type=F32)


def _dot_nt(a, b):
    return lax.dot_general(a, b, (((1,), (1,)), ((), ())), preferred_element_type=F32)


def _ffn_half_step(x, g_ref, wg_ref, wu_ref, wd_ref):
    xn = _rms(x, g_ref[...]).astype(BF16)
    gate = _dot(xn, wg_ref[...])
    up = _dot(xn, wu_ref[...])
    hid = (jax.nn.silu(gate) * up).astype(BF16)
    return x + 0.5 * _dot(hid, wd_ref[...])


def _resident(shape):
    nd = len(shape)
    return pl.BlockSpec(shape, lambda *_: (0,) * nd, pipeline_mode=pl.Buffered(1))


def _pre_mixer_kernel(x_ref, cos_ref, sin_ref, g1_ref, wg_ref, wu_ref, wd_ref, gm_ref, win_ref,
                      gq_ref, gkv_ref, wuq_ref, wuk_ref,
                      h1_ref, u_ref, ckv_ref, krope_ref, kcat_ref, qlat_ref, qrope_ref):
    h1 = _ffn_half_step(x_ref[...], g1_ref, wg_ref, wu_ref, wd_ref)
    h1_ref[...] = h1
    xm = _rms(h1, gm_ref[...]).astype(BF16)
    proj = _dot(xm, win_ref[...])
    u_ref[...] = proj[:, :S5_WIDTH]
    cqn = _rms(proj[:, OFF_CQ:OFF_CKV], gq_ref[...]).astype(BF16)
    ckv = _rms(proj[:, OFF_CKV:OFF_KR], gkv_ref[...])
    ckv_ref[...] = ckv
    cos = cos_ref[...]
    sin = sin_ref[...]
    kr8 = proj[:, OFF_KR:OFF_KRS] * cos + proj[:, OFF_KRS:IN_EXT] * sin
    krope_ref[...] = kr8[:, :QK_ROPE]
    kcat_ref[:, :KV_LORA] = ckv.astype(BF16)
    kcat_ref[:, KV_LORA:] = kr8.astype(BF16)
    q = _dot(cqn, wuq_ref[...])
    n_nope = MLA_HEADS * QK_NOPE
    qrope_ref[...] = ((q[:, n_nope:n_nope + ROPE_LANES] * cos
                       + q[:, n_nope + ROPE_LANES:] * sin) * Q_PRESCALE).astype(BF16)
    qn = q[:, :n_nope].astype(BF16)
    for h in range(MLA_HEADS):
        pair = h // 2
        qlat_ref[:, KV_LORA * h:KV_LORA * (h + 1)] = (_dot(
            qn[:, LANES * pair:LANES * (pair + 1)], wuk_ref[h]) * Q_PRESCALE).astype(BF16)


def _pre_mixer(x, cos_tab, sin_tab, w, *, seq_major_u):
    m = x.shape[0]
    tm = TOKEN_TILE
    nsteps = m // tm
    tab_blocks = cos_tab.shape[0] // tm
    row = lambda i: (i, 0)
    tab = lambda i: (i % tab_blocks, 0)
    if seq_major_u:
        blocks_per_seq = seq_major_u // tm
        u_shape = (seq_major_u, (m // seq_major_u) * S5_WIDTH)
        u_spec = pl.BlockSpec((tm, S5_WIDTH), lambda i: (i % blocks_per_seq, i // blocks_per_seq))
    else:
        u_shape = (m, S5_WIDTH)
        u_spec = pl.BlockSpec((tm, S5_WIDTH), row)
    out_shape = (
        jax.ShapeDtypeStruct((m, D_MODEL), F32),
        jax.ShapeDtypeStruct(u_shape, F32),
        jax.ShapeDtypeStruct((m, KV_LORA), F32),
        jax.ShapeDtypeStruct((m, QK_ROPE), F32),
        jax.ShapeDtypeStruct((m, KV_LORA + ROPE_LANES), BF16),
        jax.ShapeDtypeStruct((m, LAT_LANES), BF16),
        jax.ShapeDtypeStruct((m, ROPE_LANES), BF16),
    )
    out_specs = (
        pl.BlockSpec((tm, D_MODEL), row),
        u_spec,
        pl.BlockSpec((tm, KV_LORA), row),
        pl.BlockSpec((tm, QK_ROPE), row),
        pl.BlockSpec((tm, KV_LORA + ROPE_LANES), row),
        pl.BlockSpec((tm, LAT_LANES), row),
        pl.BlockSpec((tm, ROPE_LANES), row),
    )
    weights = (w['g_ffn1'], w['wg1'], w['wu1'], w['wd1'], w['g_mix'], w['w_in_ext'],
               w['g_q'], w['g_kv'], w['w_uq_ext'], w['w_uk_pad'])
    in_specs = [pl.BlockSpec((tm, D_MODEL), row),
                pl.BlockSpec((tm, ROPE_LANES), tab),
                pl.BlockSpec((tm, ROPE_LANES), tab)] + [_resident(a.shape) for a in weights]
    return pl.pallas_call(
        _pre_mixer_kernel, out_shape=out_shape, grid=(nsteps,), in_specs=in_specs, out_specs=out_specs,
        compiler_params=pltpu.CompilerParams(dimension_semantics=("parallel",),
                                             vmem_limit_bytes=V7X_VMEM_LIMIT_BYTES),
        name="pre_mixer",
    )(x, cos_tab, sin_tab, *weights)


def _s5_readout(h_ref, rows, u, cpack_ref, d_ref, wglu_ref, bglu_ref, waout_ref):
    k = S5_COLS // S5_OUT_TILES
    tiles = []
    for n in range(S5_OUT_TILES):
        h_re = h_ref[rows, k * n:k * (n + 1)].astype(BF16)
        h_im = h_ref[rows, S5_COLS + k * n:S5_COLS + k * (n + 1)].astype(BF16)
        tiles.append(_dot(h_re, cpack_ref[n, :k]) + _dot(h_im, cpack_ref[n, k:]))
    y = jnp.concatenate(tiles, axis=1) + d_ref[...] * u
    z = jax.nn.gelu(y)
    gate = _dot(z.astype(BF16), wglu_ref[...]) + bglu_ref[...]
    return _dot((z * jax.nn.sigmoid(gate)).astype(BF16), waout_ref[...])


def _s5_drive(u, bpack_ref, xs_ref, rows):
    u_bf = u.astype(BF16)
    for j in range(2 * S5_STATE_TILES):
        slab = (j % S5_STATE_TILES) // S5_TILES_PER_SLAB
        xs_ref[rows, MXU_TILE * j:MXU_TILE * (j + 1)] = _dot(u_bf[:, LANES * slab:LANES * (slab + 1)],
                                                            bpack_ref[j])


def _s5_prompt_kernel(u_ref, bpack_ref, are_ref, aim_ref, cpack_ref, d_ref, wglu_ref, bglu_ref, waout_ref,
                      ya_ref, sre_ref, sim_ref, xs_ref, hre_ref, him_ref):
    tt, nb, _ = u_ref.shape
    tp = tt // S5_TIME_SPLIT
    rp = tp * nb

    @pl.when(pl.program_id(0) == 0)
    def _():
        hre_ref[...] = jnp.zeros_like(hre_ref)
        him_ref[...] = jnp.zeros_like(him_ref)

    u = u_ref[...].reshape(tt * nb, S5_WIDTH)
    for q in range(S5_TIME_SPLIT):
        _s5_drive(u[q * rp:(q + 1) * rp], bpack_ref, xs_ref, pl.ds(q * rp, rp))
    for q in range(S5_TIME_SPLIT):
        for cb in range(S5_COLS // S5_COL_TILE):
            re_cols = pl.ds(cb * S5_COL_TILE, S5_COL_TILE)
            im_cols = pl.ds(S5_COLS + cb * S5_COL_TILE, S5_COL_TILE)
            ar = jnp.broadcast_to(are_ref[:, re_cols], (nb, S5_COL_TILE))
            ai = jnp.broadcast_to(aim_ref[:, re_cols], (nb, S5_COL_TILE))
            hr, hi = hre_ref[:, re_cols], him_ref[:, re_cols]
            for t in range(q * tp, (q + 1) * tp):
                r = pl.ds(t * nb, nb)
                hr, hi = (ar * hr - ai * hi + xs_ref[r, re_cols], ar * hi + ai * hr + xs_ref[r, im_cols])
                xs_ref[r, re_cols] = hr
                xs_ref[r, im_cols] = hi
            hre_ref[:, re_cols] = hr
            him_ref[:, re_cols] = hi
        ya = _s5_readout(xs_ref, pl.ds(q * rp, rp), u[q * rp:(q + 1) * rp],
                         cpack_ref, d_ref, wglu_ref, bglu_ref, waout_ref)
        ya_ref[pl.ds(q * tp, tp)] = ya.reshape(tp, nb, D_MODEL)
    sre_ref[...] = hre_ref[...]
    sim_ref[...] = him_ref[...]


def _s5_prompt(u3, w):
    seq, nb, _ = u3.shape
    tt = S5_TIME_TILE
    weights = (w['bpack'], w['a_re'], w['a_im'], w['cpack'], w['s5_d'], w['w_glu'], w['b_glu'], w['w_a_out'])
    state = pl.BlockSpec((nb, S5_COLS), lambda i: (0, 0))
    return pl.pallas_call(
        _s5_prompt_kernel,
        out_shape=(jax.ShapeDtypeStruct((seq, nb, D_MODEL), F32),
                   jax.ShapeDtypeStruct((nb, S5_COLS), F32),
                   jax.ShapeDtypeStruct((nb, S5_COLS), F32)),
        grid=(seq // tt,),
        in_specs=[pl.BlockSpec((tt, nb, S5_WIDTH), lambda i: (i, 0, 0))] + [_resident(a.shape) for a in weights],
        out_specs=(pl.BlockSpec((tt, nb, D_MODEL), lambda i: (i, 0, 0)), state, state),
        scratch_shapes=[pltpu.VMEM((tt * nb, 2 * S5_COLS), F32),
                        pltpu.VMEM((nb, S5_COLS), F32),
                        pltpu.VMEM((nb, S5_COLS), F32)],
        compiler_params=pltpu.CompilerParams(dimension_semantics=("arbitrary",),
                                             vmem_limit_bytes=V7X_VMEM_LIMIT_BYTES),
        name="s5_prompt",
    )(u3, *weights)


def _s5_sample_kernel(u_ref, h0re_ref, h0im_ref, bpack_ref, are_ref, aim_ref, cpack_ref, d_ref, wglu_ref,
                      bglu_ref, waout_ref, ya_ref, sre_ref, sim_ref, hs_ref):
    tt, nb, _ = u_ref.shape
    u = u_ref[...].reshape(tt * nb, S5_WIDTH)
    _s5_drive(u, bpack_ref, hs_ref, pl.ds(0, tt * nb))
    ar = are_ref[...]
    ai = aim_ref[...]
    hr = h0re_ref[...]
    hi = h0im_ref[...]
    for t in range(tt):
        r = pl.ds(t * nb, nb)
        hr, hi = (ar * hr - ai * hi + hs_ref[r, :S5_COLS], ar * hi + ai * hr + hs_ref[r, S5_COLS:])
        hs_ref[r, :S5_COLS] = hr
        hs_ref[r, S5_COLS:] = hi
    sre_ref[...] = hr
    sim_ref[...] = hi
    ya = _s5_readout(hs_ref, pl.ds(0, tt * nb), u, cpack_ref, d_ref, wglu_ref, bglu_ref, waout_ref)
    ya_ref[...] = ya.reshape(tt, nb, D_MODEL)


def _s5_sample(u3, h0_re, h0_im, w):
    tt, nb, _ = u3.shape
    weights = (w['bpack'], w['a_re'], w['a_im'], w['cpack'], w['s5_d'], w['w_glu'], w['b_glu'], w['w_a_out'])
    args = (u3, h0_re, h0_im) + weights
    whole = lambda a: pl.BlockSpec(a.shape, lambda i, nd=a.ndim: (0,) * nd)
    outs = (jax.ShapeDtypeStruct((tt, nb, D_MODEL), F32),
            jax.ShapeDtypeStruct((nb, S5_COLS), F32),
            jax.ShapeDtypeStruct((nb, S5_COLS), F32))
    return pl.pallas_call(
        _s5_sample_kernel, out_shape=outs, grid=(1,),
        in_specs=[whole(a) for a in args], out_specs=tuple(whole(o) for o in outs),
        scratch_shapes=[pltpu.VMEM((tt * nb, 2 * S5_COLS), F32)],
        compiler_params=pltpu.CompilerParams(dimension_semantics=("arbitrary",),
                                             vmem_limit_bytes=V7X_VMEM_LIMIT_BYTES),
        name="s5_sample",
    )(*args)


def _lanes(stat, width):
    if width % LANES:
        return stat[:, :width]
    return jnp.tile(stat, (1, width // LANES))


def _softmax_block_update(s, v, rows, m_ref, l_ref, acc_ref):
    m_old = m_ref[rows, :]
    m_new = jnp.maximum(m_old, jnp.max(s, axis=-1, keepdims=True))
    alpha = jnp.exp2(m_old - m_new)
    p = jnp.exp2(s - _lanes(m_new, s.shape[1]))
    l_ref[rows, :] = alpha * l_ref[rows, :] + jnp.sum(p, axis=-1, keepdims=True)
    acc_ref[rows, :] = _lanes(alpha, KV_LORA) * acc_ref[rows, :] + _dot(p.astype(BF16), v)
    m_ref[rows, :] = m_new


def _attn_prompt_kernel(qlat_ref, qrope_ref, kcat_ref, o_ref, qs_ref, m_ref, l_ref, acc_ref):
    tq = ATTN_TILE
    sub = tq // ATTN_QUERY_RUNS
    qi = pl.program_id(1)
    lane_head = lax.broadcasted_iota(jnp.int32, (sub, ROPE_LANES), 1) // QK_ROPE
    for run in range(ATTN_QUERY_RUNS):
        toks = pl.ds(run * sub, sub)
        qr = qrope_ref[toks, :]
        for h in range(MLA_HEADS):
            rows = pl.ds((run * MLA_HEADS + h) * sub, sub)
            qs_ref[rows, :KV_LORA] = qlat_ref[toks, KV_LORA * h:KV_LORA * (h + 1)]
            qs_ref[rows, KV_LORA:] = jnp.where(lane_head == h, qr, jnp.zeros_like(qr))
    m_ref[...] = jnp.full_like(m_ref, NEG_BIG)
    l_ref[...] = jnp.zeros_like(l_ref)
    acc_ref[...] = jnp.zeros_like(acc_ref)
    n = MLA_HEADS * tq // ATTN_ROW_SPLIT
    groups_per_run = ATTN_ROW_SPLIT // ATTN_QUERY_RUNS

    def block(k, causal):
        kc = kcat_ref[pl.ds(pl.multiple_of(k * tq, tq), tq), :]

        def visible(part):
            run = part // groups_per_run
            return kc[:(run + 1) * sub] if causal else kc

        def scores(part):
            s = _dot_nt(qs_ref[pl.ds(part * n, n), :], visible(part))
            if causal:
                run = part // groups_per_run
                r = (lax.broadcasted_iota(jnp.int32, s.shape, 0) & (sub - 1)) + run * sub
                c = lax.broadcasted_iota(jnp.int32, s.shape, 1)
                s = jnp.where(c <= r, s, NEG_BIG)
            return s

        ready = [scores(part) for part in range(min(ATTN_LOOKAHEAD, ATTN_ROW_SPLIT))]
        for part in range(ATTN_ROW_SPLIT):
            if part + ATTN_LOOKAHEAD < ATTN_ROW_SPLIT:
                ready.append(scores(part + ATTN_LOOKAHEAD))
            _softmax_block_update(ready.pop(0), visible(part)[:, :KV_LORA], pl.ds(part * n, n),
                                  m_ref, l_ref, acc_ref)

    def past(k, carry):
        block(k, False)
        return carry

    lax.fori_loop(0, qi, past, 0)
    block(qi, True)
    o = acc_ref[...] / _lanes(l_ref[...], KV_LORA)
    for run in range(ATTN_QUERY_RUNS):
        for h in range(MLA_HEADS):
            seg = run * MLA_HEADS + h
            o_ref[pl.ds(run * sub, sub), KV_LORA * h:KV_LORA * (h + 1)] = o[seg * sub:(seg + 1) * sub].astype(BF16)


def _attn_prompt(qlat, qrope, kcat, *, batch, seq):
    tq = ATTN_TILE
    nq = seq // tq
    rows = MLA_HEADS * tq
    qmap = lambda b, i: (b * nq + i, 0)
    return pl.pallas_call(
        _attn_prompt_kernel,
        out_shape=jax.ShapeDtypeStruct((batch * seq, LAT_LANES), BF16),
        grid=(batch, nq),
        in_specs=[pl.BlockSpec((tq, LAT_LANES), qmap),
                  pl.BlockSpec((tq, ROPE_LANES), qmap),
                  pl.BlockSpec((seq, KV_LORA + ROPE_LANES), lambda b, i: (b, 0))],
        out_specs=pl.BlockSpec((tq, LAT_LANES), qmap),
        scratch_shapes=[pltpu.VMEM((rows, KV_LORA + ROPE_LANES), BF16),
                        pltpu.VMEM((rows, LANES), F32),
                        pltpu.VMEM((rows, LANES), F32),
                        pltpu.VMEM((rows, KV_LORA), F32)],
        compiler_params=pltpu.CompilerParams(dimension_semantics=("parallel", "arbitrary"),
                                             vmem_limit_bytes=V7X_VMEM_LIMIT_BYTES),
        name="attn_prompt",
    )(qlat, qrope, kcat)


def _softmax_part(s, v):
    m = jnp.max(s, axis=-1, keepdims=True)
    p = jnp.exp2(s - m)
    return m, jnp.sum(p, axis=-1, keepdims=True), _dot(p.astype(BF16), v)


def _merge_softmax_parts(parts, m_ref, l_ref, acc_ref):
    m_old = m_ref[...]
    m_new = m_old
    for m, _, _ in parts:
        m_new = jnp.maximum(m_new, m)
    alpha = jnp.exp2(m_old - m_new)
    l = alpha * l_ref[...]
    acc = _lanes(alpha, KV_LORA) * acc_ref[...]
    for m, psum, pv in parts:
        wgt = jnp.exp2(m - m_new)
        l = l + wgt * psum
        acc = acc + _lanes(wgt, KV_LORA) * pv
    m_ref[...] = m_new
    l_ref[...] = l
    acc_ref[...] = acc


def _attn_sample_kernel(pt_ref, qlat_ref, qrope_ref, ckvn_ref, kropen_ref, cl_hbm, cr_hbm, o_ref,
                        kbuf, rbuf, sem, qs_ref, qr_ref, kn_ref, rn_ref, m_ref, l_ref, acc_ref,
                        *, n_pages, n_new):
    pg = SAMPLE_PAGES_PER_STEP
    chunks = n_pages // pg
    g = pl.program_id(0)
    n_steps = pl.num_programs(0)
    c = g % chunks
    last = n_steps - 1
    slot = lax.rem(g, SAMPLE_SLOTS)
    ahead = SAMPLE_SLOTS - 1

    def page_copies(step, dst_slot):
        first = (step // chunks) * n_pages + (step % chunks) * pg
        copies = []
        for p in range(pg):
            page = pt_ref[first + p]
            keys = pl.ds(p * PAGE_SIZE, PAGE_SIZE)
            copies.append(pltpu.make_async_copy(cl_hbm.at[page], kbuf.at[dst_slot, keys], sem.at[0, dst_slot]))
            copies.append(pltpu.make_async_copy(cr_hbm.at[page], rbuf.at[dst_slot, :, keys], sem.at[1, dst_slot]))
        return copies

    @pl.when(g == 0)
    def _():
        for step in range(ahead):
            for cp in page_copies(step, step):
                cp.start()

    @pl.when(c == 0)
    def _():
        ql = qlat_ref[0].astype(F32)
        qr = qrope_ref[0].astype(F32)
        for h in range(MLA_HEADS):
            rows = pl.ds(h * n_new, n_new)
            qs_ref[rows, :] = ql[:, KV_LORA * h:KV_LORA * (h + 1)]
            qr_ref[rows, :] = qr[:, QK_ROPE * h:QK_ROPE * (h + 1)]
        m_ref[...] = jnp.full_like(m_ref, NEG_BIG)
        l_ref[...] = jnp.zeros_like(l_ref)
        acc_ref[...] = jnp.zeros_like(acc_ref)

    for cp in page_copies(g, slot):
        cp.wait()

    qs = qs_ref[...].astype(BF16)
    qr = qr_ref[...].astype(BF16)
    sub = pg * PAGE_SIZE // SAMPLE_SUBCHUNKS
    def scores(j):
        keys = pl.ds(j * sub, sub)
        kc = kbuf[slot, keys, :].astype(BF16)
        return _dot_nt(qs, kc) + _dot(qr, rbuf[slot, :, keys].astype(BF16)), kc

    parts = []
    ready = [scores(j) for j in range(min(SAMPLE_LOOKAHEAD, SAMPLE_SUBCHUNKS))]
    for j in range(SAMPLE_SUBCHUNKS):
        if j + SAMPLE_LOOKAHEAD < SAMPLE_SUBCHUNKS:
            ready.append(scores(j + SAMPLE_LOOKAHEAD))
        parts.append(_softmax_part(*ready.pop(0)))
    _merge_softmax_parts(parts, m_ref, l_ref, acc_ref)

    for cp in page_copies(jnp.minimum(g + ahead, last), lax.rem(g + ahead, SAMPLE_SLOTS)):
        cp.start()

    @pl.when(g == last)
    def _():
        for extra in range(1, ahead + 1):
            for cp in page_copies(last, lax.rem(g + extra, SAMPLE_SLOTS)):
                cp.wait()

    @pl.when(c == chunks - 1)
    def _():
        kn_ref[...] = jnp.zeros_like(kn_ref)
        rn_ref[...] = jnp.zeros_like(rn_ref)
        kn_ref[pl.ds(0, n_new), :] = ckvn_ref[0]
        rn_ref[pl.ds(0, n_new), :] = kropen_ref[0]
        kn = kn_ref[...].astype(BF16)
        sn = _dot_nt(qs, kn) + _dot_nt(qr, rn_ref[...].astype(BF16))
        t = lax.broadcasted_iota(jnp.int32, sn.shape, 0) % n_new
        j = lax.broadcasted_iota(jnp.int32, sn.shape, 1)
        sn = jnp.where(j <= t, sn, NEG_BIG)
        _merge_softmax_parts([_softmax_part(sn, kn)], m_ref, l_ref, acc_ref)
        o_ref[0] = acc_ref[...] / _lanes(l_ref[...], KV_LORA)


def _attn_sample(page_table, qlat, qrope, ckv_new, krope_new, cache_latent, cache_k_rope):
    nb, n_pages = page_table.shape
    n_new = qlat.shape[0] // nb
    pg = SAMPLE_PAGES_PER_STEP
    chunks = n_pages // pg
    rows = MLA_HEADS * n_new
    new_pad = 8
    per_batch = lambda g, pt: (g // chunks, 0, 0)
    grid_spec = pltpu.PrefetchScalarGridSpec(
        num_scalar_prefetch=1,
        grid=(nb * chunks,),
        in_specs=[pl.BlockSpec((1, n_new, LAT_LANES), per_batch),
                  pl.BlockSpec((1, n_new, ROPE_LANES), per_batch),
                  pl.BlockSpec((1, n_new, KV_LORA), per_batch),
                  pl.BlockSpec((1, n_new, QK_ROPE), per_batch),
                  pl.BlockSpec(memory_space=pl.ANY),
                  pl.BlockSpec(memory_space=pl.ANY)],
        out_specs=pl.BlockSpec((1, rows, KV_LORA), per_batch),
        scratch_shapes=[pltpu.VMEM((SAMPLE_SLOTS, pg * PAGE_SIZE, KV_LORA), F32),
                        pltpu.VMEM((SAMPLE_SLOTS, QK_ROPE, pg * PAGE_SIZE), F32),
                        pltpu.SemaphoreType.DMA((2, SAMPLE_SLOTS)),
                        pltpu.VMEM((rows, KV_LORA), F32),
                        pltpu.VMEM((rows, QK_ROPE), F32),
                        pltpu.VMEM((new_pad, KV_LORA), F32),
                        pltpu.VMEM((new_pad, QK_ROPE), F32),
                        pltpu.VMEM((rows, LANES), F32),
                        pltpu.VMEM((rows, LANES), F32),
                        pltpu.VMEM((rows, KV_LORA), F32)])
    return pl.pallas_call(
        functools.partial(_attn_sample_kernel, n_pages=n_pages, n_new=n_new),
        out_shape=jax.ShapeDtypeStruct((nb, rows, KV_LORA), F32),
        grid_spec=grid_spec,
        compiler_params=pltpu.CompilerParams(dimension_semantics=("arbitrary",),
                                             vmem_limit_bytes=V7X_VMEM_LIMIT_BYTES),
        name="attn_sample",
    )(page_table.reshape(-1), qlat.reshape(nb, n_new, LAT_LANES), qrope.reshape(nb, n_new, ROPE_LANES),
      ckv_new.reshape(nb, n_new, KV_LORA), krope_new.reshape(nb, n_new, QK_ROPE), cache_latent, cache_k_rope)


def _post_mixer_kernel(h1_ref, ya_ref, ol_ref, gm_ref, wgt_ref, wuv_ref, wbo_ref, wout_ref,
                       g2_ref, wg_ref, wu_ref, wd_ref, gf_ref, y_ref):
    h1 = h1_ref[...]
    xm = _rms(h1, gm_ref[...]).astype(BF16)
    gates = _dot(xm, wgt_ref[...])
    ol = ol_ref[...]
    o = _dot(ol[:, :KV_LORA], wuv_ref[0])
    for h in range(1, MLA_HEADS):
        o = o + _dot(ol[:, KV_LORA * h:KV_LORA * (h + 1)], wuv_ref[h])
    yb = _dot(o.astype(BF16), wbo_ref[...])
    merged = jax.nn.sigmoid(gates[:, :D_MODEL]) * ya_ref[...] + jax.nn.sigmoid(gates[:, D_MODEL:]) * yb
    h2 = h1 + _dot(merged.astype(BF16), wout_ref[...])
    h3 = _ffn_half_step(h2, g2_ref, wg_ref, wu_ref, wd_ref)
    y_ref[...] = _rms(h3, gf_ref[...])


def _post_mixer(h1, ya, olat, w, *, seq_major_ya):
    m = h1.shape[0]
    tm = TOKEN_TILE
    row = lambda i: (i, 0)
    if seq_major_ya:
        blocks_per_seq = seq_major_ya // tm
        ya_spec = pl.BlockSpec((tm, D_MODEL), lambda i: (i % blocks_per_seq, i // blocks_per_seq))
    else:
        ya_spec = pl.BlockSpec((tm, D_MODEL), row)
    weights = (w['g_mix'], w['w_gates'], w['w_uv_pad'], w['w_b_out'], w['w_out'],
               w['g_ffn2'], w['wg2'], w['wu2'], w['wd2'], w['g_final'])
    return pl.pallas_call(
        _post_mixer_kernel,
        out_shape=jax.ShapeDtypeStruct((m, D_MODEL), F32),
        grid=(m // tm,),
        in_specs=[pl.BlockSpec((tm, D_MODEL), row), ya_spec, pl.BlockSpec((tm, LAT_LANES), row)]
        + [_resident(a.shape) for a in weights],
        out_specs=pl.BlockSpec((tm, D_MODEL), row),
        compiler_params=pltpu.CompilerParams(dimension_semantics=("parallel",),
                                             vmem_limit_bytes=V7X_VMEM_LIMIT_BYTES),
        name="post_mixer",
    )(h1, ya, olat, *weights)


def _rope_tables(pos):
    inv_freq = 1.0 / (ROPE_BASE ** (jnp.arange(0, QK_ROPE, 2, dtype=F32) / QK_ROPE))
    ang = pos.astype(F32)[:, None] * inv_freq[None, :]
    cos, sin = jnp.cos(ang), jnp.sin(ang)
    cos_tab = jnp.tile(jnp.concatenate([cos, cos], axis=-1), (1, MLA_HEADS))
    sin_tab = jnp.tile(jnp.concatenate([-sin, sin], axis=-1), (1, MLA_HEADS))
    return cos_tab, sin_tab


def _swap_halves(a):
    half = QK_ROPE // 2
    return jnp.concatenate([a[..., half:], a[..., :half]], axis=-1)


def _block_diag_runs(blocks, run):
    g, r, c = blocks.shape
    eye = jnp.eye(run, dtype=blocks.dtype)
    tiles = jnp.einsum('tgrc,gk->tgrkc', blocks.reshape(g // run, run, r, c), eye)
    return tiles.reshape(g // run, run * r, run * c)


def _prepare_weights(p):
    w = {}
    vec = lambda a: a.reshape(1, -1).astype(F32)
    for name in ('g_ffn1', 'g_mix', 'g_q', 'g_kv', 'g_ffn2', 'g_final', 'b_glu'):
        w[name] = vec(p[name])
    for src, dst in (('w_ffn1_gate', 'wg1'), ('w_ffn1_up', 'wu1'), ('w_ffn1_down', 'wd1'),
                     ('w_ffn2_gate', 'wg2'), ('w_ffn2_up', 'wu2'), ('w_ffn2_down', 'wd2'),
                     ('w_glu', 'w_glu'), ('w_a_out', 'w_a_out'), ('w_b_out', 'w_b_out'), ('w_out', 'w_out')):
        w[dst] = p[src].astype(BF16)
    w_in = p['w_in']
    off_kr = S5_WIDTH + Q_LORA + KV_LORA
    k_r = w_in[:, off_kr:off_kr + QK_ROPE]
    w['w_in_ext'] = jnp.concatenate(
        [w_in[:, :off_kr], jnp.tile(k_r, (1, MLA_HEADS)), jnp.tile(_swap_halves(k_r), (1, MLA_HEADS))],
        axis=1).astype(BF16)
    w['w_gates'] = w_in[:, off_kr + QK_ROPE:].astype(BF16)
    w_uq = p['w_uq']
    uq_rope = w_uq[:, :, QK_NOPE:]
    w['w_uq_ext'] = jnp.concatenate(
        [w_uq[:, :, :QK_NOPE].reshape(Q_LORA, -1), uq_rope.reshape(Q_LORA, -1),
         _swap_halves(uq_rope).reshape(Q_LORA, -1)], axis=1).astype(BF16)
    uk = jnp.transpose(p['w_uk'], (1, 2, 0))
    uk_pad = jnp.zeros((MLA_HEADS, 2, QK_NOPE, KV_LORA), F32)
    uk_pad = uk_pad.at[jnp.arange(MLA_HEADS), jnp.arange(MLA_HEADS) % 2].set(uk)
    w['w_uk_pad'] = uk_pad.reshape(MLA_HEADS, 2 * QK_NOPE, KV_LORA).astype(BF16)
    uv = jnp.transpose(p['w_uv'], (1, 0, 2))
    uv_pad = jnp.zeros((MLA_HEADS, KV_LORA, MLA_HEADS, V_DIM), F32)
    uv_pad = uv_pad.at[jnp.arange(MLA_HEADS), :, jnp.arange(MLA_HEADS)].set(uv)
    w['w_uv_pad'] = uv_pad.reshape(MLA_HEADS, KV_LORA, MLA_HEADS * V_DIM).astype(BF16)
    lam_re, lam_im = p['s5_a_re'].astype(F32), p['s5_a_im'].astype(F32)
    dt = jnp.exp(p['s5_log_dt'].astype(F32))[:, None]
    mag = jnp.exp(lam_re * dt)
    ab_re, ab_im = mag * jnp.cos(lam_im * dt), mag * jnp.sin(lam_im * dt)
    den = lam_re * lam_re + lam_im * lam_im
    num_re, num_im = ab_re - 1.0, ab_im
    k_re = (num_re * lam_re + num_im * lam_im) / den
    k_im = (num_im * lam_re - num_re * lam_im) / den
    w['a_re'] = ab_re.reshape(1, S5_COLS)
    w['a_im'] = ab_im.reshape(1, S5_COLS)
    b_re, b_im = p['s5_b_re'].astype(F32), p['s5_b_im'].astype(F32)
    kb_re = k_re[..., None] * b_re - k_im[..., None] * b_im
    kb_im = k_re[..., None] * b_im + k_im[..., None] * b_re

    def to_in(a):
        tiles = _block_diag_runs(jnp.transpose(a, (0, 2, 1)), S5_GROUPS_PER_STATE_TILE)
        n, r, c = tiles.shape
        pos = jnp.arange(n) % S5_TILES_PER_SLAB
        slab = jnp.zeros((n, S5_TILES_PER_SLAB, r, c), F32).at[jnp.arange(n), pos].set(tiles)
        return slab.reshape(n, S5_TILES_PER_SLAB * r, c)

    w['bpack'] = jnp.concatenate([to_in(kb_re), to_in(kb_im)], axis=0).astype(BF16)
    to_out = lambda a: _block_diag_runs(jnp.transpose(a, (0, 2, 1)), S5_GROUPS_PER_OUT_TILE)
    w['cpack'] = jnp.concatenate([to_out(p['s5_c_re'].astype(F32)), -to_out(p['s5_c_im'].astype(F32))],
                                 axis=1).astype(BF16)
    w['s5_d'] = vec(p['s5_d'])
    return w


def kernel(x_prompt, x_sample, cache_latent, cache_k_rope, state_ssm_re, state_ssm_im, page_table, g_ffn1, w_ffn1_gate, w_ffn1_up, w_ffn1_down, g_mix, w_in, s5_a_re, s5_a_im, s5_log_dt, s5_b_re, s5_b_im, s5_c_re, s5_c_im, s5_d, w_glu, b_glu, w_a_out, g_q, w_uq, g_kv, w_uk, w_uv, w_b_out, w_out, g_ffn2, w_ffn2_gate, w_ffn2_up, w_ffn2_down, g_final):
    layer = 0
    params = dict(
        g_ffn1=g_ffn1, w_ffn1_gate=w_ffn1_gate, w_ffn1_up=w_ffn1_up, w_ffn1_down=w_ffn1_down,
        g_mix=g_mix, w_in=w_in, s5_a_re=s5_a_re, s5_a_im=s5_a_im, s5_log_dt=s5_log_dt,
        s5_b_re=s5_b_re, s5_b_im=s5_b_im, s5_c_re=s5_c_re, s5_c_im=s5_c_im, s5_d=s5_d,
        w_glu=w_glu, b_glu=b_glu, w_a_out=w_a_out, g_q=g_q, w_uq=w_uq, g_kv=g_kv, w_uk=w_uk,
        w_uv=w_uv, w_b_out=w_b_out, w_out=w_out, g_ffn2=g_ffn2, w_ffn2_gate=w_ffn2_gate,
        w_ffn2_up=w_ffn2_up, w_ffn2_down=w_ffn2_down)
    p = {k: v[layer] for k, v in params.items()}
    p['g_final'] = g_final
    w = _prepare_weights(p)

    nb, seq, _ = x_prompt.shape
    cos_p, sin_p = _rope_tables(jnp.arange(seq))
    h1, u, ckv, krope, kcat, qlat, qrope = _pre_mixer(
        x_prompt.reshape(nb * seq, D_MODEL), cos_p, sin_p, w, seq_major_u=seq)
    olat = _attn_prompt(qlat, qrope, kcat, batch=nb, seq=seq)
    ya, sre_p, sim_p = _s5_prompt(u.reshape(seq, nb, S5_WIDTH), w)
    y_prompt = _post_mixer(h1, ya.reshape(seq, nb * D_MODEL), olat, w, seq_major_ya=seq).reshape(nb, seq, D_MODEL)
    ckv_prompt = ckv.reshape(1, nb, seq, KV_LORA)
    krope_prompt = krope.reshape(1, nb, seq, QK_ROPE)
    group = lambda a: a.reshape(1, a.shape[0], S5_GROUPS, S5_STATE)

    sb, n_new, _ = x_sample.shape
    n_pages = page_table.shape[1]
    past_len = n_pages * PAGE_SIZE
    cos_s, sin_s = _rope_tables(past_len + jnp.arange(n_new))
    cos_s, sin_s = jnp.tile(cos_s, (sb, 1)), jnp.tile(sin_s, (sb, 1))
    h1s, us, ckvs, kropes, _, qlats, qropes = _pre_mixer(
        x_sample.reshape(sb * n_new, D_MODEL), cos_s, sin_s, w, seq_major_u=0)
    us_t = jnp.transpose(us.reshape(sb, n_new, S5_WIDTH), (1, 0, 2))
    yas_t, sre_s, sim_s = _s5_sample(us_t, state_ssm_re[layer].reshape(sb, S5_COLS),
                                     state_ssm_im[layer].reshape(sb, S5_COLS), w)
    yas = jnp.transpose(yas_t, (1, 0, 2)).reshape(sb * n_new, D_MODEL)
    rope_pages = jnp.swapaxes(cache_k_rope[layer], 1, 2)
    ol = _attn_sample(page_table, qlats, qropes, ckvs, kropes, cache_latent[layer], rope_pages)
    olats = jnp.transpose(ol.reshape(sb, MLA_HEADS, n_new, KV_LORA), (0, 2, 1, 3))
    olats = olats.reshape(sb * n_new, LAT_LANES).astype(BF16)
    y_sample = _post_mixer(h1s, yas, olats, w, seq_major_ya=0).reshape(sb, n_new, D_MODEL)

    return (y_prompt, y_sample, ckv_prompt, krope_prompt,
            ckvs.reshape(1, sb, n_new, KV_LORA), kropes.reshape(1, sb, n_new, QK_ROPE),
            group(sre_p), group(sim_p), group(sre_s), group(sim_s))
```

```python
import functools
import math

import jax
import jax.numpy as jnp
from jax import lax
from jax.experimental import pallas as pl
from jax.experimental.pallas import tpu as pltpu

F32 = jnp.float32
BF16 = jnp.bfloat16

D_MODEL = 1024
D_FF = 2816
S5_WIDTH = 512
S5_GROUP = 16
S5_GROUPS = 32
S5_STATE = 64
S5_COLS = S5_GROUPS * S5_STATE
MLA_HEADS = 8
QK_NOPE = 64
QK_ROPE = 32
V_DIM = 64
Q_LORA = 384
KV_LORA = 256
ROPE_BASE = 10000.0
NORM_EPS = 1e-6
PAGE_SIZE = 128
ATTN_SCALE = (QK_NOPE + QK_ROPE) ** -0.5
Q_PRESCALE = ATTN_SCALE * 1.4426950408889634
LANES = 128
MXU_TILE = 256
S5_STATE_TILES = S5_COLS // MXU_TILE
S5_GROUPS_PER_STATE_TILE = MXU_TILE // S5_STATE
S5_TILES_PER_SLAB = LANES // (S5_GROUPS_PER_STATE_TILE * S5_GROUP)
S5_OUT_TILES = S5_WIDTH // MXU_TILE
S5_GROUPS_PER_OUT_TILE = MXU_TILE // S5_GROUP
ROPE_LANES = MLA_HEADS * QK_ROPE
LAT_LANES = MLA_HEADS * KV_LORA
IN_EXT = S5_WIDTH + Q_LORA + KV_LORA + 2 * ROPE_LANES
OFF_CQ = S5_WIDTH
OFF_CKV = OFF_CQ + Q_LORA
OFF_KR = OFF_CKV + KV_LORA
OFF_KRS = OFF_KR + ROPE_LANES
NEG_BIG = -1e30

V7X_VMEM_LIMIT_BYTES = 56 * 1024 * 1024
TOKEN_TILE = 512
WEIGHT_FETCH_CHUNKS = 8
S5_TIME_TILE = 64
S5_TIME_SPLIT = 2
S5_COL_TILE = 512
ATTN_TILE = 512
ATTN_ROW_SPLIT = 8
ATTN_QUERY_RUNS = 2
ATTN_LOOKAHEAD = 2
SAMPLE_LOOKAHEAD = 3
SAMPLE_PAGES_PER_STEP = 64
SAMPLE_SUBCHUNKS = 8
SAMPLE_SLOTS = 3


def _rms(x, g):
    return x * lax.rsqrt(jnp.mean(x * x, axis=-1, keepdims=True) + NORM_EPS) * g


def _dot(a, b):
    return jnp.dot(a, b, preferred_element_type=F32)


def _dot_nt(a, b):
    return lax.dot_general(a, b, (((1,), (1,)), ((), ())), preferred_element_type=F32)


def _ffn_half_step(x, g_ref, wg_ref, wu_ref, wd_ref):
    xn = _rms(x, g_ref[...]).astype(BF16)
    gate = _dot(xn, wg_ref[...])
    up = _dot(xn, wu_ref[...])
    hid = (jax.nn.silu(gate) * up).astype(BF16)
    return x + 0.5 * _dot(hid, wd_ref[...])


def _resident(shape):
    nd = len(shape)
    return pl.BlockSpec(shape, lambda *_: (0,) * nd, pipeline_mode=pl.Buffered(1))


def _fetch_weights(hbm_refs, vmem_refs, sem):
    copies = []
    for src, dst in zip(hbm_refs, vmem_refs):
        chunks = math.gcd(src.shape[0], WEIGHT_FETCH_CHUNKS)
        step = src.shape[0] // chunks
        for c in range(chunks):
            rows = pl.ds(c * step, step)
            copies.append(pltpu.make_async_copy(src.at[rows], dst.at[rows], sem))
    for cp in copies:
        cp.start()
    for cp in copies:
        cp.wait()


def _weight_specs(weights):
    specs = [pl.BlockSpec(memory_space=pl.ANY) for _ in weights]
    scratch = [pltpu.VMEM(a.shape, a.dtype) for a in weights] + [pltpu.SemaphoreType.DMA(())]
    return specs, scratch


def _pre_mixer_kernel(x_ref, cos_ref, sin_ref, g1_ref, gm_ref, gq_ref, gkv_ref,
                      wg_hbm, wu_hbm, wd_hbm, win_hbm, wuq_hbm, wuk_hbm,
                      h1_ref, u_ref, ckv_ref, krope_ref, kcat_ref, qlat_ref, qrope_ref,
                      wg_ref, wu_ref, wd_ref, win_ref, wuq_ref, wuk_ref, sem):
    @pl.when(pl.program_id(0) == 0)
    def _():
        _fetch_weights((wg_hbm, wu_hbm, wd_hbm, win_hbm, wuq_hbm, wuk_hbm),
                       (wg_ref, wu_ref, wd_ref, win_ref, wuq_ref, wuk_ref), sem)

    h1 = _ffn_half_step(x_ref[...], g1_ref, wg_ref, wu_ref, wd_ref)
    h1_ref[...] = h1
    xm = _rms(h1, gm_ref[...]).astype(BF16)
    proj = _dot(xm, win_ref[...])
    u_ref[...] = proj[:, :S5_WIDTH]
    cqn = _rms(proj[:, OFF_CQ:OFF_CKV], gq_ref[...]).astype(BF16)
    ckv = _rms(proj[:, OFF_CKV:OFF_KR], gkv_ref[...])
    ckv_ref[...] = ckv
    cos = cos_ref[...]
    sin = sin_ref[...]
    kr8 = proj[:, OFF_KR:OFF_KRS] * cos + proj[:, OFF_KRS:IN_EXT] * sin
    krope_ref[...] = kr8[:, :QK_ROPE]
    kcat_ref[:, :KV_LORA] = ckv.astype(BF16)
    kcat_ref[:, KV_LORA:] = kr8.astype(BF16)
    q = _dot(cqn, wuq_ref[...])
    n_nope = MLA_HEADS * QK_NOPE
    qrope_ref[...] = ((q[:, n_nope:n_nope + ROPE_LANES] * cos
                       + q[:, n_nope + ROPE_LANES:] * sin) * Q_PRESCALE).astype(BF16)
    qn = q[:, :n_nope].astype(BF16)
    for h in range(MLA_HEADS):
        pair = h // 2
        qlat_ref[:, KV_LORA * h:KV_LORA * (h + 1)] = (_dot(
            qn[:, LANES * pair:LANES * (pair + 1)], wuk_ref[h]) * Q_PRESCALE).astype(BF16)


def _pre_mixer(x, cos_tab, sin_tab, w, *, seq_major_u):
    m = x.shape[0]
    tm = TOKEN_TILE
    nsteps = m // tm
    tab_blocks = cos_tab.shape[0] // tm
    row = lambda i: (i, 0)
    tab = lambda i: (i % tab_blocks, 0)
    if seq_major_u:
        blocks_per_seq = seq_major_u // tm
        u_shape = (seq_major_u, (m // seq_major_u) * S5_WIDTH)
        u_spec = pl.BlockSpec((tm, S5_WIDTH), lambda i: (i % blocks_per_seq, i // blocks_per_seq))
    else:
        u_shape = (m, S5_WIDTH)
        u_spec = pl.BlockSpec((tm, S5_WIDTH), row)
    out_shape = (
        jax.ShapeDtypeStruct((m, D_MODEL), F32),
        jax.ShapeDtypeStruct(u_shape, F32),
        jax.ShapeDtypeStruct((m, KV_LORA), F32),
        jax.ShapeDtypeStruct((m, QK_ROPE), F32),
        jax.ShapeDtypeStruct((m, KV_LORA + ROPE_LANES), BF16),
        jax.ShapeDtypeStruct((m, LAT_LANES), BF16),
        jax.ShapeDtypeStruct((m, ROPE_LANES), BF16),
    )
    out_specs = (
        pl.BlockSpec((tm, D_MODEL), row),
        u_spec,
        pl.BlockSpec((tm, KV_LORA), row),
        pl.BlockSpec((tm, QK_ROPE), row),
        pl.BlockSpec((tm, KV_LORA + ROPE_LANES), row),
        pl.BlockSpec((tm, LAT_LANES), row),
        pl.BlockSpec((tm, ROPE_LANES), row),
    )
    gains = (w['g_ffn1'], w['g_mix'], w['g_q'], w['g_kv'])
    weights = (w['wg1'], w['wu1'], w['wd1'], w['w_in_ext'], w['w_uq_ext'], w['w_uk_pad'])
    weight_specs, weight_scratch = _weight_specs(weights)
    in_specs = [pl.BlockSpec((tm, D_MODEL), row),
                pl.BlockSpec((tm, ROPE_LANES), tab),
                pl.BlockSpec((tm, ROPE_LANES), tab)] + [_resident(a.shape) for a in gains] + weight_specs
    return pl.pallas_call(
        _pre_mixer_kernel, out_shape=out_shape, grid=(nsteps,), in_specs=in_specs, out_specs=out_specs,
        scratch_shapes=weight_scratch,
        compiler_params=pltpu.CompilerParams(dimension_semantics=("arbitrary",),
                                             vmem_limit_bytes=V7X_VMEM_LIMIT_BYTES),
        name="pre_mixer",
    )(x, cos_tab, sin_tab, *gains, *weights)


def _s5_readout(h_ref, rows, u, cpack_ref, d_ref, wglu_ref, bglu_ref, waout_ref):
    k = S5_COLS // S5_OUT_TILES
    tiles = []
    for n in range(S5_OUT_TILES):
        h_re = h_ref[rows, k * n:k * (n + 1)].astype(BF16)
        h_im = h_ref[rows, S5_COLS + k * n:S5_COLS + k * (n + 1)].astype(BF16)
        tiles.append(_dot(h_re, cpack_ref[n, :k]) + _dot(h_im, cpack_ref[n, k:]))
    y = jnp.concatenate(tiles, axis=1) + d_ref[...] * u
    z = jax.nn.gelu(y)
    gate = _dot(z.astype(BF16), wglu_ref[...]) + bglu_ref[...]
    return _dot((z * jax.nn.sigmoid(gate)).astype(BF16), waout_ref[...])


def _s5_drive(u, bpack_ref, xs_ref, rows):
    u_bf = u.astype(BF16)
    for j in range(2 * S5_STATE_TILES):
        slab = (j % S5_STATE_TILES) // S5_TILES_PER_SLAB
        xs_ref[rows, MXU_TILE * j:MXU_TILE * (j + 1)] = _dot(u_bf[:, LANES * slab:LANES * (slab + 1)],
                                                            bpack_ref[j])


def _s5_prompt_kernel(u_ref, bpack_ref, are_ref, aim_ref, cpack_ref, d_ref, wglu_ref, bglu_ref, waout_ref,
                      ya_ref, sre_ref, sim_ref, xs_ref, hre_ref, him_ref):
    tt, nb, _ = u_ref.shape
    tp = tt // S5_TIME_SPLIT
    rp = tp * nb

    @pl.when(pl.program_id(0) == 0)
    def _():
        hre_ref[...] = jnp.zeros_like(hre_ref)
        him_ref[...] = jnp.zeros_like(him_ref)

    u = u_ref[...].reshape(tt * nb, S5_WIDTH)
    for q in range(S5_TIME_SPLIT):
        _s5_drive(u[q * rp:(q + 1) * rp], bpack_ref, xs_ref, pl.ds(q * rp, rp))
    for q in range(S5_TIME_SPLIT):
        for cb in range(S5_COLS // S5_COL_TILE):
            re_cols = pl.ds(cb * S5_COL_TILE, S5_COL_TILE)
            im_cols = pl.ds(S5_COLS + cb * S5_COL_TILE, S5_COL_TILE)
            ar = jnp.broadcast_to(are_ref[:, re_cols], (nb, S5_COL_TILE))
            ai = jnp.broadcast_to(aim_ref[:, re_cols], (nb, S5_COL_TILE))
            hr, hi = hre_ref[:, re_cols], him_ref[:, re_cols]
            for t in range(q * tp, (q + 1) * tp):
                r = pl.ds(t * nb, nb)
                hr, hi = (ar * hr - ai * hi + xs_ref[r, re_cols], ar * hi + ai * hr + xs_ref[r, im_cols])
                xs_ref[r, re_cols] = hr
                xs_ref[r, im_cols] = hi
            hre_ref[:, re_cols] = hr
            him_ref[:, re_cols] = hi
        ya = _s5_readout(xs_ref, pl.ds(q * rp, rp), u[q * rp:(q + 1) * rp],
                         cpack_ref, d_ref, wglu_ref, bglu_ref, waout_ref)
        ya_ref[pl.ds(q * tp, tp)] = ya.reshape(tp, nb, D_MODEL)
    sre_ref[...] = hre_ref[...]
    sim_ref[...] = him_ref[...]


def _s5_prompt(u3, w):
    seq, nb, _ = u3.shape
    tt = S5_TIME_TILE
    weights = (w['bpack'], w['a_re'], w['a_im'], w['cpack'], w['s5_d'], w['w_glu'], w['b_glu'], w['w_a_out'])
    state = pl.BlockSpec((nb, S5_COLS), lambda i: (0, 0))
    return pl.pallas_call(
        _s5_prompt_kernel,
        out_shape=(jax.ShapeDtypeStruct((seq, nb, D_MODEL), F32),
                   jax.ShapeDtypeStruct((nb, S5_COLS), F32),
                   jax.ShapeDtypeStruct((nb, S5_COLS), F32)),
        grid=(seq // tt,),
        in_specs=[pl.BlockSpec((tt, nb, S5_WIDTH), lambda i: (i, 0, 0))] + [_resident(a.shape) for a in weights],
        out_specs=(pl.BlockSpec((tt, nb, D_MODEL), lambda i: (i, 0, 0)), state, state),
        scratch_shapes=[pltpu.VMEM((tt * nb, 2 * S5_COLS), F32),
                        pltpu.VMEM((nb, S5_COLS), F32),
                        pltpu.VMEM((nb, S5_COLS), F32)],
        compiler_params=pltpu.CompilerParams(dimension_semantics=("arbitrary",),
                                             vmem_limit_bytes=V7X_VMEM_LIMIT_BYTES),
        name="s5_prompt",
    )(u3, *weights)


def _s5_sample_kernel(u_ref, h0re_ref, h0im_ref, bpack_ref, are_ref, aim_ref, cpack_ref, d_ref, wglu_ref,
                      bglu_ref, waout_ref, ya_ref, sre_ref, sim_ref, hs_ref):
    tt, nb, _ = u_ref.shape
    u = u_ref[...].reshape(tt * nb, S5_WIDTH)
    _s5_drive(u, bpack_ref, hs_ref, pl.ds(0, tt * nb))
    ar = are_ref[...]
    ai = aim_ref[...]
    hr = h0re_ref[...]
    hi = h0im_ref[...]
    for t in range(tt):
        r = pl.ds(t * nb, nb)
        hr, hi = (ar * hr - ai * hi + hs_ref[r, :S5_COLS], ar * hi + ai * hr + hs_ref[r, S5_COLS:])
        hs_ref[r, :S5_COLS] = hr
        hs_ref[r, S5_COLS:] = hi
    sre_ref[...] = hr
    sim_ref[...] = hi
    ya = _s5_readout(hs_ref, pl.ds(0, tt * nb), u, cpack_ref, d_ref, wglu_ref, bglu_ref, waout_ref)
    ya_ref[...] = ya.reshape(tt, nb, D_MODEL)


def _s5_sample(u3, h0_re, h0_im, w):
    tt, nb, _ = u3.shape
    weights = (w['bpack'], w['a_re'], w['a_im'], w['cpack'], w['s5_d'], w['w_glu'], w['b_glu'], w['w_a_out'])
    args = (u3, h0_re, h0_im) + weights
    whole = lambda a: pl.BlockSpec(a.shape, lambda i, nd=a.ndim: (0,) * nd)
    outs = (jax.ShapeDtypeStruct((tt, nb, D_MODEL), F32),
            jax.ShapeDtypeStruct((nb, S5_COLS), F32),
            jax.ShapeDtypeStruct((nb, S5_COLS), F32))
    return pl.pallas_call(
        _s5_sample_kernel, out_shape=outs, grid=(1,),
        in_specs=[whole(a) for a in args], out_specs=tuple(whole(o) for o in outs),
        scratch_shapes=[pltpu.VMEM((tt * nb, 2 * S5_COLS), F32)],
        compiler_params=pltpu.CompilerParams(dimension_semantics=("arbitrary",),
                                             vmem_limit_bytes=V7X_VMEM_LIMIT_BYTES),
        name="s5_sample",
    )(*args)


def _lanes(stat, width):
    if width % LANES:
        return stat[:, :width]
    return jnp.tile(stat, (1, width // LANES))


def _softmax_block_update(s, v, rows, m_ref, l_ref, acc_ref):
    m_old = m_ref[rows, :]
    m_new = jnp.maximum(m_old, jnp.max(s, axis=-1, keepdims=True))
    alpha = jnp.exp2(m_old - m_new)
    p = jnp.exp2(s - _lanes(m_new, s.shape[1]))
    l_ref[rows, :] = alpha * l_ref[rows, :] + jnp.sum(p, axis=-1, keepdims=True)
    acc_ref[rows, :] = _lanes(alpha, KV_LORA) * acc_ref[rows, :] + _dot(p.astype(BF16), v)
    m_ref[rows, :] = m_new


def _attn_prompt_kernel(qlat_ref, qrope_ref, kcat_ref, o_ref, qs_ref, m_ref, l_ref, acc_ref):
    tq = ATTN_TILE
    sub = tq // ATTN_QUERY_RUNS
    qi = pl.program_id(1)
    lane_head = lax.broadcasted_iota(jnp.int32, (sub, ROPE_LANES), 1) // QK_ROPE
    for run in range(ATTN_QUERY_RUNS):
        toks = pl.ds(run * sub, sub)
        qr = qrope_ref[toks, :]
        for h in range(MLA_HEADS):
            rows = pl.ds((run * MLA_HEADS + h) * sub, sub)
            qs_ref[rows, :KV_LORA] = qlat_ref[toks, KV_LORA * h:KV_LORA * (h + 1)]
            qs_ref[rows, KV_LORA:] = jnp.where(lane_head == h, qr, jnp.zeros_like(qr))
    m_ref[...] = jnp.full_like(m_ref, NEG_BIG)
    l_ref[...] = jnp.zeros_like(l_ref)
    acc_ref[...] = jnp.zeros_like(acc_ref)
    n = MLA_HEADS * tq // ATTN_ROW_SPLIT
    groups_per_run = ATTN_ROW_SPLIT // ATTN_QUERY_RUNS

    def block(k, causal):
        kc = kcat_ref[pl.ds(pl.multiple_of(k * tq, tq), tq), :]

        def visible(part):
            run = part // groups_per_run
            return kc[:(run + 1) * sub] if causal else kc

        def scores(part):
            s = _dot_nt(qs_ref[pl.ds(part * n, n), :], visible(part))
            if causal:
                run = part // groups_per_run
                r = (lax.broadcasted_iota(jnp.int32, s.shape, 0) & (sub - 1)) + run * sub
                c = lax.broadcasted_iota(jnp.int32, s.shape, 1)
                s = jnp.where(c <= r, s, NEG_BIG)
            return s

        ready = [scores(part) for part in range(min(ATTN_LOOKAHEAD, ATTN_ROW_SPLIT))]
        for part in range(ATTN_ROW_SPLIT):
            if part + ATTN_LOOKAHEAD < ATTN_ROW_SPLIT:
                ready.append(scores(part + ATTN_LOOKAHEAD))
            _softmax_block_update(ready.pop(0), visible(part)[:, :KV_LORA], pl.ds(part * n, n),
                                  m_ref, l_ref, acc_ref)

    def past(k, carry):
        block(k, False)
        return carry

    lax.fori_loop(0, qi, past, 0)
    block(qi, True)
    o = acc_ref[...] / _lanes(l_ref[...], KV_LORA)
    for run in range(ATTN_QUERY_RUNS):
        for h in range(MLA_HEADS):
            seg = run * MLA_HEADS + h
            o_ref[pl.ds(run * sub, sub), KV_LORA * h:KV_LORA * (h + 1)] = o[seg * sub:(seg + 1) * sub].astype(BF16)


def _attn_prompt(qlat, qrope, kcat, *, batch, seq):
    tq = ATTN_TILE
    nq = seq // tq
    rows = MLA_HEADS * tq
    qmap = lambda b, i: (b * nq + i, 0)
    return pl.pallas_call(
        _attn_prompt_kernel,
        out_shape=jax.ShapeDtypeStruct((batch * seq, LAT_LANES), BF16),
        grid=(batch, nq),
        in_specs=[pl.BlockSpec((tq, LAT_LANES), qmap),
                  pl.BlockSpec((tq, ROPE_LANES), qmap),
                  pl.BlockSpec((seq, KV_LORA + ROPE_LANES), lambda b, i: (b, 0))],
        out_specs=pl.BlockSpec((tq, LAT_LANES), qmap),
        scratch_shapes=[pltpu.VMEM((rows, KV_LORA + ROPE_LANES), BF16),
                        pltpu.VMEM((rows, LANES), F32),
                        pltpu.VMEM((rows, LANES), F32),
                        pltpu.VMEM((rows, KV_LORA), F32)],
        compiler_params=pltpu.CompilerParams(dimension_semantics=("parallel", "arbitrary"),
                                             vmem_limit_bytes=V7X_VMEM_LIMIT_BYTES),
        name="attn_prompt",
    )(qlat, qrope, kcat)


def _softmax_part(s, v):
    m = jnp.max(s, axis=-1, keepdims=True)
    p = jnp.exp2(s - m)
    return m, jnp.sum(p, axis=-1, keepdims=True), _dot(p.astype(BF16), v)


def _merge_softmax_parts(parts, m_ref, l_ref, acc_ref):
    m_old = m_ref[...]
    m_new = m_old
    for m, _, _ in parts:
        m_new = jnp.maximum(m_new, m)
    alpha = jnp.exp2(m_old - m_new)
    l = alpha * l_ref[...]
    acc = _lanes(alpha, KV_LORA) * acc_ref[...]
    for m, psum, pv in parts:
        wgt = jnp.exp2(m - m_new)
        l = l + wgt * psum
        acc = acc + _lanes(wgt, KV_LORA) * pv
    m_ref[...] = m_new
    l_ref[...] = l
    acc_ref[...] = acc


def _attn_sample_kernel(pt_ref, qlat_ref, qrope_ref, ckvn_ref, kropen_ref, cl_hbm, cr_hbm, o_ref,
                        kbuf, rbuf, sem, qs_ref, qr_ref, kn_ref, rn_ref, m_ref, l_ref, acc_ref,
                        *, n_pages, n_new):
    pg = SAMPLE_PAGES_PER_STEP
    chunks = n_pages // pg
    g = pl.program_id(0)
    n_steps = pl.num_programs(0)
    c = g % chunks
    last = n_steps - 1
    slot = lax.rem(g, SAMPLE_SLOTS)
    ahead = SAMPLE_SLOTS - 1

    def page_copies(step, dst_slot):
        first = (step // chunks) * n_pages + (step % chunks) * pg
        copies = []
        for p in range(pg):
            page = pt_ref[first + p]
            keys = pl.ds(p * PAGE_SIZE, PAGE_SIZE)
            copies.append(pltpu.make_async_copy(cl_hbm.at[page], kbuf.at[dst_slot, keys], sem.at[0, dst_slot]))
            copies.append(pltpu.make_async_copy(cr_hbm.at[page], rbuf.at[dst_slot, :, keys], sem.at[1, dst_slot]))
        return copies

    @pl.when(g == 0)
    def _():
        for step in range(ahead):
            for cp in page_copies(step, step):
                cp.start()

    @pl.when(c == 0)
    def _():
        ql = qlat_ref[0].astype(F32)
        qr = qrope_ref[0].astype(F32)
        for h in range(MLA_HEADS):
            rows = pl.ds(h * n_new, n_new)
            qs_ref[rows, :] = ql[:, KV_LORA * h:KV_LORA * (h + 1)]
            qr_ref[rows, :] = qr[:, QK_ROPE * h:QK_ROPE * (h + 1)]
        m_ref[...] = jnp.full_like(m_ref, NEG_BIG)
        l_ref[...] = jnp.zeros_like(l_ref)
        acc_ref[...] = jnp.zeros_like(acc_ref)

    for cp in page_copies(g, slot):
        cp.wait()

    qs = qs_ref[...].astype(BF16)
    qr = qr_ref[...].astype(BF16)
    sub = pg * PAGE_SIZE // SAMPLE_SUBCHUNKS
    def scores(j):
        keys = pl.ds(j * sub, sub)
        kc = kbuf[slot, keys, :].astype(BF16)
        return _dot_nt(qs, kc) + _dot(qr, rbuf[slot, :, keys].astype(BF16)), kc

    parts = []
    ready = [scores(j) for j in range(min(SAMPLE_LOOKAHEAD, SAMPLE_SUBCHUNKS))]
    for j in range(SAMPLE_SUBCHUNKS):
        if j + SAMPLE_LOOKAHEAD < SAMPLE_SUBCHUNKS:
            ready.append(scores(j + SAMPLE_LOOKAHEAD))
        parts.append(_softmax_part(*ready.pop(0)))
    _merge_softmax_parts(parts, m_ref, l_ref, acc_ref)

    for cp in page_copies(jnp.minimum(g + ahead, last), lax.rem(g + ahead, SAMPLE_SLOTS)):
        cp.start()

    @pl.when(g == last)
    def _():
        for extra in range(1, ahead + 1):
            for cp in page_copies(last, lax.rem(g + extra, SAMPLE_SLOTS)):
                cp.wait()

    @pl.when(c == chunks - 1)
    def _():
        kn_ref[...] = jnp.zeros_like(kn_ref)
        rn_ref[...] = jnp.zeros_like(rn_ref)
        kn_ref[pl.ds(0, n_new), :] = ckvn_ref[0]
        rn_ref[pl.ds(0, n_new), :] = kropen_ref[0]
        kn = kn_ref[...].astype(BF16)
        sn = _dot_nt(qs, kn) + _dot_nt(qr, rn_ref[...].astype(BF16))
        t = lax.broadcasted_iota(jnp.int32, sn.shape, 0) % n_new
        j = lax.broadcasted_iota(jnp.int32, sn.shape, 1)
        sn = jnp.where(j <= t, sn, NEG_BIG)
        _merge_softmax_parts([_softmax_part(sn, kn)], m_ref, l_ref, acc_ref)
        o_ref[0] = acc_ref[...] / _lanes(l_ref[...], KV_LORA)


def _attn_sample(page_table, qlat, qrope, ckv_new, krope_new, cache_latent, cache_k_rope):
    nb, n_pages = page_table.shape
    n_new = qlat.shape[0] // nb
    pg = SAMPLE_PAGES_PER_STEP
    chunks = n_pages // pg
    rows = MLA_HEADS * n_new
    new_pad = 8
    per_batch = lambda g, pt: (g // chunks, 0, 0)
    grid_spec = pltpu.PrefetchScalarGridSpec(
        num_scalar_prefetch=1,
        grid=(nb * chunks,),
        in_specs=[pl.BlockSpec((1, n_new, LAT_LANES), per_batch),
                  pl.BlockSpec((1, n_new, ROPE_LANES), per_batch),
                  pl.BlockSpec((1, n_new, KV_LORA), per_batch),
                  pl.BlockSpec((1, n_new, QK_ROPE), per_batch),
                  pl.BlockSpec(memory_space=pl.ANY),
                  pl.BlockSpec(memory_space=pl.ANY)],
        out_specs=pl.BlockSpec((1, rows, KV_LORA), per_batch),
        scratch_shapes=[pltpu.VMEM((SAMPLE_SLOTS, pg * PAGE_SIZE, KV_LORA), F32),
                        pltpu.VMEM((SAMPLE_SLOTS, QK_ROPE, pg * PAGE_SIZE), F32),
                        pltpu.SemaphoreType.DMA((2, SAMPLE_SLOTS)),
                        pltpu.VMEM((rows, KV_LORA), F32),
                        pltpu.VMEM((rows, QK_ROPE), F32),
                        pltpu.VMEM((new_pad, KV_LORA), F32),
                        pltpu.VMEM((new_pad, QK_ROPE), F32),
                        pltpu.VMEM((rows, LANES), F32),
                        pltpu.VMEM((rows, LANES), F32),
                        pltpu.VMEM((rows, KV_LORA), F32)])
    return pl.pallas_call(
        functools.partial(_attn_sample_kernel, n_pages=n_pages, n_new=n_new),
        out_shape=jax.ShapeDtypeStruct((nb, rows, KV_LORA), F32),
        grid_spec=grid_spec,
        compiler_params=pltpu.CompilerParams(dimension_semantics=("arbitrary",),
                                             vmem_limit_bytes=V7X_VMEM_LIMIT_BYTES),
        name="attn_sample",
    )(page_table.reshape(-1), qlat.reshape(nb, n_new, LAT_LANES), qrope.reshape(nb, n_new, ROPE_LANES),
      ckv_new.reshape(nb, n_new, KV_LORA), krope_new.reshape(nb, n_new, QK_ROPE), cache_latent, cache_k_rope)


def _post_mixer_kernel(h1_ref, ya_ref, ol_ref, gm_ref, g2_ref, gf_ref,
                       wgt_hbm, wuv_hbm, wbo_hbm, wout_hbm, wg_hbm, wu_hbm, wd_hbm,
                       y_ref,
                       wgt_ref, wuv_ref, wbo_ref, wout_ref, wg_ref, wu_ref, wd_ref, sem):
    @pl.when(pl.program_id(0) == 0)
    def _():
        _fetch_weights((wgt_hbm, wuv_hbm, wbo_hbm, wout_hbm, wg_hbm, wu_hbm, wd_hbm),
                       (wgt_ref, wuv_ref, wbo_ref, wout_ref, wg_ref, wu_ref, wd_ref), sem)

    h1 = h1_ref[...]
    xm = _rms(h1, gm_ref[...]).astype(BF16)
    gates = _dot(xm, wgt_ref[...])
    ol = ol_ref[...]
    o = _dot(ol[:, :KV_LORA], wuv_ref[0])
    for h in range(1, MLA_HEADS):
        o = o + _dot(ol[:, KV_LORA * h:KV_LORA * (h + 1)], wuv_ref[h])
    yb = _dot(o.astype(BF16), wbo_ref[...])
    merged = jax.nn.sigmoid(gates[:, :D_MODEL]) * ya_ref[...] + jax.nn.sigmoid(gates[:, D_MODEL:]) * yb
    h2 = h1 + _dot(merged.astype(BF16), wout_ref[...])
    h3 = _ffn_half_step(h2, g2_ref, wg_ref, wu_ref, wd_ref)
    y_ref[...] = _rms(h3, gf_ref[...])


def _post_mixer(h1, ya, olat, w, *, seq_major_ya):
    m = h1.shape[0]
    tm = TOKEN_TILE
    row = lambda i: (i, 0)
    if seq_major_ya:
        blocks_per_seq = seq_major_ya // tm
        ya_spec = pl.BlockSpec((tm, D_MODEL), lambda i: (i % blocks_per_seq, i // blocks_per_seq))
    else:
        ya_spec = pl.BlockSpec((tm, D_MODEL), row)
    gains = (w['g_mix'], w['g_ffn2'], w['g_final'])
    weights = (w['w_gates'], w['w_uv_pad'], w['w_b_out'], w['w_out'], w['wg2'], w['wu2'], w['wd2'])
    weight_specs, weight_scratch = _weight_specs(weights)
    return pl.pallas_call(
        _post_mixer_kernel,
        out_shape=jax.ShapeDtypeStruct((m, D_MODEL), F32),
        grid=(m // tm,),
        in_specs=[pl.BlockSpec((tm, D_MODEL), row), ya_spec, pl.BlockSpec((tm, LAT_LANES), row)]
        + [_resident(a.shape) for a in gains] + weight_specs,
        out_specs=pl.BlockSpec((tm, D_MODEL), row),
        scratch_shapes=weight_scratch,
        compiler_params=pltpu.CompilerParams(dimension_semantics=("arbitrary",),
                                             vmem_limit_bytes=V7X_VMEM_LIMIT_BYTES),
        name="post_mixer",
    )(h1, ya, olat, *gains, *weights)


def _rope_tables(pos):
    inv_freq = 1.0 / (ROPE_BASE ** (jnp.arange(0, QK_ROPE, 2, dtype=F32) / QK_ROPE))
    ang = pos.astype(F32)[:, None] * inv_freq[None, :]
    cos, sin = jnp.cos(ang), jnp.sin(ang)
    cos_tab = jnp.tile(jnp.concatenate([cos, cos], axis=-1), (1, MLA_HEADS))
    sin_tab = jnp.tile(jnp.concatenate([-sin, sin], axis=-1), (1, MLA_HEADS))
    return cos_tab, sin_tab


def _swap_halves(a):
    half = QK_ROPE // 2
    return jnp.concatenate([a[..., half:], a[..., :half]], axis=-1)


def _block_diag_runs(blocks, run):
    g, r, c = blocks.shape
    eye = jnp.eye(run, dtype=blocks.dtype)
    tiles = jnp.einsum('tgrc,gk->tgrkc', blocks.reshape(g // run, run, r, c), eye)
    return tiles.reshape(g // run, run * r, run * c)


def _prepare_weights(p):
    w = {}
    vec = lambda a: a.reshape(1, -1).astype(F32)
    for name in ('g_ffn1', 'g_mix', 'g_q', 'g_kv', 'g_ffn2', 'g_final', 'b_glu'):
        w[name] = vec(p[name])
    for src, dst in (('w_ffn1_gate', 'wg1'), ('w_ffn1_up', 'wu1'), ('w_ffn1_down', 'wd1'),
                     ('w_ffn2_gate', 'wg2'), ('w_ffn2_up', 'wu2'), ('w_ffn2_down', 'wd2'),
                     ('w_glu', 'w_glu'), ('w_a_out', 'w_a_out'), ('w_b_out', 'w_b_out'), ('w_out', 'w_out')):
        w[dst] = p[src].astype(BF16)
    w_in = p['w_in']
    off_kr = S5_WIDTH + Q_LORA + KV_LORA
    k_r = w_in[:, off_kr:off_kr + QK_ROPE]
    w['w_in_ext'] = jnp.concatenate(
        [w_in[:, :off_kr], jnp.tile(k_r, (1, MLA_HEADS)), jnp.tile(_swap_halves(k_r), (1, MLA_HEADS))],
        axis=1).astype(BF16)
    w['w_gates'] = w_in[:, off_kr + QK_ROPE:].astype(BF16)
    w_uq = p['w_uq']
    uq_rope = w_uq[:, :, QK_NOPE:]
    w['w_uq_ext'] = jnp.concatenate(
        [w_uq[:, :, :QK_NOPE].reshape(Q_LORA, -1), uq_rope.reshape(Q_LORA, -1),
         _swap_halves(uq_rope).reshape(Q_LORA, -1)], axis=1).astype(BF16)
    uk = jnp.transpose(p['w_uk'], (1, 2, 0))
    uk_pad = jnp.zeros((MLA_HEADS, 2, QK_NOPE, KV_LORA), F32)
    uk_pad = uk_pad.at[jnp.arange(MLA_HEADS), jnp.arange(MLA_HEADS) % 2].set(uk)
    w['w_uk_pad'] = uk_pad.reshape(MLA_HEADS, 2 * QK_NOPE, KV_LORA).astype(BF16)
    uv = jnp.transpose(p['w_uv'], (1, 0, 2))
    uv_pad = jnp.zeros((MLA_HEADS, KV_LORA, MLA_HEADS, V_DIM), F32)
    uv_pad = uv_pad.at[jnp.arange(MLA_HEADS), :, jnp.arange(MLA_HEADS)].set(uv)
    w['w_uv_pad'] = uv_pad.reshape(MLA_HEADS, KV_LORA, MLA_HEADS * V_DIM).astype(BF16)
    lam_re, lam_im = p['s5_a_re'].astype(F32), p['s5_a_im'].astype(F32)
    dt = jnp.exp(p['s5_log_dt'].astype(F32))[:, None]
    mag = jnp.exp(lam_re * dt)
    ab_re, ab_im = mag * jnp.cos(lam_im * dt), mag * jnp.sin(lam_im * dt)
    den = lam_re * lam_re + lam_im * lam_im
    num_re, num_im = ab_re - 1.0, ab_im
    k_re = (num_re * lam_re + num_im * lam_im) / den
    k_im = (num_im * lam_re - num_re * lam_im) / den
    w['a_re'] = ab_re.reshape(1, S5_COLS)
    w['a_im'] = ab_im.reshape(1, S5_COLS)
    b_re, b_im = p['s5_b_re'].astype(F32), p['s5_b_im'].astype(F32)
    kb_re = k_re[..., None] * b_re - k_im[..., None] * b_im
    kb_im = k_re[..., None] * b_im + k_im[..., None] * b_re

    def to_in(a):
        tiles = _block_diag_runs(jnp.transpose(a, (0, 2, 1)), S5_GROUPS_PER_STATE_TILE)
        n, r, c = tiles.shape
        pos = jnp.arange(n) % S5_TILES_PER_SLAB
        slab = jnp.zeros((n, S5_TILES_PER_SLAB, r, c), F32).at[jnp.arange(n), pos].set(tiles)
        return slab.reshape(n, S5_TILES_PER_SLAB * r, c)

    w['bpack'] = jnp.concatenate([to_in(kb_re), to_in(kb_im)], axis=0).astype(BF16)
    to_out = lambda a: _block_diag_runs(jnp.transpose(a, (0, 2, 1)), S5_GROUPS_PER_OUT_TILE)
    w['cpack'] = jnp.concatenate([to_out(p['s5_c_re'].astype(F32)), -to_out(p['s5_c_im'].astype(F32))],
                                 axis=1).astype(BF16)
    w['s5_d'] = vec(p['s5_d'])
    return w


def kernel(x_prompt, x_sample, cache_latent, cache_k_rope, state_ssm_re, state_ssm_im, page_table, g_ffn1, w_ffn1_gate, w_ffn1_up, w_ffn1_down, g_mix, w_in, s5_a_re, s5_a_im, s5_log_dt, s5_b_re, s5_b_im, s5_c_re, s5_c_im, s5_d, w_glu, b_glu, w_a_out, g_q, w_uq, g_kv, w_uk, w_uv, w_b_out, w_out, g_ffn2, w_ffn2_gate, w_ffn2_up, w_ffn2_down, g_final):
    layer = 0
    params = dict(
        g_ffn1=g_ffn1, w_ffn1_gate=w_ffn1_gate, w_ffn1_up=w_ffn1_up, w_ffn1_down=w_ffn1_down,
        g_mix=g_mix, w_in=w_in, s5_a_re=s5_a_re, s5_a_im=s5_a_im, s5_log_dt=s5_log_dt,
        s5_b_re=s5_b_re, s5_b_im=s5_b_im, s5_c_re=s5_c_re, s5_c_im=s5_c_im, s5_d=s5_d,
        w_glu=w_glu, b_glu=b_glu, w_a_out=w_a_out, g_q=g_q, w_uq=w_uq, g_kv=g_kv, w_uk=w_uk,
        w_uv=w_uv, w_b_out=w_b_out, w_out=w_out, g_ffn2=g_ffn2, w_ffn2_gate=w_ffn2_gate,
        w_ffn2_up=w_ffn2_up, w_ffn2_down=w_ffn2_down)
    p = {k: v[layer] for k, v in params.items()}
    p['g_final'] = g_final
    w = _prepare_weights(p)

    nb, seq, _ = x_prompt.shape
    cos_p, sin_p = _rope_tables(jnp.arange(seq))
    h1, u, ckv, krope, kcat, qlat, qrope = _pre_mixer(
        x_prompt.reshape(nb * seq, D_MODEL), cos_p, sin_p, w, seq_major_u=seq)
    olat = _attn_prompt(qlat, qrope, kcat, batch=nb, seq=seq)
    ya, sre_p, sim_p = _s5_prompt(u.reshape(seq, nb, S5_WIDTH), w)
    y_prompt = _post_mixer(h1, ya.reshape(seq, nb * D_MODEL), olat, w, seq_major_ya=seq).reshape(nb, seq, D_MODEL)
    ckv_prompt = ckv.reshape(1, nb, seq, KV_LORA)
    krope_prompt = krope.reshape(1, nb, seq, QK_ROPE)
    group = lambda a: a.reshape(1, a.shape[0], S5_GROUPS, S5_STATE)

    sb, n_new, _ = x_sample.shape
    n_pages = page_table.shape[1]
    past_len = n_pages * PAGE_SIZE
    cos_s, sin_s = _rope_tables(past_len + jnp.arange(n_new))
    cos_s, sin_s = jnp.tile(cos_s, (sb, 1)), jnp.tile(sin_s, (sb, 1))
    h1s, us, ckvs, kropes, _, qlats, qropes = _pre_mixer(
        x_sample.reshape(sb * n_new, D_MODEL), cos_s, sin_s, w, seq_major_u=0)
    us_t = jnp.transpose(us.reshape(sb, n_new, S5_WIDTH), (1, 0, 2))
    yas_t, sre_s, sim_s = _s5_sample(us_t, state_ssm_re[layer].reshape(sb, S5_COLS),
                                     state_ssm_im[layer].reshape(sb, S5_COLS), w)
    yas = jnp.transpose(yas_t, (1, 0, 2)).reshape(sb * n_new, D_MODEL)
    rope_pages = jnp.swapaxes(cache_k_rope[layer], 1, 2)
    ol = _attn_sample(page_table, qlats, qropes, ckvs, kropes, cache_latent[layer], rope_pages)
    olats = jnp.transpose(ol.reshape(sb, MLA_HEADS, n_new, KV_LORA), (0, 2, 1, 3))
    olats = olats.reshape(sb * n_new, LAT_LANES).astype(BF16)
    y_sample = _post_mixer(h1s, yas, olats, w, seq_major_ya=0).reshape(sb, n_new, D_MODEL)

    return (y_prompt, y_sample, ckv_prompt, krope_prompt,
            ckvs.reshape(1, sb, n_new, KV_LORA), kropes.reshape(1, sb, n_new, QK_ROPE),
            group(sre_p), group(sim_p), group(sre_s), group(sim_s))
```

```python
import functools

import jax
import jax.numpy as jnp
from jax import lax
from jax.experimental import pallas as pl
from jax.experimental.pallas import tpu as pltpu

F32 = jnp.float32
BF16 = jnp.bfloat16

D_MODEL = 1024
D_FF = 2816
S5_WIDTH = 512
S5_GROUP = 16
S5_GROUPS = 32
S5_STATE = 64
S5_COLS = S5_GROUPS * S5_STATE
MLA_HEADS = 8
QK_NOPE = 64
QK_ROPE = 32
V_DIM = 64
Q_LORA = 384
KV_LORA = 256
ROPE_BASE = 10000.0
NORM_EPS = 1e-6
PAGE_SIZE = 128
ATTN_SCALE = (QK_NOPE + QK_ROPE) ** -0.5
Q_PRESCALE = ATTN_SCALE * 1.4426950408889634
LANES = 128
MXU_TILE = 256
S5_STATE_TILES = S5_COLS // MXU_TILE
S5_GROUPS_PER_STATE_TILE = MXU_TILE // S5_STATE
S5_TILES_PER_SLAB = LANES // (S5_GROUPS_PER_STATE_TILE * S5_GROUP)
S5_OUT_TILES = S5_WIDTH // MXU_TILE
S5_GROUPS_PER_OUT_TILE = MXU_TILE // S5_GROUP
ROPE_LANES = MLA_HEADS * QK_ROPE
LAT_LANES = MLA_HEADS * KV_LORA
IN_EXT = S5_WIDTH + Q_LORA + KV_LORA + 2 * ROPE_LANES
OFF_CQ = S5_WIDTH
OFF_CKV = OFF_CQ + Q_LORA
OFF_KR = OFF_CKV + KV_LORA
OFF_KRS = OFF_KR + ROPE_LANES
NEG_BIG = -1e30

V7X_VMEM_LIMIT_BYTES = 56 * 1024 * 1024
TOKEN_TILE = 512
S5_TIME_TILE = 64
S5_TIME_SPLIT = 2
S5_COL_TILE = 512
ATTN_TILE = 512
ATTN_ROW_SPLIT = 8
ATTN_QUERY_RUNS = 2
ATTN_LOOKAHEAD = 2
SAMPLE_LOOKAHEAD = 3
SAMPLE_PAGES_PER_STEP = 64
SAMPLE_SUBCHUNKS = 8
SAMPLE_SLOTS = 3


def _rms(x, g):
    return x * lax.rsqrt(jnp.mean(x * x, axis=-1, keepdims=True) + NORM_EPS) * g


def _dot(a, b):
    return jnp.dot(a, b, preferred_element_type=F32)


def _dot_nt(a, b):
    return lax.dot_general(a, b, (((1,), (1,)), ((), ())), preferred_element_type=F32)


def _ffn_half_step(x, g_ref, wg_ref, wu_ref, wd_ref):
    xn = _rms(x, g_ref[...]).astype(BF16)
    gate = _dot(xn, wg_ref[...])
    up = _dot(xn, wu_ref[...])
    hid = (jax.nn.silu(gate) * up).astype(BF16)
    return x + 0.5 * _dot(hid, wd_ref[...])


def _resident(shape):
    nd = len(shape)
    return pl.BlockSpec(shape, lambda *_: (0,) * nd, pipeline_mode=pl.Buffered(1))


def _pre_mixer_kernel(x_ref, cos_ref, sin_ref, g1_ref, wg_ref, wu_ref, wd_ref, gm_ref, win_ref,
                      gq_ref, gkv_ref, wuq_ref, wuk_ref,
                      h1_ref, u_ref, ckv_ref, krope_ref, kcat_ref, qlat_ref, qrope_ref):
    h1 = _ffn_half_step(x_ref[...], g1_ref, wg_ref, wu_ref, wd_ref)
    h1_ref[...] = h1
    xm = _rms(h1, gm_ref[...]).astype(BF16)
    proj = _dot(xm, win_ref[...])
    u_ref[...] = proj[:, :S5_WIDTH]
    cqn = _rms(proj[:, OFF_CQ:OFF_CKV], gq_ref[...]).astype(BF16)
    ckv = _rms(proj[:, OFF_CKV:OFF_KR], gkv_ref[...])
    ckv_ref[...] = ckv
    cos = cos_ref[...]
    sin = sin_ref[...]
    kr8 = proj[:, OFF_KR:OFF_KRS] * cos + proj[:, OFF_KRS:IN_EXT] * sin
    krope_ref[...] = kr8[:, :QK_ROPE]
    kcat_ref[:, :KV_LORA] = ckv.astype(BF16)
    kcat_ref[:, KV_LORA:] = kr8.astype(BF16)
    q = _dot(cqn, wuq_ref[...])
    n_nope = MLA_HEADS * QK_NOPE
    qrope_ref[...] = ((q[:, n_nope:n_nope + ROPE_LANES] * cos
                       + q[:, n_nope + ROPE_LANES:] * sin) * Q_PRESCALE).astype(BF16)
    qn = q[:, :n_nope].astype(BF16)
    for h in range(MLA_HEADS):
        pair = h // 2
        qlat_ref[:, KV_LORA * h:KV_LORA * (h + 1)] = (_dot(
            qn[:, LANES * pair:LANES * (pair + 1)], wuk_ref[h]) * Q_PRESCALE).astype(BF16)


def _pre_mixer(x, cos_tab, sin_tab, w, *, seq_major_u):
    m = x.shape[0]
    tm = TOKEN_TILE
    nsteps = m // tm
    tab_blocks = cos_tab.shape[0] // tm
    row = lambda i: (i, 0)
    tab = lambda i: (i % tab_blocks, 0)
    if seq_major_u:
        blocks_per_seq = seq_major_u // tm
        u_shape = (seq_major_u, (m // seq_major_u) * S5_WIDTH)
        u_spec = pl.BlockSpec((tm, S5_WIDTH), lambda i: (i % blocks_per_seq, i // blocks_per_seq))
    else:
        u_shape = (m, S5_WIDTH)
        u_spec = pl.BlockSpec((tm, S5_WIDTH), row)
    out_shape = (
        jax.ShapeDtypeStruct((m, D_MODEL), F32),
        jax.ShapeDtypeStruct(u_shape, F32),
        jax.ShapeDtypeStruct((m, KV_LORA), F32),
        jax.ShapeDtypeStruct((m, QK_ROPE), F32),
        jax.ShapeDtypeStruct((m, KV_LORA + ROPE_LANES), BF16),
        jax.ShapeDtypeStruct((m, LAT_LANES), BF16),
        jax.ShapeDtypeStruct((m, ROPE_LANES), BF16),
    )
    out_specs = (
        pl.BlockSpec((tm, D_MODEL), row),
        u_spec,
        pl.BlockSpec((tm, KV_LORA), row),
        pl.BlockSpec((tm, QK_ROPE), row),
        pl.BlockSpec((tm, KV_LORA + ROPE_LANES), row),
        pl.BlockSpec((tm, LAT_LANES), row),
        pl.BlockSpec((tm, ROPE_LANES), row),
    )
    weights = (w['g_ffn1'], w['wg1'], w['wu1'], w['wd1'], w['g_mix'], w['w_in_ext'],
               w['g_q'], w['g_kv'], w['w_uq_ext'], w['w_uk_pad'])
    in_specs = [pl.BlockSpec((tm, D_MODEL), row),
                pl.BlockSpec((tm, ROPE_LANES), tab),
                pl.BlockSpec((tm, ROPE_LANES), tab)] + [_resident(a.shape) for a in weights]
    return pl.pallas_call(
        _pre_mixer_kernel, out_shape=out_shape, grid=(nsteps,), in_specs=in_specs, out_specs=out_specs,
        compiler_params=pltpu.CompilerParams(dimension_semantics=("parallel",),
                                             vmem_limit_bytes=V7X_VMEM_LIMIT_BYTES),
        name="pre_mixer",
    )(x, cos_tab, sin_tab, *weights)


def _s5_readout(h_ref, rows, u, cpack_ref, d_ref, wglu_ref, bglu_ref, waout_ref):
    k = S5_COLS // S5_OUT_TILES
    tiles = []
    for n in range(S5_OUT_TILES):
        h_re = h_ref[rows, k * n:k * (n + 1)].astype(BF16)
        h_im = h_ref[rows, S5_COLS + k * n:S5_COLS + k * (n + 1)].astype(BF16)
        tiles.append(_dot(h_re, cpack_ref[n, :k]) + _dot(h_im, cpack_ref[n, k:]))
    y = jnp.concatenate(tiles, axis=1) + d_ref[...] * u
    z = jax.nn.gelu(y)
    gate = _dot(z.astype(BF16), wglu_ref[...]) + bglu_ref[...]
    return _dot((z * jax.nn.sigmoid(gate)).astype(BF16), waout_ref[...])


def _s5_drive(u, bpack_ref, xs_ref, rows):
    u_bf = u.astype(BF16)
    for j in range(2 * S5_STATE_TILES):
        slab = (j % S5_STATE_TILES) // S5_TILES_PER_SLAB
        xs_ref[rows, MXU_TILE * j:MXU_TILE * (j + 1)] = _dot(u_bf[:, LANES * slab:LANES * (slab + 1)],
                                                            bpack_ref[j])


def _s5_prompt_kernel(u_ref, order_ref, bpack_ref, are_ref, aim_ref, cpack_ref, d_ref, wglu_ref, bglu_ref,
                      waout_ref, ya_ref, sre_ref, sim_ref, xs_ref, hre_ref, him_ref):
    del order_ref
    tt, nb, _ = u_ref.shape
    tp = tt // S5_TIME_SPLIT
    rp = tp * nb

    @pl.when(pl.program_id(0) == 0)
    def _():
        hre_ref[...] = jnp.zeros_like(hre_ref)
        him_ref[...] = jnp.zeros_like(him_ref)

    u = u_ref[...].reshape(tt * nb, S5_WIDTH)
    for q in range(S5_TIME_SPLIT):
        _s5_drive(u[q * rp:(q + 1) * rp], bpack_ref, xs_ref, pl.ds(q * rp, rp))
    for q in range(S5_TIME_SPLIT):
        for cb in range(S5_COLS // S5_COL_TILE):
            re_cols = pl.ds(cb * S5_COL_TILE, S5_COL_TILE)
            im_cols = pl.ds(S5_COLS + cb * S5_COL_TILE, S5_COL_TILE)
            ar = jnp.broadcast_to(are_ref[:, re_cols], (nb, S5_COL_TILE))
            ai = jnp.broadcast_to(aim_ref[:, re_cols], (nb, S5_COL_TILE))
            hr, hi = hre_ref[:, re_cols], him_ref[:, re_cols]
            for t in range(q * tp, (q + 1) * tp):
                r = pl.ds(t * nb, nb)
                hr, hi = (ar * hr - ai * hi + xs_ref[r, re_cols], ar * hi + ai * hr + xs_ref[r, im_cols])
                xs_ref[r, re_cols] = hr
                xs_ref[r, im_cols] = hi
            hre_ref[:, re_cols] = hr
            him_ref[:, re_cols] = hi
        ya = _s5_readout(xs_ref, pl.ds(q * rp, rp), u[q * rp:(q + 1) * rp],
                         cpack_ref, d_ref, wglu_ref, bglu_ref, waout_ref)
        ya_ref[pl.ds(q * tp, tp)] = ya.reshape(tp, nb, D_MODEL)
    sre_ref[...] = hre_ref[...]
    sim_ref[...] = him_ref[...]


def _s5_prompt(u3, w, run_after):
    seq, nb, _ = u3.shape
    tt = S5_TIME_TILE
    weights = (run_after[:16, :LANES],
               w['bpack'], w['a_re'], w['a_im'], w['cpack'], w['s5_d'], w['w_glu'], w['b_glu'], w['w_a_out'])
    state = pl.BlockSpec((nb, S5_COLS), lambda i: (0, 0))
    return pl.pallas_call(
        _s5_prompt_kernel,
        out_shape=(jax.ShapeDtypeStruct((seq, nb, D_MODEL), F32),
                   jax.ShapeDtypeStruct((nb, S5_COLS), F32),
                   jax.ShapeDtypeStruct((nb, S5_COLS), F32)),
        grid=(seq // tt,),
        in_specs=[pl.BlockSpec((tt, nb, S5_WIDTH), lambda i: (i, 0, 0))] + [_resident(a.shape) for a in weights],
        out_specs=(pl.BlockSpec((tt, nb, D_MODEL), lambda i: (i, 0, 0)), state, state),
        scratch_shapes=[pltpu.VMEM((tt * nb, 2 * S5_COLS), F32),
                        pltpu.VMEM((nb, S5_COLS), F32),
                        pltpu.VMEM((nb, S5_COLS), F32)],
        compiler_params=pltpu.CompilerParams(dimension_semantics=("arbitrary",),
                                             vmem_limit_bytes=V7X_VMEM_LIMIT_BYTES),
        name="s5_prompt",
    )(u3, *weights)


def _s5_sample_kernel(u_ref, h0re_ref, h0im_ref, bpack_ref, are_ref, aim_ref, cpack_ref, d_ref, wglu_ref,
                      bglu_ref, waout_ref, ya_ref, sre_ref, sim_ref, hs_ref):
    tt, nb, _ = u_ref.shape
    u = u_ref[...].reshape(tt * nb, S5_WIDTH)
    _s5_drive(u, bpack_ref, hs_ref, pl.ds(0, tt * nb))
    ar = are_ref[...]
    ai = aim_ref[...]
    hr = h0re_ref[...]
    hi = h0im_ref[...]
    for t in range(tt):
        r = pl.ds(t * nb, nb)
        hr, hi = (ar * hr - ai * hi + hs_ref[r, :S5_COLS], ar * hi + ai * hr + hs_ref[r, S5_COLS:])
        hs_ref[r, :S5_COLS] = hr
        hs_ref[r, S5_COLS:] = hi
    sre_ref[...] = hr
    sim_ref[...] = hi
    ya = _s5_readout(hs_ref, pl.ds(0, tt * nb), u, cpack_ref, d_ref, wglu_ref, bglu_ref, waout_ref)
    ya_ref[...] = ya.reshape(tt, nb, D_MODEL)


def _s5_sample(u3, h0_re, h0_im, w):
    tt, nb, _ = u3.shape
    weights = (w['bpack'], w['a_re'], w['a_im'], w['cpack'], w['s5_d'], w['w_glu'], w['b_glu'], w['w_a_out'])
    args = (u3, h0_re, h0_im) + weights
    whole = lambda a: pl.BlockSpec(a.shape, lambda i, nd=a.ndim: (0,) * nd)
    outs = (jax.ShapeDtypeStruct((tt, nb, D_MODEL), F32),
            jax.ShapeDtypeStruct((nb, S5_COLS), F32),
            jax.ShapeDtypeStruct((nb, S5_COLS), F32))
    return pl.pallas_call(
        _s5_sample_kernel, out_shape=outs, grid=(1,),
        in_specs=[whole(a) for a in args], out_specs=tuple(whole(o) for o in outs),
        scratch_shapes=[pltpu.VMEM((tt * nb, 2 * S5_COLS), F32)],
        compiler_params=pltpu.CompilerParams(dimension_semantics=("arbitrary",),
                                             vmem_limit_bytes=V7X_VMEM_LIMIT_BYTES),
        name="s5_sample",
    )(*args)


def _lanes(stat, width):
    if width % LANES:
        return stat[:, :width]
    return jnp.tile(stat, (1, width // LANES))


def _softmax_block_update(s, v, rows, m_ref, l_ref, acc_ref, first=False):
    if first:
        m_new = jnp.broadcast_to(jnp.max(s, axis=-1, keepdims=True), (s.shape[0], LANES))
        p = jnp.exp2(s - _lanes(m_new, s.shape[1]))
        l_ref[rows, :] = jnp.broadcast_to(jnp.sum(p, axis=-1, keepdims=True), (s.shape[0], LANES))
        acc_ref[rows, :] = _dot(p.astype(BF16), v)
        m_ref[rows, :] = m_new
        return
    m_old = m_ref[rows, :]
    m_new = jnp.maximum(m_old, jnp.max(s, axis=-1, keepdims=True))
    alpha = jnp.exp2(m_old - m_new)
    p = jnp.exp2(s - _lanes(m_new, s.shape[1]))
    l_ref[rows, :] = alpha * l_ref[rows, :] + jnp.sum(p, axis=-1, keepdims=True)
    acc_ref[rows, :] = _lanes(alpha, KV_LORA) * acc_ref[rows, :] + _dot(p.astype(BF16), v)
    m_ref[rows, :] = m_new


def _attn_prompt_kernel(qlat_ref, qrope_ref, kcat_ref, o_ref, qs_ref, m_ref, l_ref, acc_ref):
    tq = ATTN_TILE
    sub = tq // ATTN_QUERY_RUNS
    qi = pl.program_id(1)
    lane_head = lax.broadcasted_iota(jnp.int32, (sub, ROPE_LANES), 1) // QK_ROPE
    for run in range(ATTN_QUERY_RUNS):
        toks = pl.ds(run * sub, sub)
        qr = qrope_ref[toks, :]
        for h in range(MLA_HEADS):
            rows = pl.ds((run * MLA_HEADS + h) * sub, sub)
            qs_ref[rows, :KV_LORA] = qlat_ref[toks, KV_LORA * h:KV_LORA * (h + 1)]
            qs_ref[rows, KV_LORA:] = jnp.where(lane_head == h, qr, jnp.zeros_like(qr))
    n = MLA_HEADS * tq // ATTN_ROW_SPLIT
    groups_per_run = ATTN_ROW_SPLIT // ATTN_QUERY_RUNS

    def block(k, causal, first=False):
        kc = kcat_ref[pl.ds(pl.multiple_of(k * tq, tq), tq), :]

        def visible(part):
            run = part // groups_per_run
            return kc[:(run + 1) * sub] if causal else kc

        def scores(part):
            s = _dot_nt(qs_ref[pl.ds(part * n, n), :], visible(part))
            if causal:
                run = part // groups_per_run
                r = (lax.broadcasted_iota(jnp.int32, s.shape, 0) & (sub - 1)) + run * sub
                c = lax.broadcasted_iota(jnp.int32, s.shape, 1)
                s = jnp.where(c <= r, s, NEG_BIG)
            return s

        ready = [scores(part) for part in range(min(ATTN_LOOKAHEAD, ATTN_ROW_SPLIT))]
        for part in range(ATTN_ROW_SPLIT):
            if part + ATTN_LOOKAHEAD < ATTN_ROW_SPLIT:
                ready.append(scores(part + ATTN_LOOKAHEAD))
            _softmax_block_update(ready.pop(0), visible(part)[:, :KV_LORA], pl.ds(part * n, n),
                                  m_ref, l_ref, acc_ref, first=first)

    def past(k, carry):
        block(k, False)
        return carry

    @pl.when(qi == 0)
    def _():
        block(0, True, first=True)

    @pl.when(qi > 0)
    def _():
        block(0, False, first=True)
        lax.fori_loop(1, qi, past, 0)
        block(qi, True)

    o = acc_ref[...] / _lanes(l_ref[...], KV_LORA)
    for run in range(ATTN_QUERY_RUNS):
        for h in range(MLA_HEADS):
            seg = run * MLA_HEADS + h
            o_ref[pl.ds(run * sub, sub), KV_LORA * h:KV_LORA * (h + 1)] = o[seg * sub:(seg + 1) * sub].astype(BF16)


def _attn_prompt(qlat, qrope, kcat, *, batch, seq):
    tq = ATTN_TILE
    nq = seq // tq
    rows = MLA_HEADS * tq
    qmap = lambda b, i: (b * nq + i, 0)
    return pl.pallas_call(
        _attn_prompt_kernel,
        out_shape=jax.ShapeDtypeStruct((batch * seq, LAT_LANES), BF16),
        grid=(batch, nq),
        in_specs=[pl.BlockSpec((tq, LAT_LANES), qmap),
                  pl.BlockSpec((tq, ROPE_LANES), qmap),
                  pl.BlockSpec((seq, KV_LORA + ROPE_LANES), lambda b, i: (b, 0))],
        out_specs=pl.BlockSpec((tq, LAT_LANES), qmap),
        scratch_shapes=[pltpu.VMEM((rows, KV_LORA + ROPE_LANES), BF16),
                        pltpu.VMEM((rows, LANES), F32),
                        pltpu.VMEM((rows, LANES), F32),
                        pltpu.VMEM((rows, KV_LORA), F32)],
        compiler_params=pltpu.CompilerParams(dimension_semantics=("parallel", "arbitrary"),
                                             vmem_limit_bytes=V7X_VMEM_LIMIT_BYTES),
        name="attn_prompt",
    )(qlat, qrope, kcat)


def _softmax_part(s, v):
    m = jnp.max(s, axis=-1, keepdims=True)
    p = jnp.exp2(s - m)
    return m, jnp.sum(p, axis=-1, keepdims=True), _dot(p.astype(BF16), v)


def _merge_softmax_parts(parts, m_ref, l_ref, acc_ref):
    m_old = m_ref[...]
    m_new = m_old
    for m, _, _ in parts:
        m_new = jnp.maximum(m_new, m)
    alpha = jnp.exp2(m_old - m_new)
    l = alpha * l_ref[...]
    acc = _lanes(alpha, KV_LORA) * acc_ref[...]
    for m, psum, pv in parts:
        wgt = jnp.exp2(m - m_new)
        l = l + wgt * psum
        acc = acc + _lanes(wgt, KV_LORA) * pv
    m_ref[...] = m_new
    l_ref[...] = l
    acc_ref[...] = acc


def _attn_sample_kernel(pt_ref, qlat_ref, qrope_ref, ckvn_ref, kropen_ref, cl_hbm, cr_hbm, o_ref,
                        kbuf, rbuf, sem, qs_ref, qr_ref, kn_ref, rn_ref, m_ref, l_ref, acc_ref,
                        *, n_pages, n_new):
    pg = SAMPLE_PAGES_PER_STEP
    chunks = n_pages // pg
    g = pl.program_id(0)
    n_steps = pl.num_programs(0)
    c = g % chunks
    last = n_steps - 1
    slot = lax.rem(g, SAMPLE_SLOTS)
    ahead = SAMPLE_SLOTS - 1

    def page_copies(step, dst_slot):
        first = (step // chunks) * n_pages + (step % chunks) * pg
        copies = []
        for p in range(pg):
            page = pt_ref[first + p]
            keys = pl.ds(p * PAGE_SIZE, PAGE_SIZE)
            copies.append(pltpu.make_async_copy(cl_hbm.at[page], kbuf.at[dst_slot, keys], sem.at[0, dst_slot]))
            copies.append(pltpu.make_async_copy(cr_hbm.at[page], rbuf.at[dst_slot, :, keys], sem.at[1, dst_slot]))
        return copies

    @pl.when(g == 0)
    def _():
        for step in range(ahead):
            for cp in page_copies(step, step):
                cp.start()

    @pl.when(c == 0)
    def _():
        ql = qlat_ref[0].astype(F32)
        qr = qrope_ref[0].astype(F32)
        for h in range(MLA_HEADS):
            rows = pl.ds(h * n_new, n_new)
            qs_ref[rows, :] = ql[:, KV_LORA * h:KV_LORA * (h + 1)]
            qr_ref[rows, :] = qr[:, QK_ROPE * h:QK_ROPE * (h + 1)]
        m_ref[...] = jnp.full_like(m_ref, NEG_BIG)
        l_ref[...] = jnp.zeros_like(l_ref)
        acc_ref[...] = jnp.zeros_like(acc_ref)

    for cp in page_copies(g, slot):
        cp.wait()

    qs = qs_ref[...].astype(BF16)
    qr = qr_ref[...].astype(BF16)
    sub = pg * PAGE_SIZE // SAMPLE_SUBCHUNKS
    def scores(j):
        keys = pl.ds(j * sub, sub)
        kc = kbuf[slot, keys, :].astype(BF16)
        return _dot_nt(qs, kc) + _dot(qr, rbuf[slot, :, keys].astype(BF16)), kc

    parts = []
    ready = [scores(j) for j in range(min(SAMPLE_LOOKAHEAD, SAMPLE_SUBCHUNKS))]
    for j in range(SAMPLE_SUBCHUNKS):
        if j + SAMPLE_LOOKAHEAD < SAMPLE_SUBCHUNKS:
            ready.append(scores(j + SAMPLE_LOOKAHEAD))
        parts.append(_softmax_part(*ready.pop(0)))
    _merge_softmax_parts(parts, m_ref, l_ref, acc_ref)

    for cp in page_copies(jnp.minimum(g + ahead, last), lax.rem(g + ahead, SAMPLE_SLOTS)):
        cp.start()

    @pl.when(g == last)
    def _():
        for extra in range(1, ahead + 1):
            for cp in page_copies(last, lax.rem(g + extra, SAMPLE_SLOTS)):
                cp.wait()

    @pl.when(c == chunks - 1)
    def _():
        kn_ref[...] = jnp.zeros_like(kn_ref)
        rn_ref[...] = jnp.zeros_like(rn_ref)
        kn_ref[pl.ds(0, n_new), :] = ckvn_ref[0]
        rn_ref[pl.ds(0, n_new), :] = kropen_ref[0]
        kn = kn_ref[...].astype(BF16)
        sn = _dot_nt(qs, kn) + _dot_nt(qr, rn_ref[...].astype(BF16))
        t = lax.broadcasted_iota(jnp.int32, sn.shape, 0) % n_new
        j = lax.broadcasted_iota(jnp.int32, sn.shape, 1)
        sn = jnp.where(j <= t, sn, NEG_BIG)
        _merge_softmax_parts([_softmax_part(sn, kn)], m_ref, l_ref, acc_ref)
        o_ref[0] = acc_ref[...] / _lanes(l_ref[...], KV_LORA)


def _attn_sample(page_table, qlat, qrope, ckv_new, krope_new, cache_latent, cache_k_rope):
    nb, n_pages = page_table.shape
    n_new = qlat.shape[0] // nb
    pg = SAMPLE_PAGES_PER_STEP
    chunks = n_pages // pg
    rows = MLA_HEADS * n_new
    new_pad = 8
    per_batch = lambda g, pt: (g // chunks, 0, 0)
    grid_spec = pltpu.PrefetchScalarGridSpec(
        num_scalar_prefetch=1,
        grid=(nb * chunks,),
        in_specs=[pl.BlockSpec((1, n_new, LAT_LANES), per_batch),
                  pl.BlockSpec((1, n_new, ROPE_LANES), per_batch),
                  pl.BlockSpec((1, n_new, KV_LORA), per_batch),
                  pl.BlockSpec((1, n_new, QK_ROPE), per_batch),
                  pl.BlockSpec(memory_space=pl.ANY),
                  pl.BlockSpec(memory_space=pl.ANY)],
        out_specs=pl.BlockSpec((1, rows, KV_LORA), per_batch),
        scratch_shapes=[pltpu.VMEM((SAMPLE_SLOTS, pg * PAGE_SIZE, KV_LORA), F32),
                        pltpu.VMEM((SAMPLE_SLOTS, QK_ROPE, pg * PAGE_SIZE), F32),
                        pltpu.SemaphoreType.DMA((2, SAMPLE_SLOTS)),
                        pltpu.VMEM((rows, KV_LORA), F32),
                        pltpu.VMEM((rows, QK_ROPE), F32),
                        pltpu.VMEM((new_pad, KV_LORA), F32),
                        pltpu.VMEM((new_pad, QK_ROPE), F32),
                        pltpu.VMEM((rows, LANES), F32),
                        pltpu.VMEM((rows, LANES), F32),
                        pltpu.VMEM((rows, KV_LORA), F32)])
    return pl.pallas_call(
        functools.partial(_attn_sample_kernel, n_pages=n_pages, n_new=n_new),
        out_shape=jax.ShapeDtypeStruct((nb, rows, KV_LORA), F32),
        grid_spec=grid_spec,
        compiler_params=pltpu.CompilerParams(dimension_semantics=("arbitrary",),
                                             vmem_limit_bytes=V7X_VMEM_LIMIT_BYTES),
        name="attn_sample",
    )(page_table.reshape(-1), qlat.reshape(nb, n_new, LAT_LANES), qrope.reshape(nb, n_new, ROPE_LANES),
      ckv_new.reshape(nb, n_new, KV_LORA), krope_new.reshape(nb, n_new, QK_ROPE), cache_latent, cache_k_rope)


def _post_mixer_kernel(h1_ref, ya_ref, ol_ref, gm_ref, wgt_ref, wuv_ref, wbo_ref, wout_ref,
                       g2_ref, wg_ref, wu_ref, wd_ref, gf_ref, y_ref):
    h1 = h1_ref[...]
    xm = _rms(h1, gm_ref[...]).astype(BF16)
    gates = _dot(xm, wgt_ref[...])
    ol = ol_ref[...]
    o = _dot(ol[:, :KV_LORA], wuv_ref[0])
    for h in range(1, MLA_HEADS):
        o = o + _dot(ol[:, KV_LORA * h:KV_LORA * (h + 1)], wuv_ref[h])
    yb = _dot(o.astype(BF16), wbo_ref[...])
    merged = jax.nn.sigmoid(gates[:, :D_MODEL]) * ya_ref[...] + jax.nn.sigmoid(gates[:, D_MODEL:]) * yb
    h2 = h1 + _dot(merged.astype(BF16), wout_ref[...])
    h3 = _ffn_half_step(h2, g2_ref, wg_ref, wu_ref, wd_ref)
    y_ref[...] = _rms(h3, gf_ref[...])


def _post_mixer(h1, ya, olat, w, *, seq_major_ya):
    m = h1.shape[0]
    tm = TOKEN_TILE
    row = lambda i: (i, 0)
    if seq_major_ya:
        blocks_per_seq = seq_major_ya // tm
        ya_spec = pl.BlockSpec((tm, D_MODEL), lambda i: (i % blocks_per_seq, i // blocks_per_seq))
    else:
        ya_spec = pl.BlockSpec((tm, D_MODEL), row)
    weights = (w['g_mix'], w['w_gates'], w['w_uv_pad'], w['w_b_out'], w['w_out'],
               w['g_ffn2'], w['wg2'], w['wu2'], w['wd2'], w['g_final'])
    return pl.pallas_call(
        _post_mixer_kernel,
        out_shape=jax.ShapeDtypeStruct((m, D_MODEL), F32),
        grid=(m // tm,),
        in_specs=[pl.BlockSpec((tm, D_MODEL), row), ya_spec, pl.BlockSpec((tm, LAT_LANES), row)]
        + [_resident(a.shape) for a in weights],
        out_specs=pl.BlockSpec((tm, D_MODEL), row),
        compiler_params=pltpu.CompilerParams(dimension_semantics=("parallel",),
                                             vmem_limit_bytes=V7X_VMEM_LIMIT_BYTES),
        name="post_mixer",
    )(h1, ya, olat, *weights)


def _rope_tables(pos):
    inv_freq = 1.0 / (ROPE_BASE ** (jnp.arange(0, QK_ROPE, 2, dtype=F32) / QK_ROPE))
    ang = pos.astype(F32)[:, None] * inv_freq[None, :]
    cos, sin = jnp.cos(ang), jnp.sin(ang)
    cos_tab = jnp.tile(jnp.concatenate([cos, cos], axis=-1), (1, MLA_HEADS))
    sin_tab = jnp.tile(jnp.concatenate([-sin, sin], axis=-1), (1, MLA_HEADS))
    return cos_tab, sin_tab


def _swap_halves(a):
    half = QK_ROPE // 2
    return jnp.concatenate([a[..., half:], a[..., :half]], axis=-1)


def _block_diag_runs(blocks, run):
    g, r, c = blocks.shape
    eye = jnp.eye(run, dtype=blocks.dtype)
    tiles = jnp.einsum('tgrc,gk->tgrkc', blocks.reshape(g // run, run, r, c), eye)
    return tiles.reshape(g // run, run * r, run * c)


def _prepare_weights(p):
    w = {}
    vec = lambda a: a.reshape(1, -1).astype(F32)
    for name in ('g_ffn1', 'g_mix', 'g_q', 'g_kv', 'g_ffn2', 'g_final', 'b_glu'):
        w[name] = vec(p[name])
    for src, dst in (('w_ffn1_gate', 'wg1'), ('w_ffn1_up', 'wu1'), ('w_ffn1_down', 'wd1'),
                     ('w_ffn2_gate', 'wg2'), ('w_ffn2_up', 'wu2'), ('w_ffn2_down', 'wd2'),
                     ('w_glu', 'w_glu'), ('w_a_out', 'w_a_out'), ('w_b_out', 'w_b_out'), ('w_out', 'w_out')):
        w[dst] = p[src].astype(BF16)
    w_in = p['w_in']
    off_kr = S5_WIDTH + Q_LORA + KV_LORA
    k_r = w_in[:, off_kr:off_kr + QK_ROPE]
    w['w_in_ext'] = jnp.concatenate(
        [w_in[:, :off_kr], jnp.tile(k_r, (1, MLA_HEADS)), jnp.tile(_swap_halves(k_r), (1, MLA_HEADS))],
        axis=1).astype(BF16)
    w['w_gates'] = w_in[:, off_kr + QK_ROPE:].astype(BF16)
    w_uq = p['w_uq']
    uq_rope = w_uq[:, :, QK_NOPE:]
    w['w_uq_ext'] = jnp.concatenate(
        [w_uq[:, :, :QK_NOPE].reshape(Q_LORA, -1), uq_rope.reshape(Q_LORA, -1),
         _swap_halves(uq_rope).reshape(Q_LORA, -1)], axis=1).astype(BF16)
    uk = jnp.transpose(p['w_uk'], (1, 2, 0))
    uk_pad = jnp.zeros((MLA_HEADS, 2, QK_NOPE, KV_LORA), F32)
    uk_pad = uk_pad.at[jnp.arange(MLA_HEADS), jnp.arange(MLA_HEADS) % 2].set(uk)
    w['w_uk_pad'] = uk_pad.reshape(MLA_HEADS, 2 * QK_NOPE, KV_LORA).astype(BF16)
    uv = jnp.transpose(p['w_uv'], (1, 0, 2))
    uv_pad = jnp.zeros((MLA_HEADS, KV_LORA, MLA_HEADS, V_DIM), F32)
    uv_pad = uv_pad.at[jnp.arange(MLA_HEADS), :, jnp.arange(MLA_HEADS)].set(uv)
    w['w_uv_pad'] = uv_pad.reshape(MLA_HEADS, KV_LORA, MLA_HEADS * V_DIM).astype(BF16)
    lam_re, lam_im = p['s5_a_re'].astype(F32), p['s5_a_im'].astype(F32)
    dt = jnp.exp(p['s5_log_dt'].astype(F32))[:, None]
    mag = jnp.exp(lam_re * dt)
    ab_re, ab_im = mag * jnp.cos(lam_im * dt), mag * jnp.sin(lam_im * dt)
    den = lam_re * lam_re + lam_im * lam_im
    num_re, num_im = ab_re - 1.0, ab_im
    k_re = (num_re * lam_re + num_im * lam_im) / den
    k_im = (num_im * lam_re - num_re * lam_im) / den
    w['a_re'] = ab_re.reshape(1, S5_COLS)
    w['a_im'] = ab_im.reshape(1, S5_COLS)
    b_re, b_im = p['s5_b_re'].astype(F32), p['s5_b_im'].astype(F32)
    kb_re = k_re[..., None] * b_re - k_im[..., None] * b_im
    kb_im = k_re[..., None] * b_im + k_im[..., None] * b_re

    def to_in(a):
        tiles = _block_diag_runs(jnp.transpose(a, (0, 2, 1)), S5_GROUPS_PER_STATE_TILE)
        n, r, c = tiles.shape
        pos = jnp.arange(n) % S5_TILES_PER_SLAB
        slab = jnp.zeros((n, S5_TILES_PER_SLAB, r, c), F32).at[jnp.arange(n), pos].set(tiles)
        return slab.reshape(n, S5_TILES_PER_SLAB * r, c)

    w['bpack'] = jnp.concatenate([to_in(kb_re), to_in(kb_im)], axis=0).astype(BF16)
    to_out = lambda a: _block_diag_runs(jnp.transpose(a, (0, 2, 1)), S5_GROUPS_PER_OUT_TILE)
    w['cpack'] = jnp.concatenate([to_out(p['s5_c_re'].astype(F32)), -to_out(p['s5_c_im'].astype(F32))],
                                 axis=1).astype(BF16)
    w['s5_d'] = vec(p['s5_d'])
    return w


def kernel(x_prompt, x_sample, cache_latent, cache_k_rope, state_ssm_re, state_ssm_im, page_table, g_ffn1, w_ffn1_gate, w_ffn1_up, w_ffn1_down, g_mix, w_in, s5_a_re, s5_a_im, s5_log_dt, s5_b_re, s5_b_im, s5_c_re, s5_c_im, s5_d, w_glu, b_glu, w_a_out, g_q, w_uq, g_kv, w_uk, w_uv, w_b_out, w_out, g_ffn2, w_ffn2_gate, w_ffn2_up, w_ffn2_down, g_final):
    layer = 0
    params = dict(
        g_ffn1=g_ffn1, w_ffn1_gate=w_ffn1_gate, w_ffn1_up=w_ffn1_up, w_ffn1_down=w_ffn1_down,
        g_mix=g_mix, w_in=w_in, s5_a_re=s5_a_re, s5_a_im=s5_a_im, s5_log_dt=s5_log_dt,
        s5_b_re=s5_b_re, s5_b_im=s5_b_im, s5_c_re=s5_c_re, s5_c_im=s5_c_im, s5_d=s5_d,
        w_glu=w_glu, b_glu=b_glu, w_a_out=w_a_out, g_q=g_q, w_uq=w_uq, g_kv=g_kv, w_uk=w_uk,
        w_uv=w_uv, w_b_out=w_b_out, w_out=w_out, g_ffn2=g_ffn2, w_ffn2_gate=w_ffn2_gate,
        w_ffn2_up=w_ffn2_up, w_ffn2_down=w_ffn2_down)
    p = {k: v[layer] for k, v in params.items()}
    p['g_final'] = g_final
    w = _prepare_weights(p)

    nb, seq, _ = x_prompt.shape
    cos_p, sin_p = _rope_tables(jnp.arange(seq))
    h1, u, ckv, krope, kcat, qlat, qrope = _pre_mixer(
        x_prompt.reshape(nb * seq, D_MODEL), cos_p, sin_p, w, seq_major_u=seq)
    olat = _attn_prompt(qlat, qrope, kcat, batch=nb, seq=seq)
    ya, sre_p, sim_p = _s5_prompt(u.reshape(seq, nb, S5_WIDTH), w, run_after=olat)
    y_prompt = _post_mixer(h1, ya.reshape(seq, nb * D_MODEL), olat, w, seq_major_ya=seq).reshape(nb, seq, D_MODEL)
    ckv_prompt = ckv.reshape(1, nb, seq, KV_LORA)
    krope_prompt = krope.reshape(1, nb, seq, QK_ROPE)
    group = lambda a: a.reshape(1, a.shape[0], S5_GROUPS, S5_STATE)

    sb, n_new, _ = x_sample.shape
    n_pages = page_table.shape[1]
    past_len = n_pages * PAGE_SIZE
    cos_s, sin_s = _rope_tables(past_len + jnp.arange(n_new))
    cos_s, sin_s = jnp.tile(cos_s, (sb, 1)), jnp.tile(sin_s, (sb, 1))
    h1s, us, ckvs, kropes, _, qlats, qropes = _pre_mixer(
        x_sample.reshape(sb * n_new, D_MODEL), cos_s, sin_s, w, seq_major_u=0)
    us_t = jnp.transpose(us.reshape(sb, n_new, S5_WIDTH), (1, 0, 2))
    yas_t, sre_s, sim_s = _s5_sample(us_t, state_ssm_re[layer].reshape(sb, S5_COLS),
                                     state_ssm_im[layer].reshape(sb, S5_COLS), w)
    yas = jnp.transpose(yas_t, (1, 0, 2)).reshape(sb * n_new, D_MODEL)
    rope_pages = jnp.swapaxes(cache_k_rope[layer], 1, 2)
    ol = _attn_sample(page_table, qlats, qropes, ckvs, kropes, cache_latent[layer], rope_pages)
    olats = jnp.transpose(ol.reshape(sb, MLA_HEADS, n_new, KV_LORA), (0, 2, 1, 3))
    olats = olats.reshape(sb * n_new, LAT_LANES).astype(BF16)
    y_sample = _post_mixer(h1s, yas, olats, w, seq_major_ya=0).reshape(sb, n_new, D_MODEL)

    return (y_prompt, y_sample, ckv_prompt, krope_prompt,
            ckvs.reshape(1, sb, n_new, KV_LORA), kropes.reshape(1, sb, n_new, QK_ROPE),
            group(sre_p), group(sim_p), group(sre_s), group(sim_s))
```

```python
import functools

import jax
import jax.numpy as jnp
from jax import lax
from jax.experimental import pallas as pl
from jax.experimental.pallas import tpu as pltpu

F32 = jnp.float32
BF16 = jnp.bfloat16

D_MODEL = 1024
D_FF = 2816
S5_WIDTH = 512
S5_GROUP = 16
S5_GROUPS = 32
S5_STATE = 64
S5_COLS = S5_GROUPS * S5_STATE
MLA_HEADS = 8
QK_NOPE = 64
QK_ROPE = 32
V_DIM = 64
Q_LORA = 384
KV_LORA = 256
ROPE_BASE = 10000.0
NORM_EPS = 1e-6
PAGE_SIZE = 128
ATTN_SCALE = (QK_NOPE + QK_ROPE) ** -0.5
Q_PRESCALE = ATTN_SCALE * 1.4426950408889634
LANES = 128
MXU_TILE = 256
S5_STATE_TILES = S5_COLS // MXU_TILE
S5_GROUPS_PER_STATE_TILE = MXU_TILE // S5_STATE
S5_TILES_PER_SLAB = LANES // (S5_GROUPS_PER_STATE_TILE * S5_GROUP)
UV_HEADS_PER_TILE = MXU_TILE // V_DIM
S5_OUT_TILES = S5_WIDTH // MXU_TILE
S5_GROUPS_PER_OUT_TILE = MXU_TILE // S5_GROUP
ROPE_LANES = MLA_HEADS * QK_ROPE
LAT_LANES = MLA_HEADS * KV_LORA
IN_EXT = S5_WIDTH + Q_LORA + KV_LORA + 2 * LANES
OFF_CQ = S5_WIDTH
OFF_CKV = OFF_CQ + Q_LORA
OFF_KR = OFF_CKV + KV_LORA
OFF_KRS = OFF_KR + LANES
NEG_BIG = -1e30

V7X_VMEM_LIMIT_BYTES = 56 * 1024 * 1024
TOKEN_TILE = 512
S5_TIME_TILE = 64
S5_TIME_SPLIT = 2
S5_COL_TILE = 512
ATTN_TILE = 512
ATTN_ROW_SPLIT = 8
ATTN_QUERY_RUNS = 2
ATTN_LOOKAHEAD = 2
SAMPLE_LOOKAHEAD = 3
SAMPLE_PAGES_PER_STEP = 64
SAMPLE_SUBCHUNKS = 8
SAMPLE_SLOTS = 3


def _rms(x, g):
    return x * lax.rsqrt(jnp.mean(x * x, axis=-1, keepdims=True) + NORM_EPS) * g


def _dot(a, b):
    return jnp.dot(a, b, preferred_element_type=F32)


def _dot_nt(a, b):
    return lax.dot_general(a, b, (((1,), (1,)), ((), ())), preferred_element_type=F32)


def _ffn_half_step(x, g_ref, wg_ref, wu_ref, wd_ref):
    xn = _rms(x, g_ref[...]).astype(BF16)
    gate = _dot(xn, wg_ref[...])
    up = _dot(xn, wu_ref[...])
    hid = (jax.nn.silu(gate) * up).astype(BF16)
    return x + 0.5 * _dot(hid, wd_ref[...])


def _resident(shape):
    nd = len(shape)
    return pl.BlockSpec(shape, lambda *_: (0,) * nd, pipeline_mode=pl.Buffered(1))


def _pre_mixer_kernel(x_ref, cos_ref, sin_ref, g1_ref, wg_ref, wu_ref, wd_ref, gm_ref, win_ref,
                      gq_ref, gkv_ref, wuq_ref, wuk_ref,
                      h1_ref, u_ref, ckv_ref, krope_ref, kcat_ref, qlat_ref, qrope_ref):
    h1 = _ffn_half_step(x_ref[...], g1_ref, wg_ref, wu_ref, wd_ref)
    h1_ref[...] = h1
    xm = _rms(h1, gm_ref[...]).astype(BF16)
    proj = _dot(xm, win_ref[...])
    u_ref[...] = proj[:, :S5_WIDTH]
    cqn = _rms(proj[:, OFF_CQ:OFF_CKV], gq_ref[...]).astype(BF16)
    ckv = _rms(proj[:, OFF_CKV:OFF_KR], gkv_ref[...])
    ckv_ref[...] = ckv
    cos = cos_ref[...]
    sin = sin_ref[...]
    kr = proj[:, OFF_KR:OFF_KRS] * cos[:, :LANES] + proj[:, OFF_KRS:IN_EXT] * sin[:, :LANES]
    krope_ref[...] = kr[:, :QK_ROPE]
    kcat_ref[:, :KV_LORA] = ckv.astype(BF16)
    for slab in range(ROPE_LANES // LANES):
        kcat_ref[:, KV_LORA + LANES * slab:KV_LORA + LANES * (slab + 1)] = kr.astype(BF16)
    q = _dot(cqn, wuq_ref[...])
    n_nope = MLA_HEADS * QK_NOPE
    qrope_ref[...] = ((q[:, n_nope:n_nope + ROPE_LANES] * cos
                       + q[:, n_nope + ROPE_LANES:] * sin) * Q_PRESCALE).astype(BF16)
    qn = q[:, :n_nope].astype(BF16)
    for h in range(MLA_HEADS):
        pair = h // 2
        qlat_ref[:, KV_LORA * h:KV_LORA * (h + 1)] = (_dot(
            qn[:, LANES * pair:LANES * (pair + 1)], wuk_ref[h]) * Q_PRESCALE).astype(BF16)


def _pre_mixer(x, cos_tab, sin_tab, w, *, seq_major_u):
    m = x.shape[0]
    tm = TOKEN_TILE
    nsteps = m // tm
    tab_blocks = cos_tab.shape[0] // tm
    row = lambda i: (i, 0)
    tab = lambda i: (i % tab_blocks, 0)
    if seq_major_u:
        blocks_per_seq = seq_major_u // tm
        u_shape = (seq_major_u, (m // seq_major_u) * S5_WIDTH)
        u_spec = pl.BlockSpec((tm, S5_WIDTH), lambda i: (i % blocks_per_seq, i // blocks_per_seq))
    else:
        u_shape = (m, S5_WIDTH)
        u_spec = pl.BlockSpec((tm, S5_WIDTH), row)
    out_shape = (
        jax.ShapeDtypeStruct((m, D_MODEL), F32),
        jax.ShapeDtypeStruct(u_shape, F32),
        jax.ShapeDtypeStruct((m, KV_LORA), F32),
        jax.ShapeDtypeStruct((m, QK_ROPE), F32),
        jax.ShapeDtypeStruct((m, KV_LORA + ROPE_LANES), BF16),
        jax.ShapeDtypeStruct((m, LAT_LANES), BF16),
        jax.ShapeDtypeStruct((m, ROPE_LANES), BF16),
    )
    out_specs = (
        pl.BlockSpec((tm, D_MODEL), row),
        u_spec,
        pl.BlockSpec((tm, KV_LORA), row),
        pl.BlockSpec((tm, QK_ROPE), row),
        pl.BlockSpec((tm, KV_LORA + ROPE_LANES), row),
        pl.BlockSpec((tm, LAT_LANES), row),
        pl.BlockSpec((tm, ROPE_LANES), row),
    )
    weights = (w['g_ffn1'], w['wg1'], w['wu1'], w['wd1'], w['g_mix'], w['w_in_ext'],
               w['g_q'], w['g_kv'], w['w_uq_ext'], w['w_uk_pad'])
    in_specs = [pl.BlockSpec((tm, D_MODEL), row),
                pl.BlockSpec((tm, ROPE_LANES), tab),
                pl.BlockSpec((tm, ROPE_LANES), tab)] + [_resident(a.shape) for a in weights]
    return pl.pallas_call(
        _pre_mixer_kernel, out_shape=out_shape, grid=(nsteps,), in_specs=in_specs, out_specs=out_specs,
        compiler_params=pltpu.CompilerParams(dimension_semantics=("parallel",),
                                             vmem_limit_bytes=V7X_VMEM_LIMIT_BYTES),
        name="pre_mixer",
    )(x, cos_tab, sin_tab, *weights)


def _s5_readout(h_ref, rows, u, cpack_ref, d_ref, wglu_ref, bglu_ref, waout_ref):
    k = S5_COLS // S5_OUT_TILES
    tiles = []
    for n in range(S5_OUT_TILES):
        h_re = h_ref[rows, k * n:k * (n + 1)].astype(BF16)
        h_im = h_ref[rows, S5_COLS + k * n:S5_COLS + k * (n + 1)].astype(BF16)
        tiles.append(_dot(h_re, cpack_ref[n, :k]) + _dot(h_im, cpack_ref[n, k:]))
    y = jnp.concatenate(tiles, axis=1) + d_ref[...] * u
    z = jax.nn.gelu(y)
    gate = _dot(z.astype(BF16), wglu_ref[...]) + bglu_ref[...]
    return _dot((z * jax.nn.sigmoid(gate)).astype(BF16), waout_ref[...])


def _s5_drive(u, bpack_ref, xs_ref, rows):
    u_bf = u.astype(BF16)
    for j in range(2 * S5_STATE_TILES):
        slab = (j % S5_STATE_TILES) // S5_TILES_PER_SLAB
        xs_ref[rows, MXU_TILE * j:MXU_TILE * (j + 1)] = _dot(u_bf[:, LANES * slab:LANES * (slab + 1)],
                                                            bpack_ref[j])


def _s5_prompt_kernel(u_ref, order_ref, bpack_ref, are_ref, aim_ref, cpack_ref, d_ref, wglu_ref, bglu_ref,
                      waout_ref, ya_ref, sre_ref, sim_ref, xs_ref, hre_ref, him_ref):
    del order_ref
    tt, nb, _ = u_ref.shape
    tp = tt // S5_TIME_SPLIT
    rp = tp * nb

    @pl.when(pl.program_id(0) == 0)
    def _():
        hre_ref[...] = jnp.zeros_like(hre_ref)
        him_ref[...] = jnp.zeros_like(him_ref)

    u = u_ref[...].reshape(tt * nb, S5_WIDTH)
    for q in range(S5_TIME_SPLIT):
        _s5_drive(u[q * rp:(q + 1) * rp], bpack_ref, xs_ref, pl.ds(q * rp, rp))
    for q in range(S5_TIME_SPLIT):
        for cb in range(S5_COLS // S5_COL_TILE):
            re_cols = pl.ds(cb * S5_COL_TILE, S5_COL_TILE)
            im_cols = pl.ds(S5_COLS + cb * S5_COL_TILE, S5_COL_TILE)
            ar = jnp.broadcast_to(are_ref[:, re_cols], (nb, S5_COL_TILE))
            ai = jnp.broadcast_to(aim_ref[:, re_cols], (nb, S5_COL_TILE))
            hr, hi = hre_ref[:, re_cols], him_ref[:, re_cols]
            for t in range(q * tp, (q + 1) * tp):
                r = pl.ds(t * nb, nb)
                hr, hi = (ar * hr - ai * hi + xs_ref[r, re_cols], ar * hi + ai * hr + xs_ref[r, im_cols])
                xs_ref[r, re_cols] = hr
                xs_ref[r, im_cols] = hi
            hre_ref[:, re_cols] = hr
            him_ref[:, re_cols] = hi
        ya = _s5_readout(xs_ref, pl.ds(q * rp, rp), u[q * rp:(q + 1) * rp],
                         cpack_ref, d_ref, wglu_ref, bglu_ref, waout_ref)
        ya_ref[pl.ds(q * tp, tp)] = ya.reshape(tp, nb, D_MODEL)
    sre_ref[...] = hre_ref[...]
    sim_ref[...] = him_ref[...]


def _s5_prompt(u3, w, run_after):
    seq, nb, _ = u3.shape
    tt = S5_TIME_TILE
    weights = (run_after[:16, :LANES],
               w['bpack'], w['a_re'], w['a_im'], w['cpack'], w['s5_d'], w['w_glu'], w['b_glu'], w['w_a_out'])
    state = pl.BlockSpec((nb, S5_COLS), lambda i: (0, 0))
    return pl.pallas_call(
        _s5_prompt_kernel,
        out_shape=(jax.ShapeDtypeStruct((seq, nb, D_MODEL), F32),
                   jax.ShapeDtypeStruct((nb, S5_COLS), F32),
                   jax.ShapeDtypeStruct((nb, S5_COLS), F32)),
        grid=(seq // tt,),
        in_specs=[pl.BlockSpec((tt, nb, S5_WIDTH), lambda i: (i, 0, 0))] + [_resident(a.shape) for a in weights],
        out_specs=(pl.BlockSpec((tt, nb, D_MODEL), lambda i: (i, 0, 0)), state, state),
        scratch_shapes=[pltpu.VMEM((tt * nb, 2 * S5_COLS), F32),
                        pltpu.VMEM((nb, S5_COLS), F32),
                        pltpu.VMEM((nb, S5_COLS), F32)],
        compiler_params=pltpu.CompilerParams(dimension_semantics=("arbitrary",),
                                             vmem_limit_bytes=V7X_VMEM_LIMIT_BYTES),
        name="s5_prompt",
    )(u3, *weights)


def _s5_sample_kernel(u_ref, h0re_ref, h0im_ref, bpack_ref, are_ref, aim_ref, cpack_ref, d_ref, wglu_ref,
                      bglu_ref, waout_ref, ya_ref, sre_ref, sim_ref, hs_ref):
    tt, nb, _ = u_ref.shape
    u = u_ref[...].reshape(tt * nb, S5_WIDTH)
    _s5_drive(u, bpack_ref, hs_ref, pl.ds(0, tt * nb))
    ar = are_ref[...]
    ai = aim_ref[...]
    hr = h0re_ref[...]
    hi = h0im_ref[...]
    for t in range(tt):
        r = pl.ds(t * nb, nb)
        hr, hi = (ar * hr - ai * hi + hs_ref[r, :S5_COLS], ar * hi + ai * hr + hs_ref[r, S5_COLS:])
        hs_ref[r, :S5_COLS] = hr
        hs_ref[r, S5_COLS:] = hi
    sre_ref[...] = hr
    sim_ref[...] = hi
    ya = _s5_readout(hs_ref, pl.ds(0, tt * nb), u, cpack_ref, d_ref, wglu_ref, bglu_ref, waout_ref)
    ya_ref[...] = ya.reshape(tt, nb, D_MODEL)


def _s5_sample(u3, h0_re, h0_im, w):
    tt, nb, _ = u3.shape
    weights = (w['bpack'], w['a_re'], w['a_im'], w['cpack'], w['s5_d'], w['w_glu'], w['b_glu'], w['w_a_out'])
    args = (u3, h0_re, h0_im) + weights
    whole = lambda a: pl.BlockSpec(a.shape, lambda i, nd=a.ndim: (0,) * nd)
    outs = (jax.ShapeDtypeStruct((tt, nb, D_MODEL), F32),
            jax.ShapeDtypeStruct((nb, S5_COLS), F32),
            jax.ShapeDtypeStruct((nb, S5_COLS), F32))
    return pl.pallas_call(
        _s5_sample_kernel, out_shape=outs, grid=(1,),
        in_specs=[whole(a) for a in args], out_specs=tuple(whole(o) for o in outs),
        scratch_shapes=[pltpu.VMEM((tt * nb, 2 * S5_COLS), F32)],
        compiler_params=pltpu.CompilerParams(dimension_semantics=("arbitrary",),
                                             vmem_limit_bytes=V7X_VMEM_LIMIT_BYTES),
        name="s5_sample",
    )(*args)


def _lanes(stat, width):
    if width % LANES:
        return stat[:, :width]
    return jnp.tile(stat, (1, width // LANES))


def _softmax_block_update(s, v, rows, m_ref, l_ref, acc_ref, first=False):
    if first:
        m_new = jnp.broadcast_to(jnp.max(s, axis=-1, keepdims=True), (s.shape[0], LANES))
        p = jnp.exp2(s - _lanes(m_new, s.shape[1]))
        l_ref[rows, :] = jnp.broadcast_to(jnp.sum(p, axis=-1, keepdims=True), (s.shape[0], LANES))
        acc_ref[rows, :] = _dot(p.astype(BF16), v)
        m_ref[rows, :] = m_new
        return
    m_old = m_ref[rows, :]
    m_new = jnp.maximum(m_old, jnp.max(s, axis=-1, keepdims=True))
    alpha = jnp.exp2(m_old - m_new)
    p = jnp.exp2(s - _lanes(m_new, s.shape[1]))
    l_ref[rows, :] = alpha * l_ref[rows, :] + jnp.sum(p, axis=-1, keepdims=True)
    acc_ref[rows, :] = _lanes(alpha, KV_LORA) * acc_ref[rows, :] + _dot(p.astype(BF16), v)
    m_ref[rows, :] = m_new


def _attn_prompt_kernel(qlat_ref, qrope_ref, kcat_ref, o_ref, qs_ref, m_ref, l_ref, acc_ref):
    tq = ATTN_TILE
    sub = tq // ATTN_QUERY_RUNS
    qi = pl.program_id(1)
    lane_head = lax.broadcasted_iota(jnp.int32, (sub, ROPE_LANES), 1) // QK_ROPE
    for run in range(ATTN_QUERY_RUNS):
        toks = pl.ds(run * sub, sub)
        qr = qrope_ref[toks, :]
        for h in range(MLA_HEADS):
            rows = pl.ds((run * MLA_HEADS + h) * sub, sub)
            qs_ref[rows, :KV_LORA] = qlat_ref[toks, KV_LORA * h:KV_LORA * (h + 1)]
            qs_ref[rows, KV_LORA:] = jnp.where(lane_head == h, qr, jnp.zeros_like(qr))
    n = MLA_HEADS * tq // ATTN_ROW_SPLIT
    groups_per_run = ATTN_ROW_SPLIT // ATTN_QUERY_RUNS

    def block(k, causal, first=False):
        kc = kcat_ref[pl.ds(pl.multiple_of(k * tq, tq), tq), :]

        def visible(part):
            run = part // groups_per_run
            return kc[:(run + 1) * sub] if causal else kc

        def scores(part):
            s = _dot_nt(qs_ref[pl.ds(part * n, n), :], visible(part))
            if causal:
                run = part // groups_per_run
                r = (lax.broadcasted_iota(jnp.int32, s.shape, 0) & (sub - 1)) + run * sub
                c = lax.broadcasted_iota(jnp.int32, s.shape, 1)
                s = jnp.where(c <= r, s, NEG_BIG)
            return s

        ready = [scores(part) for part in range(min(ATTN_LOOKAHEAD, ATTN_ROW_SPLIT))]
        for part in range(ATTN_ROW_SPLIT):
            if part + ATTN_LOOKAHEAD < ATTN_ROW_SPLIT:
                ready.append(scores(part + ATTN_LOOKAHEAD))
            _softmax_block_update(ready.pop(0), visible(part)[:, :KV_LORA], pl.ds(part * n, n),
                                  m_ref, l_ref, acc_ref, first=first)

    def past(k, carry):
        block(k, False)
        return carry

    @pl.when(qi == 0)
    def _():
        block(0, True, first=True)

    @pl.when(qi > 0)
    def _():
        block(0, False, first=True)
        lax.fori_loop(1, qi, past, 0)
        block(qi, True)

    o = acc_ref[...] / _lanes(l_ref[...], KV_LORA)
    for run in range(ATTN_QUERY_RUNS):
        for h in range(MLA_HEADS):
            seg = run * MLA_HEADS + h
            o_ref[pl.ds(run * sub, sub), KV_LORA * h:KV_LORA * (h + 1)] = o[seg * sub:(seg + 1) * sub].astype(BF16)


def _attn_prompt(qlat, qrope, kcat, *, batch, seq):
    tq = ATTN_TILE
    nq = seq // tq
    rows = MLA_HEADS * tq
    qmap = lambda b, i: (b * nq + i, 0)
    return pl.pallas_call(
        _attn_prompt_kernel,
        out_shape=jax.ShapeDtypeStruct((batch * seq, LAT_LANES), BF16),
        grid=(batch, nq),
        in_specs=[pl.BlockSpec((tq, LAT_LANES), qmap),
                  pl.BlockSpec((tq, ROPE_LANES), qmap),
                  pl.BlockSpec((seq, KV_LORA + ROPE_LANES), lambda b, i: (b, 0))],
        out_specs=pl.BlockSpec((tq, LAT_LANES), qmap),
        scratch_shapes=[pltpu.VMEM((rows, KV_LORA + ROPE_LANES), BF16),
                        pltpu.VMEM((rows, LANES), F32),
                        pltpu.VMEM((rows, LANES), F32),
                        pltpu.VMEM((rows, KV_LORA), F32)],
        compiler_params=pltpu.CompilerParams(dimension_semantics=("parallel", "arbitrary"),
                                             vmem_limit_bytes=V7X_VMEM_LIMIT_BYTES),
        name="attn_prompt",
    )(qlat, qrope, kcat)


def _softmax_part(s, v):
    m = jnp.max(s, axis=-1, keepdims=True)
    p = jnp.exp2(s - m)
    return m, jnp.sum(p, axis=-1, keepdims=True), _dot(p.astype(BF16), v)


def _merge_softmax_parts(parts, m_ref, l_ref, acc_ref):
    m_old = m_ref[...]
    m_new = m_old
    for m, _, _ in parts:
        m_new = jnp.maximum(m_new, m)
    alpha = jnp.exp2(m_old - m_new)
    l = alpha * l_ref[...]
    acc = _lanes(alpha, KV_LORA) * acc_ref[...]
    for m, psum, pv in parts:
        wgt = jnp.exp2(m - m_new)
        l = l + wgt * psum
        acc = acc + _lanes(wgt, KV_LORA) * pv
    m_ref[...] = m_new
    l_ref[...] = l
    acc_ref[...] = acc


def _attn_sample_kernel(pt_ref, qlat_ref, qrope_ref, ckvn_ref, kropen_ref, cl_hbm, cr_hbm, o_ref,
                        kbuf, rbuf, sem, qs_ref, qr_ref, kn_ref, rn_ref, m_ref, l_ref, acc_ref,
                        *, n_pages, n_new):
    pg = SAMPLE_PAGES_PER_STEP
    chunks = n_pages // pg
    g = pl.program_id(0)
    n_steps = pl.num_programs(0)
    c = g % chunks
    last = n_steps - 1
    slot = lax.rem(g, SAMPLE_SLOTS)
    ahead = SAMPLE_SLOTS - 1

    def page_copies(step, dst_slot):
        first = (step // chunks) * n_pages + (step % chunks) * pg
        copies = []
        for p in range(pg):
            page = pt_ref[first + p]
            keys = pl.ds(p * PAGE_SIZE, PAGE_SIZE)
            copies.append(pltpu.make_async_copy(cl_hbm.at[page], kbuf.at[dst_slot, keys], sem.at[0, dst_slot]))
            copies.append(pltpu.make_async_copy(cr_hbm.at[page], rbuf.at[dst_slot, :, keys], sem.at[1, dst_slot]))
        return copies

    @pl.when(g == 0)
    def _():
        for step in range(ahead):
            for cp in page_copies(step, step):
                cp.start()

    @pl.when(c == 0)
    def _():
        ql = qlat_ref[0].astype(F32)
        qr = qrope_ref[0].astype(F32)
        for h in range(MLA_HEADS):
            rows = pl.ds(h * n_new, n_new)
            qs_ref[rows, :] = ql[:, KV_LORA * h:KV_LORA * (h + 1)]
            qr_ref[rows, :] = qr[:, QK_ROPE * h:QK_ROPE * (h + 1)]
        m_ref[...] = jnp.full_like(m_ref, NEG_BIG)
        l_ref[...] = jnp.zeros_like(l_ref)
        acc_ref[...] = jnp.zeros_like(acc_ref)

    for cp in page_copies(g, slot):
        cp.wait()

    qs = qs_ref[...].astype(BF16)
    qr = qr_ref[...].astype(BF16)
    sub = pg * PAGE_SIZE // SAMPLE_SUBCHUNKS
    def scores(j):
        keys = pl.ds(j * sub, sub)
        kc = kbuf[slot, keys, :].astype(BF16)
        return _dot_nt(qs, kc) + _dot(qr, rbuf[slot, :, keys].astype(BF16)), kc

    parts = []
    ready = [scores(j) for j in range(min(SAMPLE_LOOKAHEAD, SAMPLE_SUBCHUNKS))]
    for j in range(SAMPLE_SUBCHUNKS):
        if j + SAMPLE_LOOKAHEAD < SAMPLE_SUBCHUNKS:
            ready.append(scores(j + SAMPLE_LOOKAHEAD))
        parts.append(_softmax_part(*ready.pop(0)))
    _merge_softmax_parts(parts, m_ref, l_ref, acc_ref)

    for cp in page_copies(jnp.minimum(g + ahead, last), lax.rem(g + ahead, SAMPLE_SLOTS)):
        cp.start()

    @pl.when(g == last)
    def _():
        for extra in range(1, ahead + 1):
            for cp in page_copies(last, lax.rem(g + extra, SAMPLE_SLOTS)):
                cp.wait()

    @pl.when(c == chunks - 1)
    def _():
        kn_ref[...] = jnp.zeros_like(kn_ref)
        rn_ref[...] = jnp.zeros_like(rn_ref)
        kn_ref[pl.ds(0, n_new), :] = ckvn_ref[0]
        rn_ref[pl.ds(0, n_new), :] = kropen_ref[0]
        kn = kn_ref[...].astype(BF16)
        sn = _dot_nt(qs, kn) + _dot_nt(qr, rn_ref[...].astype(BF16))
        t = lax.broadcasted_iota(jnp.int32, sn.shape, 0) % n_new
        j = lax.broadcasted_iota(jnp.int32, sn.shape, 1)
        sn = jnp.where(j <= t, sn, NEG_BIG)
        _merge_softmax_parts([_softmax_part(sn, kn)], m_ref, l_ref, acc_ref)
        o_ref[0] = acc_ref[...] / _lanes(l_ref[...], KV_LORA)


def _attn_sample(page_table, qlat, qrope, ckv_new, krope_new, cache_latent, cache_k_rope):
    nb, n_pages = page_table.shape
    n_new = qlat.shape[0] // nb
    pg = SAMPLE_PAGES_PER_STEP
    chunks = n_pages // pg
    rows = MLA_HEADS * n_new
    new_pad = 8
    per_batch = lambda g, pt: (g // chunks, 0, 0)
    grid_spec = pltpu.PrefetchScalarGridSpec(
        num_scalar_prefetch=1,
        grid=(nb * chunks,),
        in_specs=[pl.BlockSpec((1, n_new, LAT_LANES), per_batch),
                  pl.BlockSpec((1, n_new, ROPE_LANES), per_batch),
                  pl.BlockSpec((1, n_new, KV_LORA), per_batch),
                  pl.BlockSpec((1, n_new, QK_ROPE), per_batch),
                  pl.BlockSpec(memory_space=pl.ANY),
                  pl.BlockSpec(memory_space=pl.ANY)],
        out_specs=pl.BlockSpec((1, rows, KV_LORA), per_batch),
        scratch_shapes=[pltpu.VMEM((SAMPLE_SLOTS, pg * PAGE_SIZE, KV_LORA), F32),
                        pltpu.VMEM((SAMPLE_SLOTS, QK_ROPE, pg * PAGE_SIZE), F32),
                        pltpu.SemaphoreType.DMA((2, SAMPLE_SLOTS)),
                        pltpu.VMEM((rows, KV_LORA), F32),
                        pltpu.VMEM((rows, QK_ROPE), F32),
                        pltpu.VMEM((new_pad, KV_LORA), F32),
                        pltpu.VMEM((new_pad, QK_ROPE), F32),
                        pltpu.VMEM((rows, LANES), F32),
                        pltpu.VMEM((rows, LANES), F32),
                        pltpu.VMEM((rows, KV_LORA), F32)])
    return pl.pallas_call(
        functools.partial(_attn_sample_kernel, n_pages=n_pages, n_new=n_new),
        out_shape=jax.ShapeDtypeStruct((nb, rows, KV_LORA), F32),
        grid_spec=grid_spec,
        compiler_params=pltpu.CompilerParams(dimension_semantics=("arbitrary",),
                                             vmem_limit_bytes=V7X_VMEM_LIMIT_BYTES),
        name="attn_sample",
    )(page_table.reshape(-1), qlat.reshape(nb, n_new, LAT_LANES), qrope.reshape(nb, n_new, ROPE_LANES),
      ckv_new.reshape(nb, n_new, KV_LORA), krope_new.reshape(nb, n_new, QK_ROPE), cache_latent, cache_k_rope)


def _post_mixer_kernel(h1_ref, ya_ref, ol_ref, gm_ref, wgt_ref, wuv_ref, wbo_ref, wout_ref,
                       g2_ref, wg_ref, wu_ref, wd_ref, gf_ref, y_ref):
    h1 = h1_ref[...]
    xm = _rms(h1, gm_ref[...]).astype(BF16)
    gates = _dot(xm, wgt_ref[...])
    ol = ol_ref[...]
    k = UV_HEADS_PER_TILE * KV_LORA
    o = jnp.concatenate([_dot(ol[:, k * j:k * (j + 1)], wuv_ref[j])
                         for j in range(MLA_HEADS // UV_HEADS_PER_TILE)], axis=1)
    yb = _dot(o.astype(BF16), wbo_ref[...])
    merged = jax.nn.sigmoid(gates[:, :D_MODEL]) * ya_ref[...] + jax.nn.sigmoid(gates[:, D_MODEL:]) * yb
    h2 = h1 + _dot(merged.astype(BF16), wout_ref[...])
    h3 = _ffn_half_step(h2, g2_ref, wg_ref, wu_ref, wd_ref)
    y_ref[...] = _rms(h3, gf_ref[...])


def _post_mixer(h1, ya, olat, w, *, seq_major_ya):
    m = h1.shape[0]
    tm = TOKEN_TILE
    row = lambda i: (i, 0)
    if seq_major_ya:
        blocks_per_seq = seq_major_ya // tm
        ya_spec = pl.BlockSpec((tm, D_MODEL), lambda i: (i % blocks_per_seq, i // blocks_per_seq))
    else:
        ya_spec = pl.BlockSpec((tm, D_MODEL), row)
    weights = (w['g_mix'], w['w_gates'], w['w_uv_pad'], w['w_b_out'], w['w_out'],
               w['g_ffn2'], w['wg2'], w['wu2'], w['wd2'], w['g_final'])
    return pl.pallas_call(
        _post_mixer_kernel,
        out_shape=jax.ShapeDtypeStruct((m, D_MODEL), F32),
        grid=(m // tm,),
        in_specs=[pl.BlockSpec((tm, D_MODEL), row), ya_spec, pl.BlockSpec((tm, LAT_LANES), row)]
        + [_resident(a.shape) for a in weights],
        out_specs=pl.BlockSpec((tm, D_MODEL), row),
        compiler_params=pltpu.CompilerParams(dimension_semantics=("parallel",),
                                             vmem_limit_bytes=V7X_VMEM_LIMIT_BYTES),
        name="post_mixer",
    )(h1, ya, olat, *weights)


def _rope_tables(pos):
    inv_freq = 1.0 / (ROPE_BASE ** (jnp.arange(0, QK_ROPE, 2, dtype=F32) / QK_ROPE))
    ang = pos.astype(F32)[:, None] * inv_freq[None, :]
    cos, sin = jnp.cos(ang), jnp.sin(ang)
    cos_tab = jnp.tile(jnp.concatenate([cos, cos], axis=-1), (1, MLA_HEADS))
    sin_tab = jnp.tile(jnp.concatenate([-sin, sin], axis=-1), (1, MLA_HEADS))
    return cos_tab, sin_tab


def _swap_halves(a):
    half = QK_ROPE // 2
    return jnp.concatenate([a[..., half:], a[..., :half]], axis=-1)


def _block_diag_runs(blocks, run):
    g, r, c = blocks.shape
    eye = jnp.eye(run, dtype=blocks.dtype)
    tiles = jnp.einsum('tgrc,gk->tgrkc', blocks.reshape(g // run, run, r, c), eye)
    return tiles.reshape(g // run, run * r, run * c)


def _prepare_weights(p):
    w = {}
    vec = lambda a: a.reshape(1, -1).astype(F32)
    for name in ('g_ffn1', 'g_mix', 'g_q', 'g_kv', 'g_ffn2', 'g_final', 'b_glu'):
        w[name] = vec(p[name])
    for src, dst in (('w_ffn1_gate', 'wg1'), ('w_ffn1_up', 'wu1'), ('w_ffn1_down', 'wd1'),
                     ('w_ffn2_gate', 'wg2'), ('w_ffn2_up', 'wu2'), ('w_ffn2_down', 'wd2'),
                     ('w_glu', 'w_glu'), ('w_a_out', 'w_a_out'), ('w_b_out', 'w_b_out'), ('w_out', 'w_out')):
        w[dst] = p[src].astype(BF16)
    w_in = p['w_in']
    off_kr = S5_WIDTH + Q_LORA + KV_LORA
    k_r = w_in[:, off_kr:off_kr + QK_ROPE]
    reps = LANES // QK_ROPE
    w['w_in_ext'] = jnp.concatenate(
        [w_in[:, :off_kr], jnp.tile(k_r, (1, reps)), jnp.tile(_swap_halves(k_r), (1, reps))],
        axis=1).astype(BF16)
    w['w_gates'] = w_in[:, off_kr + QK_ROPE:].astype(BF16)
    w_uq = p['w_uq']
    uq_rope = w_uq[:, :, QK_NOPE:]
    w['w_uq_ext'] = jnp.concatenate(
        [w_uq[:, :, :QK_NOPE].reshape(Q_LORA, -1), uq_rope.reshape(Q_LORA, -1),
         _swap_halves(uq_rope).reshape(Q_LORA, -1)], axis=1).astype(BF16)
    uk = jnp.transpose(p['w_uk'], (1, 2, 0))
    uk_pad = jnp.zeros((MLA_HEADS, 2, QK_NOPE, KV_LORA), F32)
    uk_pad = uk_pad.at[jnp.arange(MLA_HEADS), jnp.arange(MLA_HEADS) % 2].set(uk)
    w['w_uk_pad'] = uk_pad.reshape(MLA_HEADS, 2 * QK_NOPE, KV_LORA).astype(BF16)
    uv = jnp.transpose(p['w_uv'], (1, 0, 2))
    w['w_uv_pad'] = _block_diag_runs(uv, UV_HEADS_PER_TILE).astype(BF16)
    lam_re, lam_im = p['s5_a_re'].astype(F32), p['s5_a_im'].astype(F32)
    dt = jnp.exp(p['s5_log_dt'].astype(F32))[:, None]
    mag = jnp.exp(lam_re * dt)
    ab_re, ab_im = mag * jnp.cos(lam_im * dt), mag * jnp.sin(lam_im * dt)
    den = lam_re * lam_re + lam_im * lam_im
    num_re, num_im = ab_re - 1.0, ab_im
    k_re = (num_re * lam_re + num_im * lam_im) / den
    k_im = (num_im * lam_re - num_re * lam_im) / den
    w['a_re'] = ab_re.reshape(1, S5_COLS)
    w['a_im'] = ab_im.reshape(1, S5_COLS)
    b_re, b_im = p['s5_b_re'].astype(F32), p['s5_b_im'].astype(F32)
    kb_re = k_re[..., None] * b_re - k_im[..., None] * b_im
    kb_im = k_re[..., None] * b_im + k_im[..., None] * b_re

    def to_in(a):
        tiles = _block_diag_runs(jnp.transpose(a, (0, 2, 1)), S5_GROUPS_PER_STATE_TILE)
        n, r, c = tiles.shape
        pos = jnp.arange(n) % S5_TILES_PER_SLAB
        slab = jnp.zeros((n, S5_TILES_PER_SLAB, r, c), F32).at[jnp.arange(n), pos].set(tiles)
        return slab.reshape(n, S5_TILES_PER_SLAB * r, c)

    w['bpack'] = jnp.concatenate([to_in(kb_re), to_in(kb_im)], axis=0).astype(BF16)
    to_out = lambda a: _block_diag_runs(jnp.transpose(a, (0, 2, 1)), S5_GROUPS_PER_OUT_TILE)
    w['cpack'] = jnp.concatenate([to_out(p['s5_c_re'].astype(F32)), -to_out(p['s5_c_im'].astype(F32))],
                                 axis=1).astype(BF16)
    w['s5_d'] = vec(p['s5_d'])
    return w


def kernel(x_prompt, x_sample, cache_latent, cache_k_rope, state_ssm_re, state_ssm_im, page_table, g_ffn1, w_ffn1_gate, w_ffn1_up, w_ffn1_down, g_mix, w_in, s5_a_re, s5_a_im, s5_log_dt, s5_b_re, s5_b_im, s5_c_re, s5_c_im, s5_d, w_glu, b_glu, w_a_out, g_q, w_uq, g_kv, w_uk, w_uv, w_b_out, w_out, g_ffn2, w_ffn2_gate, w_ffn2_up, w_ffn2_down, g_final):
    layer = 0
    params = dict(
        g_ffn1=g_ffn1, w_ffn1_gate=w_ffn1_gate, w_ffn1_up=w_ffn1_up, w_ffn1_down=w_ffn1_down,
        g_mix=g_mix, w_in=w_in, s5_a_re=s5_a_re, s5_a_im=s5_a_im, s5_log_dt=s5_log_dt,
        s5_b_re=s5_b_re, s5_b_im=s5_b_im, s5_c_re=s5_c_re, s5_c_im=s5_c_im, s5_d=s5_d,
        w_glu=w_glu, b_glu=b_glu, w_a_out=w_a_out, g_q=g_q, w_uq=w_uq, g_kv=g_kv, w_uk=w_uk,
        w_uv=w_uv, w_b_out=w_b_out, w_out=w_out, g_ffn2=g_ffn2, w_ffn2_gate=w_ffn2_gate,
        w_ffn2_up=w_ffn2_up, w_ffn2_down=w_ffn2_down)
    p = {k: v[layer] for k, v in params.items()}
    p['g_final'] = g_final
    w = _prepare_weights(p)

    nb, seq, _ = x_prompt.shape
    cos_p, sin_p = _rope_tables(jnp.arange(seq))
    h1, u, ckv, krope, kcat, qlat, qrope = _pre_mixer(
        x_prompt.reshape(nb * seq, D_MODEL), cos_p, sin_p, w, seq_major_u=seq)
    olat = _attn_prompt(qlat, qrope, kcat, batch=nb, seq=seq)
    ya, sre_p, sim_p = _s5_prompt(u.reshape(seq, nb, S5_WIDTH), w, run_after=olat)
    y_prompt = _post_mixer(h1, ya.reshape(seq, nb * D_MODEL), olat, w, seq_major_ya=seq).reshape(nb, seq, D_MODEL)
    ckv_prompt = ckv.reshape(1, nb, seq, KV_LORA)
    krope_prompt = krope.reshape(1, nb, seq, QK_ROPE)
    group = lambda a: a.reshape(1, a.shape[0], S5_GROUPS, S5_STATE)

    sb, n_new, _ = x_sample.shape
    n_pages = page_table.shape[1]
    past_len = n_pages * PAGE_SIZE
    cos_s, sin_s = _rope_tables(past_len + jnp.arange(n_new))
    cos_s, sin_s = jnp.tile(cos_s, (sb, 1)), jnp.tile(sin_s, (sb, 1))
    h1s, us, ckvs, kropes, _, qlats, qropes = _pre_mixer(
        x_sample.reshape(sb * n_new, D_MODEL), cos_s, sin_s, w, seq_major_u=0)
    us_t = jnp.transpose(us.reshape(sb, n_new, S5_WIDTH), (1, 0, 2))
    yas_t, sre_s, sim_s = _s5_sample(us_t, state_ssm_re[layer].reshape(sb, S5_COLS),
                                     state_ssm_im[layer].reshape(sb, S5_COLS), w)
    yas = jnp.transpose(yas_t, (1, 0, 2)).reshape(sb * n_new, D_MODEL)
    rope_pages = jnp.swapaxes(cache_k_rope[layer], 1, 2)
    ol = _attn_sample(page_table, qlats, qropes, ckvs, kropes, cache_latent[layer], rope_pages)
    olats = jnp.transpose(ol.reshape(sb, MLA_HEADS, n_new, KV_LORA), (0, 2, 1, 3))
    olats = olats.reshape(sb * n_new, LAT_LANES).astype(BF16)
    y_sample = _post_mixer(h1s, yas, olats, w, seq_major_ya=0).reshape(sb, n_new, D_MODEL)

    return (y_prompt, y_sample, ckv_prompt, krope_prompt,
            ckvs.reshape(1, sb, n_new, KV_LORA), kropes.reshape(1, sb, n_new, QK_ROPE),
            group(sre_p), group(sim_p), group(sre_s), group(sim_s))
```

```python
import functools

import jax
import jax.numpy as jnp
from jax import lax
from jax.experimental import pallas as pl
from jax.experimental.pallas import tpu as pltpu

F32 = jnp.float32
BF16 = jnp.bfloat16

D_MODEL = 1024
D_FF = 2816
S5_WIDTH = 512
S5_GROUP = 16
S5_GROUPS = 32
S5_STATE = 64
S5_COLS = S5_GROUPS * S5_STATE
MLA_HEADS = 8
QK_NOPE = 64
QK_ROPE = 32
V_DIM = 64
Q_LORA = 384
KV_LORA = 256
ROPE_BASE = 10000.0
NORM_EPS = 1e-6
PAGE_SIZE = 128
ATTN_SCALE = (QK_NOPE + QK_ROPE) ** -0.5
Q_PRESCALE = ATTN_SCALE * 1.4426950408889634
LANES = 128
MXU_TILE = 256
S5_STATE_TILES = S5_COLS // MXU_TILE
S5_GROUPS_PER_STATE_TILE = MXU_TILE // S5_STATE
S5_TILES_PER_SLAB = LANES // (S5_GROUPS_PER_STATE_TILE * S5_GROUP)
UV_HEADS_PER_TILE = MXU_TILE // V_DIM
S5_OUT_TILES = S5_WIDTH // MXU_TILE
S5_GROUPS_PER_OUT_TILE = MXU_TILE // S5_GROUP
ROPE_LANES = MLA_HEADS * QK_ROPE
LAT_LANES = MLA_HEADS * KV_LORA
IN_EXT = S5_WIDTH + Q_LORA + KV_LORA + LANES
OFF_CQ = S5_WIDTH
OFF_CKV = OFF_CQ + Q_LORA
OFF_KR = OFF_CKV + KV_LORA
NEG_BIG = -1e30

V7X_VMEM_LIMIT_BYTES = 56 * 1024 * 1024
TOKEN_TILE = 512
S5_TIME_TILE = 64
S5_TIME_SPLIT = 2
S5_COL_TILE = 512
ATTN_TILE = 512
ATTN_ROW_SPLIT = 8
ATTN_QUERY_RUNS = 2
ATTN_LOOKAHEAD = 2
SAMPLE_LOOKAHEAD = 3
SAMPLE_PAGES_PER_STEP = 64
SAMPLE_SUBCHUNKS = 8
SAMPLE_SLOTS = 3


def _rms(x, g):
    return x * lax.rsqrt(jnp.mean(x * x, axis=-1, keepdims=True) + NORM_EPS) * g


def _dot(a, b):
    return jnp.dot(a, b, preferred_element_type=F32)


def _dot_nt(a, b):
    return lax.dot_general(a, b, (((1,), (1,)), ((), ())), preferred_element_type=F32)


def _ffn_half_step(x, g_ref, wg_ref, wu_ref, wd_ref):
    xn = _rms(x, g_ref[...]).astype(BF16)
    gate = _dot(xn, wg_ref[...])
    up = _dot(xn, wu_ref[...])
    hid = (jax.nn.silu(gate) * up).astype(BF16)
    return x + 0.5 * _dot(hid, wd_ref[...])


def _resident(shape):
    nd = len(shape)
    return pl.BlockSpec(shape, lambda *_: (0,) * nd, pipeline_mode=pl.Buffered(1))


def _pre_mixer_kernel(x_ref, cos_ref, sin_ref, g1_ref, wg_ref, wu_ref, wd_ref, gm_ref, win_ref,
                      gq_ref, gkv_ref, wuq_ref, wuk_ref,
                      h1_ref, u_ref, ckv_ref, krope_ref, kcat_ref, qlat_ref, qrope_ref):
    h1 = _ffn_half_step(x_ref[...], g1_ref, wg_ref, wu_ref, wd_ref)
    h1_ref[...] = h1
    xm = _rms(h1, gm_ref[...]).astype(BF16)
    proj = _dot(xm, win_ref[...])
    u_ref[...] = proj[:, :S5_WIDTH]
    cqn = _rms(proj[:, OFF_CQ:OFF_CKV], gq_ref[...]).astype(BF16)
    ckv = _rms(proj[:, OFF_CKV:OFF_KR], gkv_ref[...])
    ckv_ref[...] = ckv
    cos = cos_ref[...]
    sin = sin_ref[...]
    half = LANES // 2
    lane = lax.broadcasted_iota(jnp.int32, (proj.shape[0], LANES), 1)
    prod = proj[:, OFF_KR:IN_EXT] * jnp.where(lane < half, cos[:, :LANES], sin[:, :LANES])
    kr = prod + pltpu.roll(prod, half, axis=1)
    krope_ref[...] = kr[:, :QK_ROPE]
    kcat_ref[:, :KV_LORA] = ckv.astype(BF16)
    for slab in range(ROPE_LANES // LANES):
        kcat_ref[:, KV_LORA + LANES * slab:KV_LORA + LANES * (slab + 1)] = kr.astype(BF16)
    q = _dot(cqn, wuq_ref[...])
    n_nope = MLA_HEADS * QK_NOPE
    qrope_ref[...] = ((q[:, n_nope:n_nope + ROPE_LANES] * cos
                       + q[:, n_nope + ROPE_LANES:] * sin) * Q_PRESCALE).astype(BF16)
    qn = q[:, :n_nope].astype(BF16)
    for h in range(MLA_HEADS):
        pair = h // 2
        qlat_ref[:, KV_LORA * h:KV_LORA * (h + 1)] = (_dot(
            qn[:, LANES * pair:LANES * (pair + 1)], wuk_ref[h]) * Q_PRESCALE).astype(BF16)


def _pre_mixer(x, cos_tab, sin_tab, w, *, seq_major_u):
    m = x.shape[0]
    tm = TOKEN_TILE
    nsteps = m // tm
    tab_blocks = cos_tab.shape[0] // tm
    row = lambda i: (i, 0)
    tab = lambda i: (i % tab_blocks, 0)
    if seq_major_u:
        blocks_per_seq = seq_major_u // tm
        u_shape = (seq_major_u, (m // seq_major_u) * S5_WIDTH)
        u_spec = pl.BlockSpec((tm, S5_WIDTH), lambda i: (i % blocks_per_seq, i // blocks_per_seq))
    else:
        u_shape = (m, S5_WIDTH)
        u_spec = pl.BlockSpec((tm, S5_WIDTH), row)
    out_shape = (
        jax.ShapeDtypeStruct((m, D_MODEL), F32),
        jax.ShapeDtypeStruct(u_shape, F32),
        jax.ShapeDtypeStruct((m, KV_LORA), F32),
        jax.ShapeDtypeStruct((m, QK_ROPE), F32),
        jax.ShapeDtypeStruct((m, KV_LORA + ROPE_LANES), BF16),
        jax.ShapeDtypeStruct((m, LAT_LANES), BF16),
        jax.ShapeDtypeStruct((m, ROPE_LANES), BF16),
    )
    out_specs = (
        pl.BlockSpec((tm, D_MODEL), row),
        u_spec,
        pl.BlockSpec((tm, KV_LORA), row),
        pl.BlockSpec((tm, QK_ROPE), row),
        pl.BlockSpec((tm, KV_LORA + ROPE_LANES), row),
        pl.BlockSpec((tm, LAT_LANES), row),
        pl.BlockSpec((tm, ROPE_LANES), row),
    )
    weights = (w['g_ffn1'], w['wg1'], w['wu1'], w['wd1'], w['g_mix'], w['w_in_ext'],
               w['g_q'], w['g_kv'], w['w_uq_ext'], w['w_uk_pad'])
    in_specs = [pl.BlockSpec((tm, D_MODEL), row),
                pl.BlockSpec((tm, ROPE_LANES), tab),
                pl.BlockSpec((tm, ROPE_LANES), tab)] + [_resident(a.shape) for a in weights]
    return pl.pallas_call(
        _pre_mixer_kernel, out_shape=out_shape, grid=(nsteps,), in_specs=in_specs, out_specs=out_specs,
        compiler_params=pltpu.CompilerParams(dimension_semantics=("parallel",),
                                             vmem_limit_bytes=V7X_VMEM_LIMIT_BYTES),
        name="pre_mixer",
    )(x, cos_tab, sin_tab, *weights)


def _s5_readout(h_ref, rows, u, cpack_ref, d_ref, wglu_ref, bglu_ref, waout_ref):
    k = S5_COLS // S5_OUT_TILES
    tiles = []
    for n in range(S5_OUT_TILES):
        h_re = h_ref[rows, k * n:k * (n + 1)].astype(BF16)
        h_im = h_ref[rows, S5_COLS + k * n:S5_COLS + k * (n + 1)].astype(BF16)
        tiles.append(_dot(h_re, cpack_ref[n, :k]) + _dot(h_im, cpack_ref[n, k:]))
    y = jnp.concatenate(tiles, axis=1) + d_ref[...] * u
    z = jax.nn.gelu(y)
    gate = _dot(z.astype(BF16), wglu_ref[...]) + bglu_ref[...]
    return _dot((z * jax.nn.sigmoid(gate)).astype(BF16), waout_ref[...])


def _s5_drive(u, bpack_ref, xs_ref, rows):
    u_bf = u.astype(BF16)
    for j in range(2 * S5_STATE_TILES):
        slab = (j % S5_STATE_TILES) // S5_TILES_PER_SLAB
        xs_ref[rows, MXU_TILE * j:MXU_TILE * (j + 1)] = _dot(u_bf[:, LANES * slab:LANES * (slab + 1)],
                                                            bpack_ref[j])


def _s5_prompt_kernel(u_ref, order_ref, bpack_ref, are_ref, aim_ref, cpack_ref, d_ref, wglu_ref, bglu_ref,
                      waout_ref, ya_ref, sre_ref, sim_ref, xs_ref, hre_ref, him_ref):
    del order_ref
    tt, nb, _ = u_ref.shape
    tp = tt // S5_TIME_SPLIT
    rp = tp * nb

    @pl.when(pl.program_id(0) == 0)
    def _():
        hre_ref[...] = jnp.zeros_like(hre_ref)
        him_ref[...] = jnp.zeros_like(him_ref)

    u = u_ref[...].reshape(tt * nb, S5_WIDTH)
    for q in range(S5_TIME_SPLIT):
        _s5_drive(u[q * rp:(q + 1) * rp], bpack_ref, xs_ref, pl.ds(q * rp, rp))
    for q in range(S5_TIME_SPLIT):
        for cb in range(S5_COLS // S5_COL_TILE):
            re_cols = pl.ds(cb * S5_COL_TILE, S5_COL_TILE)
            im_cols = pl.ds(S5_COLS + cb * S5_COL_TILE, S5_COL_TILE)
            ar = jnp.broadcast_to(are_ref[:, re_cols], (nb, S5_COL_TILE))
            ai = jnp.broadcast_to(aim_ref[:, re_cols], (nb, S5_COL_TILE))
            hr, hi = hre_ref[:, re_cols], him_ref[:, re_cols]
            for t in range(q * tp, (q + 1) * tp):
                r = pl.ds(t * nb, nb)
                hr, hi = (ar * hr - ai * hi + xs_ref[r, re_cols], ar * hi + ai * hr + xs_ref[r, im_cols])
                xs_ref[r, re_cols] = hr
                xs_ref[r, im_cols] = hi
            hre_ref[:, re_cols] = hr
            him_ref[:, re_cols] = hi
        ya = _s5_readout(xs_ref, pl.ds(q * rp, rp), u[q * rp:(q + 1) * rp],
                         cpack_ref, d_ref, wglu_ref, bglu_ref, waout_ref)
        ya_ref[pl.ds(q * tp, tp)] = ya.reshape(tp, nb, D_MODEL)
    sre_ref[...] = hre_ref[...]
    sim_ref[...] = him_ref[...]


def _s5_prompt(u3, w, run_after):
    seq, nb, _ = u3.shape
    tt = S5_TIME_TILE
    weights = (run_after[:16, :LANES],
               w['bpack'], w['a_re'], w['a_im'], w['cpack'], w['s5_d'], w['w_glu'], w['b_glu'], w['w_a_out'])
    state = pl.BlockSpec((nb, S5_COLS), lambda i: (0, 0))
    return pl.pallas_call(
        _s5_prompt_kernel,
        out_shape=(jax.ShapeDtypeStruct((seq, nb, D_MODEL), F32),
                   jax.ShapeDtypeStruct((nb, S5_COLS), F32),
                   jax.ShapeDtypeStruct((nb, S5_COLS), F32)),
        grid=(seq // tt,),
        in_specs=[pl.BlockSpec((tt, nb, S5_WIDTH), lambda i: (i, 0, 0))] + [_resident(a.shape) for a in weights],
        out_specs=(pl.BlockSpec((tt, nb, D_MODEL), lambda i: (i, 0, 0)), state, state),
        scratch_shapes=[pltpu.VMEM((tt * nb, 2 * S5_COLS), F32),
                        pltpu.VMEM((nb, S5_COLS), F32),
                        pltpu.VMEM((nb, S5_COLS), F32)],
        compiler_params=pltpu.CompilerParams(dimension_semantics=("arbitrary",),
                                             vmem_limit_bytes=V7X_VMEM_LIMIT_BYTES),
        name="s5_prompt",
    )(u3, *weights)


def _s5_sample_kernel(u_ref, h0re_ref, h0im_ref, bpack_ref, are_ref, aim_ref, cpack_ref, d_ref, wglu_ref,
                      bglu_ref, waout_ref, ya_ref, sre_ref, sim_ref, hs_ref):
    tt, nb, _ = u_ref.shape
    u = u_ref[...].reshape(tt * nb, S5_WIDTH)
    _s5_drive(u, bpack_ref, hs_ref, pl.ds(0, tt * nb))
    ar = are_ref[...]
    ai = aim_ref[...]
    hr = h0re_ref[...]
    hi = h0im_ref[...]
    for t in range(tt):
        r = pl.ds(t * nb, nb)
        hr, hi = (ar * hr - ai * hi + hs_ref[r, :S5_COLS], ar * hi + ai * hr + hs_ref[r, S5_COLS:])
        hs_ref[r, :S5_COLS] = hr
        hs_ref[r, S5_COLS:] = hi
    sre_ref[...] = hr
    sim_ref[...] = hi
    ya = _s5_readout(hs_ref, pl.ds(0, tt * nb), u, cpack_ref, d_ref, wglu_ref, bglu_ref, waout_ref)
    ya_ref[...] = ya.reshape(tt, nb, D_MODEL)


def _s5_sample(u3, h0_re, h0_im, w):
    tt, nb, _ = u3.shape
    weights = (w['bpack'], w['a_re'], w['a_im'], w['cpack'], w['s5_d'], w['w_glu'], w['b_glu'], w['w_a_out'])
    args = (u3, h0_re, h0_im) + weights
    whole = lambda a: pl.BlockSpec(a.shape, lambda i, nd=a.ndim: (0,) * nd)
    outs = (jax.ShapeDtypeStruct((tt, nb, D_MODEL), F32),
            jax.ShapeDtypeStruct((nb, S5_COLS), F32),
            jax.ShapeDtypeStruct((nb, S5_COLS), F32))
    return pl.pallas_call(
        _s5_sample_kernel, out_shape=outs, grid=(1,),
        in_specs=[whole(a) for a in args], out_specs=tuple(whole(o) for o in outs),
        scratch_shapes=[pltpu.VMEM((tt * nb, 2 * S5_COLS), F32)],
        compiler_params=pltpu.CompilerParams(dimension_semantics=("arbitrary",),
                                             vmem_limit_bytes=V7X_VMEM_LIMIT_BYTES),
        name="s5_sample",
    )(*args)


def _lanes(stat, width):
    if width % LANES:
        return stat[:, :width]
    return jnp.tile(stat, (1, width // LANES))


def _softmax_block_update(s, v, rows, m_ref, l_ref, acc_ref, first=False):
    if first:
        m_new = jnp.broadcast_to(jnp.max(s, axis=-1, keepdims=True), (s.shape[0], LANES))
        p = jnp.exp2(s - _lanes(m_new, s.shape[1]))
        l_ref[rows, :] = jnp.broadcast_to(jnp.sum(p, axis=-1, keepdims=True), (s.shape[0], LANES))
        acc_ref[rows, :] = _dot(p.astype(BF16), v)
        m_ref[rows, :] = m_new
        return
    m_old = m_ref[rows, :]
    m_new = jnp.maximum(m_old, jnp.max(s, axis=-1, keepdims=True))
    alpha = jnp.exp2(m_old - m_new)
    p = jnp.exp2(s - _lanes(m_new, s.shape[1]))
    l_ref[rows, :] = alpha * l_ref[rows, :] + jnp.sum(p, axis=-1, keepdims=True)
    acc_ref[rows, :] = _lanes(alpha, KV_LORA) * acc_ref[rows, :] + _dot(p.astype(BF16), v)
    m_ref[rows, :] = m_new


def _attn_prompt_kernel(qlat_ref, qrope_ref, kcat_ref, o_ref, qs_ref, m_ref, l_ref, acc_ref):
    tq = ATTN_TILE
    sub = tq // ATTN_QUERY_RUNS
    qi = pl.program_id(1)
    lane_head = lax.broadcasted_iota(jnp.int32, (sub, ROPE_LANES), 1) // QK_ROPE
    for run in range(ATTN_QUERY_RUNS):
        toks = pl.ds(run * sub, sub)
        qr = qrope_ref[toks, :]
        for h in range(MLA_HEADS):
            rows = pl.ds((run * MLA_HEADS + h) * sub, sub)
            qs_ref[rows, :KV_LORA] = qlat_ref[toks, KV_LORA * h:KV_LORA * (h + 1)]
            qs_ref[rows, KV_LORA:] = jnp.where(lane_head == h, qr, jnp.zeros_like(qr))
    n = MLA_HEADS * tq // ATTN_ROW_SPLIT
    groups_per_run = ATTN_ROW_SPLIT // ATTN_QUERY_RUNS

    def block(k, causal, first=False):
        kc = kcat_ref[pl.ds(pl.multiple_of(k * tq, tq), tq), :]

        def visible(part):
            run = part // groups_per_run
            return kc[:(run + 1) * sub] if causal else kc

        def scores(part):
            s = _dot_nt(qs_ref[pl.ds(part * n, n), :], visible(part))
            if causal:
                run = part // groups_per_run
                r = (lax.broadcasted_iota(jnp.int32, s.shape, 0) & (sub - 1)) + run * sub
                c = lax.broadcasted_iota(jnp.int32, s.shape, 1)
                s = jnp.where(c <= r, s, NEG_BIG)
            return s

        ready = [scores(part) for part in range(min(ATTN_LOOKAHEAD, ATTN_ROW_SPLIT))]
        for part in range(ATTN_ROW_SPLIT):
            if part + ATTN_LOOKAHEAD < ATTN_ROW_SPLIT:
                ready.append(scores(part + ATTN_LOOKAHEAD))
            _softmax_block_update(ready.pop(0), visible(part)[:, :KV_LORA], pl.ds(part * n, n),
                                  m_ref, l_ref, acc_ref, first=first)

    def past(k, carry):
        block(k, False)
        return carry

    @pl.when(qi == 0)
    def _():
        block(0, True, first=True)

    @pl.when(qi > 0)
    def _():
        block(0, False, first=True)
        lax.fori_loop(1, qi, past, 0)
        block(qi, True)

    o = acc_ref[...] / _lanes(l_ref[...], KV_LORA)
    for run in range(ATTN_QUERY_RUNS):
        for h in range(MLA_HEADS):
            seg = run * MLA_HEADS + h
            o_ref[pl.ds(run * sub, sub), KV_LORA * h:KV_LORA * (h + 1)] = o[seg * sub:(seg + 1) * sub].astype(BF16)


def _attn_prompt(qlat, qrope, kcat, *, batch, seq):
    tq = ATTN_TILE
    nq = seq // tq
    rows = MLA_HEADS * tq
    qmap = lambda b, i: (b * nq + i, 0)
    return pl.pallas_call(
        _attn_prompt_kernel,
        out_shape=jax.ShapeDtypeStruct((batch * seq, LAT_LANES), BF16),
        grid=(batch, nq),
        in_specs=[pl.BlockSpec((tq, LAT_LANES), qmap),
                  pl.BlockSpec((tq, ROPE_LANES), qmap),
                  pl.BlockSpec((seq, KV_LORA + ROPE_LANES), lambda b, i: (b, 0))],
        out_specs=pl.BlockSpec((tq, LAT_LANES), qmap),
        scratch_shapes=[pltpu.VMEM((rows, KV_LORA + ROPE_LANES), BF16),
                        pltpu.VMEM((rows, LANES), F32),
                        pltpu.VMEM((rows, LANES), F32),
                        pltpu.VMEM((rows, KV_LORA), F32)],
        compiler_params=pltpu.CompilerParams(dimension_semantics=("parallel", "arbitrary"),
                                             vmem_limit_bytes=V7X_VMEM_LIMIT_BYTES),
        name="attn_prompt",
    )(qlat, qrope, kcat)


def _softmax_part(s, v):
    m = jnp.max(s, axis=-1, keepdims=True)
    p = jnp.exp2(s - m)
    return m, jnp.sum(p, axis=-1, keepdims=True), _dot(p.astype(BF16), v)


def _merge_softmax_parts(parts, m_ref, l_ref, acc_ref):
    m_old = m_ref[...]
    m_new = m_old
    for m, _, _ in parts:
        m_new = jnp.maximum(m_new, m)
    alpha = jnp.exp2(m_old - m_new)
    l = alpha * l_ref[...]
    acc = _lanes(alpha, KV_LORA) * acc_ref[...]
    for m, psum, pv in parts:
        wgt = jnp.exp2(m - m_new)
        l = l + wgt * psum
        acc = acc + _lanes(wgt, KV_LORA) * pv
    m_ref[...] = m_new
    l_ref[...] = l
    acc_ref[...] = acc


def _attn_sample_kernel(pt_ref, qlat_ref, qrope_ref, ckvn_ref, kropen_ref, cl_hbm, cr_hbm, o_ref,
                        kbuf, rbuf, sem, qs_ref, qr_ref, kn_ref, rn_ref, m_ref, l_ref, acc_ref,
                        *, n_pages, n_new):
    pg = SAMPLE_PAGES_PER_STEP
    chunks = n_pages // pg
    g = pl.program_id(0)
    n_steps = pl.num_programs(0)
    c = g % chunks
    last = n_steps - 1
    slot = lax.rem(g, SAMPLE_SLOTS)
    ahead = SAMPLE_SLOTS - 1

    def page_copies(step, dst_slot):
        first = (step // chunks) * n_pages + (step % chunks) * pg
        copies = []
        for p in range(pg):
            page = pt_ref[first + p]
            keys = pl.ds(p * PAGE_SIZE, PAGE_SIZE)
            copies.append(pltpu.make_async_copy(cl_hbm.at[page], kbuf.at[dst_slot, keys], sem.at[0, dst_slot]))
            copies.append(pltpu.make_async_copy(cr_hbm.at[page], rbuf.at[dst_slot, :, keys], sem.at[1, dst_slot]))
        return copies

    @pl.when(g == 0)
    def _():
        for step in range(ahead):
            for cp in page_copies(step, step):
                cp.start()

    @pl.when(c == 0)
    def _():
        ql = qlat_ref[0].astype(F32)
        qr = qrope_ref[0].astype(F32)
        for h in range(MLA_HEADS):
            rows = pl.ds(h * n_new, n_new)
            qs_ref[rows, :] = ql[:, KV_LORA * h:KV_LORA * (h + 1)]
            qr_ref[rows, :] = qr[:, QK_ROPE * h:QK_ROPE * (h + 1)]
        m_ref[...] = jnp.full_like(m_ref, NEG_BIG)
        l_ref[...] = jnp.zeros_like(l_ref)
        acc_ref[...] = jnp.zeros_like(acc_ref)

    for cp in page_copies(g, slot):
        cp.wait()

    qs = qs_ref[...].astype(BF16)
    qr = qr_ref[...].astype(BF16)
    sub = pg * PAGE_SIZE // SAMPLE_SUBCHUNKS
    def scores(j):
        keys = pl.ds(j * sub, sub)
        kc = kbuf[slot, keys, :].astype(BF16)
        return _dot_nt(qs, kc) + _dot(qr, rbuf[slot, :, keys].astype(BF16)), kc

    parts = []
    ready = [scores(j) for j in range(min(SAMPLE_LOOKAHEAD, SAMPLE_SUBCHUNKS))]
    for j in range(SAMPLE_SUBCHUNKS):
        if j + SAMPLE_LOOKAHEAD < SAMPLE_SUBCHUNKS:
            ready.append(scores(j + SAMPLE_LOOKAHEAD))
        parts.append(_softmax_part(*ready.pop(0)))
    _merge_softmax_parts(parts, m_ref, l_ref, acc_ref)

    for cp in page_copies(jnp.minimum(g + ahead, last), lax.rem(g + ahead, SAMPLE_SLOTS)):
        cp.start()

    @pl.when(g == last)
    def _():
        for extra in range(1, ahead + 1):
            for cp in page_copies(last, lax.rem(g + extra, SAMPLE_SLOTS)):
                cp.wait()

    @pl.when(c == chunks - 1)
    def _():
        kn_ref[...] = jnp.zeros_like(kn_ref)
        rn_ref[...] = jnp.zeros_like(rn_ref)
        kn_ref[pl.ds(0, n_new), :] = ckvn_ref[0]
        rn_ref[pl.ds(0, n_new), :] = kropen_ref[0]
        kn = kn_ref[...].astype(BF16)
        sn = _dot_nt(qs, kn) + _dot_nt(qr, rn_ref[...].astype(BF16))
        t = lax.broadcasted_iota(jnp.int32, sn.shape, 0) % n_new
        j = lax.broadcasted_iota(jnp.int32, sn.shape, 1)
        sn = jnp.where(j <= t, sn, NEG_BIG)
        _merge_softmax_parts([_softmax_part(sn, kn)], m_ref, l_ref, acc_ref)
        o_ref[0] = acc_ref[...] / _lanes(l_ref[...], KV_LORA)


def _attn_sample(page_table, qlat, qrope, ckv_new, krope_new, cache_latent, cache_k_rope):
    nb, n_pages = page_table.shape
    n_new = qlat.shape[0] // nb
    pg = SAMPLE_PAGES_PER_STEP
    chunks = n_pages // pg
    rows = MLA_HEADS * n_new
    new_pad = 8
    per_batch = lambda g, pt: (g // chunks, 0, 0)
    grid_spec = pltpu.PrefetchScalarGridSpec(
        num_scalar_prefetch=1,
        grid=(nb * chunks,),
        in_specs=[pl.BlockSpec((1, n_new, LAT_LANES), per_batch),
                  pl.BlockSpec((1, n_new, ROPE_LANES), per_batch),
                  pl.BlockSpec((1, n_new, KV_LORA), per_batch),
                  pl.BlockSpec((1, n_new, QK_ROPE), per_batch),
                  pl.BlockSpec(memory_space=pl.ANY),
                  pl.BlockSpec(memory_space=pl.ANY)],
        out_specs=pl.BlockSpec((1, rows, KV_LORA), per_batch),
        scratch_shapes=[pltpu.VMEM((SAMPLE_SLOTS, pg * PAGE_SIZE, KV_LORA), F32),
                        pltpu.VMEM((SAMPLE_SLOTS, QK_ROPE, pg * PAGE_SIZE), F32),
                        pltpu.SemaphoreType.DMA((2, SAMPLE_SLOTS)),
                        pltpu.VMEM((rows, KV_LORA), F32),
                        pltpu.VMEM((rows, QK_ROPE), F32),
                        pltpu.VMEM((new_pad, KV_LORA), F32),
                        pltpu.VMEM((new_pad, QK_ROPE), F32),
                        pltpu.VMEM((rows, LANES), F32),
                        pltpu.VMEM((rows, LANES), F32),
                        pltpu.VMEM((rows, KV_LORA), F32)])
    return pl.pallas_call(
        functools.partial(_attn_sample_kernel, n_pages=n_pages, n_new=n_new),
        out_shape=jax.ShapeDtypeStruct((nb, rows, KV_LORA), F32),
        grid_spec=grid_spec,
        compiler_params=pltpu.CompilerParams(dimension_semantics=("arbitrary",),
                                             vmem_limit_bytes=V7X_VMEM_LIMIT_BYTES),
        name="attn_sample",
    )(page_table.reshape(-1), qlat.reshape(nb, n_new, LAT_LANES), qrope.reshape(nb, n_new, ROPE_LANES),
      ckv_new.reshape(nb, n_new, KV_LORA), krope_new.reshape(nb, n_new, QK_ROPE), cache_latent, cache_k_rope)


def _post_mixer_kernel(h1_ref, ya_ref, ol_ref, gm_ref, wgt_ref, wuv_ref, wbo_ref, wout_ref,
                       g2_ref, wg_ref, wu_ref, wd_ref, gf_ref, y_ref):
    h1 = h1_ref[...]
    xm = _rms(h1, gm_ref[...]).astype(BF16)
    gates = _dot(xm, wgt_ref[...])
    ol = ol_ref[...]
    k = UV_HEADS_PER_TILE * KV_LORA
    o = jnp.concatenate([_dot(ol[:, k * j:k * (j + 1)], wuv_ref[j])
                         for j in range(MLA_HEADS // UV_HEADS_PER_TILE)], axis=1)
    yb = _dot(o.astype(BF16), wbo_ref[...])
    merged = jax.nn.sigmoid(gates[:, :D_MODEL]) * ya_ref[...] + jax.nn.sigmoid(gates[:, D_MODEL:]) * yb
    h2 = h1 + _dot(merged.astype(BF16), wout_ref[...])
    h3 = _ffn_half_step(h2, g2_ref, wg_ref, wu_ref, wd_ref)
    y_ref[...] = _rms(h3, gf_ref[...])


def _post_mixer(h1, ya, olat, w, *, seq_major_ya):
    m = h1.shape[0]
    tm = TOKEN_TILE
    row = lambda i: (i, 0)
    if seq_major_ya:
        blocks_per_seq = seq_major_ya // tm
        ya_spec = pl.BlockSpec((tm, D_MODEL), lambda i: (i % blocks_per_seq, i // blocks_per_seq))
    else:
        ya_spec = pl.BlockSpec((tm, D_MODEL), row)
    weights = (w['g_mix'], w['w_gates'], w['w_uv_pad'], w['w_b_out'], w['w_out'],
               w['g_ffn2'], w['wg2'], w['wu2'], w['wd2'], w['g_final'])
    return pl.pallas_call(
        _post_mixer_kernel,
        out_shape=jax.ShapeDtypeStruct((m, D_MODEL), F32),
        grid=(m // tm,),
        in_specs=[pl.BlockSpec((tm, D_MODEL), row), ya_spec, pl.BlockSpec((tm, LAT_LANES), row)]
        + [_resident(a.shape) for a in weights],
        out_specs=pl.BlockSpec((tm, D_MODEL), row),
        compiler_params=pltpu.CompilerParams(dimension_semantics=("parallel",),
                                             vmem_limit_bytes=V7X_VMEM_LIMIT_BYTES),
        name="post_mixer",
    )(h1, ya, olat, *weights)


def _rope_tables(pos):
    inv_freq = 1.0 / (ROPE_BASE ** (jnp.arange(0, QK_ROPE, 2, dtype=F32) / QK_ROPE))
    ang = pos.astype(F32)[:, None] * inv_freq[None, :]
    cos, sin = jnp.cos(ang), jnp.sin(ang)
    cos_tab = jnp.tile(jnp.concatenate([cos, cos], axis=-1), (1, MLA_HEADS))
    sin_tab = jnp.tile(jnp.concatenate([-sin, sin], axis=-1), (1, MLA_HEADS))
    return cos_tab, sin_tab


def _swap_halves(a):
    half = QK_ROPE // 2
    return jnp.concatenate([a[..., half:], a[..., :half]], axis=-1)


def _block_diag_runs(blocks, run):
    g, r, c = blocks.shape
    eye = jnp.eye(run, dtype=blocks.dtype)
    tiles = jnp.einsum('tgrc,gk->tgrkc', blocks.reshape(g // run, run, r, c), eye)
    return tiles.reshape(g // run, run * r, run * c)


def _prepare_weights(p):
    w = {}
    vec = lambda a: a.reshape(1, -1).astype(F32)
    for name in ('g_ffn1', 'g_mix', 'g_q', 'g_kv', 'g_ffn2', 'g_final', 'b_glu'):
        w[name] = vec(p[name])
    for src, dst in (('w_ffn1_gate', 'wg1'), ('w_ffn1_up', 'wu1'), ('w_ffn1_down', 'wd1'),
                     ('w_ffn2_gate', 'wg2'), ('w_ffn2_up', 'wu2'), ('w_ffn2_down', 'wd2'),
                     ('w_glu', 'w_glu'), ('w_a_out', 'w_a_out'), ('w_b_out', 'w_b_out'), ('w_out', 'w_out')):
        w[dst] = p[src].astype(BF16)
    w_in = p['w_in']
    off_kr = S5_WIDTH + Q_LORA + KV_LORA
    k_r = w_in[:, off_kr:off_kr + QK_ROPE]
    reps = LANES // 2 // QK_ROPE
    w['w_in_ext'] = jnp.concatenate(
        [w_in[:, :off_kr], jnp.tile(k_r, (1, reps)), jnp.tile(_swap_halves(k_r), (1, reps))],
        axis=1).astype(BF16)
    w['w_gates'] = w_in[:, off_kr + QK_ROPE:].astype(BF16)
    w_uq = p['w_uq']
    uq_rope = w_uq[:, :, QK_NOPE:]
    w['w_uq_ext'] = jnp.concatenate(
        [w_uq[:, :, :QK_NOPE].reshape(Q_LORA, -1), uq_rope.reshape(Q_LORA, -1),
         _swap_halves(uq_rope).reshape(Q_LORA, -1)], axis=1).astype(BF16)
    uk = jnp.transpose(p['w_uk'], (1, 2, 0))
    uk_pad = jnp.zeros((MLA_HEADS, 2, QK_NOPE, KV_LORA), F32)
    uk_pad = uk_pad.at[jnp.arange(MLA_HEADS), jnp.arange(MLA_HEADS) % 2].set(uk)
    w['w_uk_pad'] = uk_pad.reshape(MLA_HEADS, 2 * QK_NOPE, KV_LORA).astype(BF16)
    uv = jnp.transpose(p['w_uv'], (1, 0, 2))
    w['w_uv_pad'] = _block_diag_runs(uv, UV_HEADS_PER_TILE).astype(BF16)
    lam_re, lam_im = p['s5_a_re'].astype(F32), p['s5_a_im'].astype(F32)
    dt = jnp.exp(p['s5_log_dt'].astype(F32))[:, None]
    mag = jnp.exp(lam_re * dt)
    ab_re, ab_im = mag * jnp.cos(lam_im * dt), mag * jnp.sin(lam_im * dt)
    den = lam_re * lam_re + lam_im * lam_im
    num_re, num_im = ab_re - 1.0, ab_im
    k_re = (num_re * lam_re + num_im * lam_im) / den
    k_im = (num_im * lam_re - num_re * lam_im) / den
    w['a_re'] = ab_re.reshape(1, S5_COLS)
    w['a_im'] = ab_im.reshape(1, S5_COLS)
    b_re, b_im = p['s5_b_re'].astype(F32), p['s5_b_im'].astype(F32)
    kb_re = k_re[..., None] * b_re - k_im[..., None] * b_im
    kb_im = k_re[..., None] * b_im + k_im[..., None] * b_re

    def to_in(a):
        tiles = _block_diag_runs(jnp.transpose(a, (0, 2, 1)), S5_GROUPS_PER_STATE_TILE)
        n, r, c = tiles.shape
        pos = jnp.arange(n) % S5_TILES_PER_SLAB
        slab = jnp.zeros((n, S5_TILES_PER_SLAB, r, c), F32).at[jnp.arange(n), pos].set(tiles)
        return slab.reshape(n, S5_TILES_PER_SLAB * r, c)

    w['bpack'] = jnp.concatenate([to_in(kb_re), to_in(kb_im)], axis=0).astype(BF16)
    to_out = lambda a: _block_diag_runs(jnp.transpose(a, (0, 2, 1)), S5_GROUPS_PER_OUT_TILE)
    w['cpack'] = jnp.concatenate([to_out(p['s5_c_re'].astype(F32)), -to_out(p['s5_c_im'].astype(F32))],
                                 axis=1).astype(BF16)
    w['s5_d'] = vec(p['s5_d'])
    return w


def kernel(x_prompt, x_sample, cache_latent, cache_k_rope, state_ssm_re, state_ssm_im, page_table, g_ffn1, w_ffn1_gate, w_ffn1_up, w_ffn1_down, g_mix, w_in, s5_a_re, s5_a_im, s5_log_dt, s5_b_re, s5_b_im, s5_c_re, s5_c_im, s5_d, w_glu, b_glu, w_a_out, g_q, w_uq, g_kv, w_uk, w_uv, w_b_out, w_out, g_ffn2, w_ffn2_gate, w_ffn2_up, w_ffn2_down, g_final):
    layer = 0
    params = dict(
        g_ffn1=g_ffn1, w_ffn1_gate=w_ffn1_gate, w_ffn1_up=w_ffn1_up, w_ffn1_down=w_ffn1_down,
        g_mix=g_mix, w_in=w_in, s5_a_re=s5_a_re, s5_a_im=s5_a_im, s5_log_dt=s5_log_dt,
        s5_b_re=s5_b_re, s5_b_im=s5_b_im, s5_c_re=s5_c_re, s5_c_im=s5_c_im, s5_d=s5_d,
        w_glu=w_glu, b_glu=b_glu, w_a_out=w_a_out, g_q=g_q, w_uq=w_uq, g_kv=g_kv, w_uk=w_uk,
        w_uv=w_uv, w_b_out=w_b_out, w_out=w_out, g_ffn2=g_ffn2, w_ffn2_gate=w_ffn2_gate,
        w_ffn2_up=w_ffn2_up, w_ffn2_down=w_ffn2_down)
    p = {k: v[layer] for k, v in params.items()}
    p['g_final'] = g_final
    w = _prepare_weights(p)

    nb, seq, _ = x_prompt.shape
    cos_p, sin_p = _rope_tables(jnp.arange(seq))
    h1, u, ckv, krope, kcat, qlat, qrope = _pre_mixer(
        x_prompt.reshape(nb * seq, D_MODEL), cos_p, sin_p, w, seq_major_u=seq)
    olat = _attn_prompt(qlat, qrope, kcat, batch=nb, seq=seq)
    ya, sre_p, sim_p = _s5_prompt(u.reshape(seq, nb, S5_WIDTH), w, run_after=olat)
    y_prompt = _post_mixer(h1, ya.reshape(seq, nb * D_MODEL), olat, w, seq_major_ya=seq).reshape(nb, seq, D_MODEL)
    ckv_prompt = ckv.reshape(1, nb, seq, KV_LORA)
    krope_prompt = krope.reshape(1, nb, seq, QK_ROPE)
    group = lambda a: a.reshape(1, a.shape[0], S5_GROUPS, S5_STATE)

    sb, n_new, _ = x_sample.shape
    n_pages = page_table.shape[1]
    past_len = n_pages * PAGE_SIZE
    cos_s, sin_s = _rope_tables(past_len + jnp.arange(n_new))
    cos_s, sin_s = jnp.tile(cos_s, (sb, 1)), jnp.tile(sin_s, (sb, 1))
    h1s, us, ckvs, kropes, _, qlats, qropes = _pre_mixer(
        x_sample.reshape(sb * n_new, D_MODEL), cos_s, sin_s, w, seq_major_u=0)
    us_t = jnp.transpose(us.reshape(sb, n_new, S5_WIDTH), (1, 0, 2))
    yas_t, sre_s, sim_s = _s5_sample(us_t, state_ssm_re[layer].reshape(sb, S5_COLS),
                                     state_ssm_im[layer].reshape(sb, S5_COLS), w)
    yas = jnp.transpose(yas_t, (1, 0, 2)).reshape(sb * n_new, D_MODEL)
    rope_pages = jnp.swapaxes(cache_k_rope[layer], 1, 2)
    ol = _attn_sample(page_table, qlats, qropes, ckvs, kropes, cache_latent[layer], rope_pages)
    olats = jnp.transpose(ol.reshape(sb, MLA_HEADS, n_new, KV_LORA), (0, 2, 1, 3))
    olats = olats.reshape(sb * n_new, LAT_LANES).astype(BF16)
    y_sample = _post_mixer(h1s, yas, olats, w, seq_major_ya=0).reshape(sb, n_new, D_MODEL)

    return (y_prompt, y_sample, ckv_prompt, krope_prompt,
            ckvs.reshape(1, sb, n_new, KV_LORA), kropes.reshape(1, sb, n_new, QK_ROPE),
            group(sre_p), group(sim_p), group(sre_s), group(sim_s))
```

```python
import functools

import jax
import jax.numpy as jnp
from jax import lax
from jax.experimental import pallas as pl
from jax.experimental.pallas import tpu as pltpu

F32 = jnp.float32
BF16 = jnp.bfloat16

D_MODEL = 1024
D_FF = 2816
S5_WIDTH = 512
S5_GROUP = 16
S5_GROUPS = 32
S5_STATE = 64
S5_COLS = S5_GROUPS * S5_STATE
MLA_HEADS = 8
QK_NOPE = 64
QK_ROPE = 32
V_DIM = 64
Q_LORA = 384
KV_LORA = 256
ROPE_BASE = 10000.0
NORM_EPS = 1e-6
PAGE_SIZE = 128
ATTN_SCALE = (QK_NOPE + QK_ROPE) ** -0.5
Q_PRESCALE = ATTN_SCALE * 1.4426950408889634
LANES = 128
MXU_TILE = 256
S5_STATE_TILES = S5_COLS // MXU_TILE
S5_GROUPS_PER_STATE_TILE = MXU_TILE // S5_STATE
S5_TILES_PER_SLAB = LANES // (S5_GROUPS_PER_STATE_TILE * S5_GROUP)
UV_HEADS_PER_TILE = MXU_TILE // V_DIM
S5_OUT_TILES = S5_WIDTH // MXU_TILE
S5_GROUPS_PER_OUT_TILE = MXU_TILE // S5_GROUP
ROPE_LANES = MLA_HEADS * QK_ROPE
LAT_LANES = MLA_HEADS * KV_LORA
IN_EXT = S5_WIDTH + Q_LORA + KV_LORA + LANES
OFF_CQ = S5_WIDTH
OFF_CKV = OFF_CQ + Q_LORA
OFF_KR = OFF_CKV + KV_LORA
NEG_BIG = -1e30

V7X_VMEM_LIMIT_BYTES = 56 * 1024 * 1024
TOKEN_TILE = 512
S5_TIME_TILE = 64
S5_TIME_SPLIT = 2
S5_COL_TILE = 512
ATTN_TILE = 512
ATTN_ROW_SPLIT = 8
ATTN_QUERY_RUNS = 2
ATTN_LOOKAHEAD = 2
SAMPLE_LOOKAHEAD = 3
SAMPLE_PAGES_PER_STEP = 64
SAMPLE_SUBCHUNKS = 8
SAMPLE_SLOTS = 3


def _rms(x, g):
    return x * lax.rsqrt(jnp.mean(x * x, axis=-1, keepdims=True) + NORM_EPS) * g


def _dot(a, b):
    return jnp.dot(a, b, preferred_element_type=F32)


def _dot_nt(a, b):
    return lax.dot_general(a, b, (((1,), (1,)), ((), ())), preferred_element_type=F32)


def _ffn_half_step(x, g_ref, wg_ref, wu_ref, wd_ref):
    xn = _rms(x, g_ref[...]).astype(BF16)
    gate = _dot(xn, wg_ref[...])
    up = _dot(xn, wu_ref[...])
    hid = (jax.nn.silu(gate) * up).astype(BF16)
    return x + 0.5 * _dot(hid, wd_ref[...])


def _resident(shape):
    nd = len(shape)
    return pl.BlockSpec(shape, lambda *_: (0,) * nd, pipeline_mode=pl.Buffered(1))


def _pre_mixer_kernel(x_ref, cos_ref, sin_ref, g1_ref, wg_ref, wu_ref, wd_ref, gm_ref, win_ref,
                      gq_ref, gkv_ref, wuq_ref, wuk_ref,
                      h1_ref, u_ref, ckv_ref, krope_ref, kcat_ref, qlat_ref, qrope_ref):
    h1 = _ffn_half_step(x_ref[...], g1_ref, wg_ref, wu_ref, wd_ref)
    h1_ref[...] = h1
    xm = _rms(h1, gm_ref[...]).astype(BF16)
    proj = _dot(xm, win_ref[...])
    u_ref[...] = proj[:, :S5_WIDTH]
    cqn = _rms(proj[:, OFF_CQ:OFF_CKV], gq_ref[...]).astype(BF16)
    ckv = _rms(proj[:, OFF_CKV:OFF_KR], gkv_ref[...])
    ckv_ref[...] = ckv
    cos = cos_ref[...]
    sin = sin_ref[...]
    half = LANES // 2
    lane = lax.broadcasted_iota(jnp.int32, (proj.shape[0], LANES), 1)
    prod = proj[:, OFF_KR:IN_EXT] * jnp.where(lane < half, cos[:, :LANES], sin[:, :LANES])
    kr = prod + pltpu.roll(prod, half, axis=1)
    krope_ref[...] = kr[:, :QK_ROPE]
    kcat_ref[:, :KV_LORA] = ckv.astype(BF16)
    for slab in range(ROPE_LANES // LANES):
        kcat_ref[:, KV_LORA + LANES * slab:KV_LORA + LANES * (slab + 1)] = kr.astype(BF16)
    q = _dot(cqn, wuq_ref[...])
    n_nope = MLA_HEADS * QK_NOPE
    qrope_ref[...] = ((q[:, n_nope:n_nope + ROPE_LANES] * cos
                       + q[:, n_nope + ROPE_LANES:] * sin) * Q_PRESCALE).astype(BF16)
    qn = q[:, :n_nope].astype(BF16)
    for h in range(MLA_HEADS):
        pair = h // 2
        qlat_ref[:, KV_LORA * h:KV_LORA * (h + 1)] = (_dot(
            qn[:, LANES * pair:LANES * (pair + 1)], wuk_ref[h]) * Q_PRESCALE).astype(BF16)


def _pre_mixer(x, cos_tab, sin_tab, w, *, seq_major_u):
    m = x.shape[0]
    tm = TOKEN_TILE
    nsteps = m // tm
    tab_blocks = cos_tab.shape[0] // tm
    row = lambda i: (i, 0)
    tab = lambda i: (i % tab_blocks, 0)
    if seq_major_u:
        blocks_per_seq = seq_major_u // tm
        u_shape = (seq_major_u, (m // seq_major_u) * S5_WIDTH)
        u_spec = pl.BlockSpec((tm, S5_WIDTH), lambda i: (i % blocks_per_seq, i // blocks_per_seq))
    else:
        u_shape = (m, S5_WIDTH)
        u_spec = pl.BlockSpec((tm, S5_WIDTH), row)
    out_shape = (
        jax.ShapeDtypeStruct((m, D_MODEL), F32),
        jax.ShapeDtypeStruct(u_shape, F32),
        jax.ShapeDtypeStruct((m, KV_LORA), F32),
        jax.ShapeDtypeStruct((m, QK_ROPE), F32),
        jax.ShapeDtypeStruct((m, KV_LORA + ROPE_LANES), BF16),
        jax.ShapeDtypeStruct((m, LAT_LANES), BF16),
        jax.ShapeDtypeStruct((m, ROPE_LANES), BF16),
    )
    out_specs = (
        pl.BlockSpec((tm, D_MODEL), row),
        u_spec,
        pl.BlockSpec((tm, KV_LORA), row),
        pl.BlockSpec((tm, QK_ROPE), row),
        pl.BlockSpec((tm, KV_LORA + ROPE_LANES), row),
        pl.BlockSpec((tm, LAT_LANES), row),
        pl.BlockSpec((tm, ROPE_LANES), row),
    )
    weights = (w['g_ffn1'], w['wg1'], w['wu1'], w['wd1'], w['g_mix'], w['w_in_ext'],
               w['g_q'], w['g_kv'], w['w_uq_ext'], w['w_uk_pad'])
    in_specs = [pl.BlockSpec((tm, D_MODEL), row),
                pl.BlockSpec((tm, ROPE_LANES), tab),
                pl.BlockSpec((tm, ROPE_LANES), tab)] + [_resident(a.shape) for a in weights]
    return pl.pallas_call(
        _pre_mixer_kernel, out_shape=out_shape, grid=(nsteps,), in_specs=in_specs, out_specs=out_specs,
        compiler_params=pltpu.CompilerParams(dimension_semantics=("parallel",),
                                             vmem_limit_bytes=V7X_VMEM_LIMIT_BYTES),
        name="pre_mixer",
    )(x, cos_tab, sin_tab, *weights)


def _s5_readout(h_ref, rows, u, cpack_ref, d_ref, wglu_ref, bglu_ref, waout_ref):
    k = S5_COLS // S5_OUT_TILES
    tiles = []
    for n in range(S5_OUT_TILES):
        h_re = h_ref[rows, k * n:k * (n + 1)].astype(BF16)
        h_im = h_ref[rows, S5_COLS + k * n:S5_COLS + k * (n + 1)].astype(BF16)
        tiles.append(_dot(h_re, cpack_ref[n, :k]) + _dot(h_im, cpack_ref[n, k:]))
    y = jnp.concatenate(tiles, axis=1) + d_ref[...] * u
    z = jax.nn.gelu(y)
    gate = _dot(z.astype(BF16), wglu_ref[...]) + bglu_ref[...]
    return _dot((z * jax.nn.sigmoid(gate)).astype(BF16), waout_ref[...])


def _s5_drive(u, bpack_ref, xs_ref, rows):
    u_bf = u.astype(BF16)
    for j in range(2 * S5_STATE_TILES):
        slab = (j % S5_STATE_TILES) // S5_TILES_PER_SLAB
        xs_ref[rows, MXU_TILE * j:MXU_TILE * (j + 1)] = _dot(u_bf[:, LANES * slab:LANES * (slab + 1)],
                                                            bpack_ref[j])


def _s5_prompt_kernel(u_ref, order_ref, bpack_ref, are_ref, aim_ref, cpack_ref, d_ref, wglu_ref, bglu_ref,
                      waout_ref, ya_ref, sre_ref, sim_ref, xs_ref, hre_ref, him_ref):
    del order_ref
    tt, nb, _ = u_ref.shape
    tp = tt // S5_TIME_SPLIT
    rp = tp * nb

    @pl.when(pl.program_id(0) == 0)
    def _():
        hre_ref[...] = jnp.zeros_like(hre_ref)
        him_ref[...] = jnp.zeros_like(him_ref)

    u = u_ref[...].reshape(tt * nb, S5_WIDTH)
    for q in range(S5_TIME_SPLIT):
        _s5_drive(u[q * rp:(q + 1) * rp], bpack_ref, xs_ref, pl.ds(q * rp, rp))
    for q in range(S5_TIME_SPLIT):
        for cb in range(S5_COLS // S5_COL_TILE):
            re_cols = pl.ds(cb * S5_COL_TILE, S5_COL_TILE)
            im_cols = pl.ds(S5_COLS + cb * S5_COL_TILE, S5_COL_TILE)
            ar = jnp.broadcast_to(are_ref[:, re_cols], (nb, S5_COL_TILE))
            ai = jnp.broadcast_to(aim_ref[:, re_cols], (nb, S5_COL_TILE))
            hr, hi = hre_ref[:, re_cols], him_ref[:, re_cols]
            for t in range(q * tp, (q + 1) * tp):
                r = pl.ds(t * nb, nb)
                hr, hi = (ar * hr - ai * hi + xs_ref[r, re_cols], ar * hi + ai * hr + xs_ref[r, im_cols])
                xs_ref[r, re_cols] = hr
                xs_ref[r, im_cols] = hi
            hre_ref[:, re_cols] = hr
            him_ref[:, re_cols] = hi
        ya = _s5_readout(xs_ref, pl.ds(q * rp, rp), u[q * rp:(q + 1) * rp],
                         cpack_ref, d_ref, wglu_ref, bglu_ref, waout_ref)
        ya_ref[pl.ds(q * tp, tp)] = ya.reshape(tp, nb, D_MODEL)
    sre_ref[...] = hre_ref[...]
    sim_ref[...] = him_ref[...]


def _s5_prompt(u3, w, run_after):
    seq, nb, _ = u3.shape
    tt = S5_TIME_TILE
    weights = (run_after[:16, :LANES],
               w['bpack'], w['a_re'], w['a_im'], w['cpack'], w['s5_d'], w['w_glu'], w['b_glu'], w['w_a_out'])
    state = pl.BlockSpec((nb, S5_COLS), lambda i: (0, 0))
    return pl.pallas_call(
        _s5_prompt_kernel,
        out_shape=(jax.ShapeDtypeStruct((seq, nb, D_MODEL), F32),
                   jax.ShapeDtypeStruct((nb, S5_COLS), F32),
                   jax.ShapeDtypeStruct((nb, S5_COLS), F32)),
        grid=(seq // tt,),
        in_specs=[pl.BlockSpec((tt, nb, S5_WIDTH), lambda i: (i, 0, 0))] + [_resident(a.shape) for a in weights],
        out_specs=(pl.BlockSpec((tt, nb, D_MODEL), lambda i: (i, 0, 0)), state, state),
        scratch_shapes=[pltpu.VMEM((tt * nb, 2 * S5_COLS), F32),
                        pltpu.VMEM((nb, S5_COLS), F32),
                        pltpu.VMEM((nb, S5_COLS), F32)],
        compiler_params=pltpu.CompilerParams(dimension_semantics=("arbitrary",),
                                             vmem_limit_bytes=V7X_VMEM_LIMIT_BYTES),
        name="s5_prompt",
    )(u3, *weights)


def _s5_sample_kernel(u_ref, h0re_ref, h0im_ref, bpack_ref, are_ref, aim_ref, cpack_ref, d_ref, wglu_ref,
                      bglu_ref, waout_ref, ya_ref, sre_ref, sim_ref, hs_ref):
    tt, nb, _ = u_ref.shape
    u = u_ref[...].reshape(tt * nb, S5_WIDTH)
    _s5_drive(u, bpack_ref, hs_ref, pl.ds(0, tt * nb))
    ar = are_ref[...]
    ai = aim_ref[...]
    hr = h0re_ref[...]
    hi = h0im_ref[...]
    for t in range(tt):
        r = pl.ds(t * nb, nb)
        hr, hi = (ar * hr - ai * hi + hs_ref[r, :S5_COLS], ar * hi + ai * hr + hs_ref[r, S5_COLS:])
        hs_ref[r, :S5_COLS] = hr
        hs_ref[r, S5_COLS:] = hi
    sre_ref[...] = hr
    sim_ref[...] = hi
    ya = _s5_readout(hs_ref, pl.ds(0, tt * nb), u, cpack_ref, d_ref, wglu_ref, bglu_ref, waout_ref)
    ya_ref[...] = ya.reshape(tt, nb, D_MODEL)


def _s5_sample(u3, h0_re, h0_im, w):
    tt, nb, _ = u3.shape
    weights = (w['bpack'], w['a_re'], w['a_im'], w['cpack'], w['s5_d'], w['w_glu'], w['b_glu'], w['w_a_out'])
    args = (u3, h0_re, h0_im) + weights
    whole = lambda a: pl.BlockSpec(a.shape, lambda i, nd=a.ndim: (0,) * nd)
    outs = (jax.ShapeDtypeStruct((tt, nb, D_MODEL), F32),
            jax.ShapeDtypeStruct((nb, S5_COLS), F32),
            jax.ShapeDtypeStruct((nb, S5_COLS), F32))
    return pl.pallas_call(
        _s5_sample_kernel, out_shape=outs, grid=(1,),
        in_specs=[whole(a) for a in args], out_specs=tuple(whole(o) for o in outs),
        scratch_shapes=[pltpu.VMEM((tt * nb, 2 * S5_COLS), F32)],
        compiler_params=pltpu.CompilerParams(dimension_semantics=("arbitrary",),
                                             vmem_limit_bytes=V7X_VMEM_LIMIT_BYTES),
        name="s5_sample",
    )(*args)


def _lanes(stat, width):
    if width % LANES:
        return stat[:, :width]
    return jnp.tile(stat, (1, width // LANES))


def _softmax_block_update(s, v, rows, m_ref, l_ref, acc_ref, first=False):
    if first:
        m_new = jnp.broadcast_to(jnp.max(s, axis=-1, keepdims=True), (s.shape[0], LANES))
        p = jnp.exp2(s - _lanes(m_new, s.shape[1]))
        l_ref[rows, :] = jnp.broadcast_to(jnp.sum(p, axis=-1, keepdims=True), (s.shape[0], LANES))
        acc_ref[rows, :] = _dot(p.astype(BF16), v)
        m_ref[rows, :] = m_new
        return
    m_old = m_ref[rows, :]
    m_new = jnp.maximum(m_old, jnp.max(s, axis=-1, keepdims=True))
    alpha = jnp.exp2(m_old - m_new)
    p = jnp.exp2(s - _lanes(m_new, s.shape[1]))
    l_ref[rows, :] = alpha * l_ref[rows, :] + jnp.sum(p, axis=-1, keepdims=True)
    acc_ref[rows, :] = _lanes(alpha, KV_LORA) * acc_ref[rows, :] + _dot(p.astype(BF16), v)
    m_ref[rows, :] = m_new


def _attn_prompt_kernel(qlat_ref, qrope_ref, kcat_ref, o_ref, qs_ref, m_ref, l_ref, acc_ref):
    tq = ATTN_TILE
    sub = tq // ATTN_QUERY_RUNS
    qi = pl.program_id(1)
    lane_head = lax.broadcasted_iota(jnp.int32, (sub, ROPE_LANES), 1) // QK_ROPE
    for run in range(ATTN_QUERY_RUNS):
        toks = pl.ds(run * sub, sub)
        qr = qrope_ref[toks, :]
        for h in range(MLA_HEADS):
            rows = pl.ds((run * MLA_HEADS + h) * sub, sub)
            qs_ref[rows, :KV_LORA] = qlat_ref[toks, KV_LORA * h:KV_LORA * (h + 1)]
            qs_ref[rows, KV_LORA:] = jnp.where(lane_head == h, qr, jnp.zeros_like(qr))
    n = MLA_HEADS * tq // ATTN_ROW_SPLIT
    groups_per_run = ATTN_ROW_SPLIT // ATTN_QUERY_RUNS

    def block(k, causal, first=False):
        kc = kcat_ref[pl.ds(pl.multiple_of(k * tq, tq), tq), :]

        def visible(part):
            run = part // groups_per_run
            return kc[:(run + 1) * sub] if causal else kc

        def scores(part):
            s = _dot_nt(qs_ref[pl.ds(part * n, n), :], visible(part))
            if causal:
                run = part // groups_per_run
                r = (lax.broadcasted_iota(jnp.int32, s.shape, 0) & (sub - 1)) + run * sub
                c = lax.broadcasted_iota(jnp.int32, s.shape, 1)
                s = jnp.where(c <= r, s, NEG_BIG)
            return s

        ready = [scores(part) for part in range(min(ATTN_LOOKAHEAD, ATTN_ROW_SPLIT))]
        for part in range(ATTN_ROW_SPLIT):
            if part + ATTN_LOOKAHEAD < ATTN_ROW_SPLIT:
                ready.append(scores(part + ATTN_LOOKAHEAD))
            _softmax_block_update(ready.pop(0), visible(part)[:, :KV_LORA], pl.ds(part * n, n),
                                  m_ref, l_ref, acc_ref, first=first)

    def past(k, carry):
        block(k, False)
        return carry

    @pl.when(qi == 0)
    def _():
        block(0, True, first=True)

    @pl.when(qi > 0)
    def _():
        block(0, False, first=True)
        lax.fori_loop(1, qi, past, 0)
        block(qi, True)

    o = acc_ref[...] / _lanes(l_ref[...], KV_LORA)
    for run in range(ATTN_QUERY_RUNS):
        for h in range(MLA_HEADS):
            seg = run * MLA_HEADS + h
            o_ref[pl.ds(run * sub, sub), KV_LORA * h:KV_LORA * (h + 1)] = o[seg * sub:(seg + 1) * sub].astype(BF16)


def _attn_prompt(qlat, qrope, kcat, *, batch, seq):
    tq = ATTN_TILE
    nq = seq // tq
    rows = MLA_HEADS * tq
    qmap = lambda b, i: (b * nq + i, 0)
    return pl.pallas_call(
        _attn_prompt_kernel,
        out_shape=jax.ShapeDtypeStruct((batch * seq, LAT_LANES), BF16),
        grid=(batch, nq),
        in_specs=[pl.BlockSpec((tq, LAT_LANES), qmap),
                  pl.BlockSpec((tq, ROPE_LANES), qmap),
                  pl.BlockSpec((seq, KV_LORA + ROPE_LANES), lambda b, i: (b, 0))],
        out_specs=pl.BlockSpec((tq, LAT_LANES), qmap),
        scratch_shapes=[pltpu.VMEM((rows, KV_LORA + ROPE_LANES), BF16),
                        pltpu.VMEM((rows, LANES), F32),
                        pltpu.VMEM((rows, LANES), F32),
                        pltpu.VMEM((rows, KV_LORA), F32)],
        compiler_params=pltpu.CompilerParams(dimension_semantics=("parallel", "arbitrary"),
                                             vmem_limit_bytes=V7X_VMEM_LIMIT_BYTES),
        name="attn_prompt",
    )(qlat, qrope, kcat)


def _softmax_part(s, v):
    m = jnp.max(s, axis=-1, keepdims=True)
    p = jnp.exp2(s - m)
    return m, jnp.sum(p, axis=-1, keepdims=True), _dot(p.astype(BF16), v)


def _merge_softmax_parts(parts, m_ref, l_ref, acc_ref):
    m_old = m_ref[...]
    m_new = m_old
    for m, _, _ in parts:
        m_new = jnp.maximum(m_new, m)
    alpha = jnp.exp2(m_old - m_new)
    l = alpha * l_ref[...]
    acc = _lanes(alpha, KV_LORA) * acc_ref[...]
    for m, psum, pv in parts:
        wgt = jnp.exp2(m - m_new)
        l = l + wgt * psum
        acc = acc + _lanes(wgt, KV_LORA) * pv
    m_ref[...] = m_new
    l_ref[...] = l
    acc_ref[...] = acc


def _attn_sample_kernel(pt_ref, qlat_ref, qrope_ref, ckvn_ref, kropen_ref, cl_hbm, cr_hbm, o_ref,
                        kbuf, rbuf, sem, qs_ref, qr_ref, kn_ref, rn_ref, m_ref, l_ref, acc_ref,
                        *, n_pages, n_new):
    pg = SAMPLE_PAGES_PER_STEP
    chunks = n_pages // pg
    g = pl.program_id(0)
    n_steps = pl.num_programs(0)
    c = g % chunks
    last = n_steps - 1
    slot = lax.rem(g, SAMPLE_SLOTS)
    ahead = SAMPLE_SLOTS - 1

    def page_copies(step, dst_slot):
        first = (step // chunks) * n_pages + (step % chunks) * pg
        copies = []
        for p in range(pg):
            page = pt_ref[first + p]
            keys = pl.ds(p * PAGE_SIZE, PAGE_SIZE)
            copies.append(pltpu.make_async_copy(cl_hbm.at[page], kbuf.at[dst_slot, keys], sem.at[0, dst_slot]))
            copies.append(pltpu.make_async_copy(cr_hbm.at[page], rbuf.at[dst_slot, :, keys], sem.at[1, dst_slot]))
        return copies

    def start_pages(step, dst_slot):
        for i, cp in enumerate(page_copies(step, dst_slot)):
            cp.start(priority=(i // 2) % 2)

    @pl.when(g == 0)
    def _():
        for step in range(ahead):
            start_pages(step, step)

    @pl.when(c == 0)
    def _():
        ql = qlat_ref[0].astype(F32)
        qr = qrope_ref[0].astype(F32)
        for h in range(MLA_HEADS):
            rows = pl.ds(h * n_new, n_new)
            qs_ref[rows, :] = ql[:, KV_LORA * h:KV_LORA * (h + 1)]
            qr_ref[rows, :] = qr[:, QK_ROPE * h:QK_ROPE * (h + 1)]
        m_ref[...] = jnp.full_like(m_ref, NEG_BIG)
        l_ref[...] = jnp.zeros_like(l_ref)
        acc_ref[...] = jnp.zeros_like(acc_ref)

    for cp in page_copies(g, slot):
        cp.wait()

    qs = qs_ref[...].astype(BF16)
    qr = qr_ref[...].astype(BF16)
    sub = pg * PAGE_SIZE // SAMPLE_SUBCHUNKS
    def scores(j):
        keys = pl.ds(j * sub, sub)
        kc = kbuf[slot, keys, :].astype(BF16)
        return _dot_nt(qs, kc) + _dot(qr, rbuf[slot, :, keys].astype(BF16)), kc

    parts = []
    ready = [scores(j) for j in range(min(SAMPLE_LOOKAHEAD, SAMPLE_SUBCHUNKS))]
    for j in range(SAMPLE_SUBCHUNKS):
        if j + SAMPLE_LOOKAHEAD < SAMPLE_SUBCHUNKS:
            ready.append(scores(j + SAMPLE_LOOKAHEAD))
        parts.append(_softmax_part(*ready.pop(0)))
    _merge_softmax_parts(parts, m_ref, l_ref, acc_ref)

    start_pages(jnp.minimum(g + ahead, last), lax.rem(g + ahead, SAMPLE_SLOTS))

    @pl.when(g == last)
    def _():
        for extra in range(1, ahead + 1):
            for cp in page_copies(last, lax.rem(g + extra, SAMPLE_SLOTS)):
                cp.wait()

    @pl.when(c == chunks - 1)
    def _():
        kn_ref[...] = jnp.zeros_like(kn_ref)
        rn_ref[...] = jnp.zeros_like(rn_ref)
        kn_ref[pl.ds(0, n_new), :] = ckvn_ref[0]
        rn_ref[pl.ds(0, n_new), :] = kropen_ref[0]
        kn = kn_ref[...].astype(BF16)
        sn = _dot_nt(qs, kn) + _dot_nt(qr, rn_ref[...].astype(BF16))
        t = lax.broadcasted_iota(jnp.int32, sn.shape, 0) % n_new
        j = lax.broadcasted_iota(jnp.int32, sn.shape, 1)
        sn = jnp.where(j <= t, sn, NEG_BIG)
        _merge_softmax_parts([_softmax_part(sn, kn)], m_ref, l_ref, acc_ref)
        o_ref[0] = acc_ref[...] / _lanes(l_ref[...], KV_LORA)


def _attn_sample(page_table, qlat, qrope, ckv_new, krope_new, cache_latent, cache_k_rope):
    nb, n_pages = page_table.shape
    n_new = qlat.shape[0] // nb
    pg = SAMPLE_PAGES_PER_STEP
    chunks = n_pages // pg
    rows = MLA_HEADS * n_new
    new_pad = 8
    per_batch = lambda g, pt: (g // chunks, 0, 0)
    grid_spec = pltpu.PrefetchScalarGridSpec(
        num_scalar_prefetch=1,
        grid=(nb * chunks,),
        in_specs=[pl.BlockSpec((1, n_new, LAT_LANES), per_batch),
                  pl.BlockSpec((1, n_new, ROPE_LANES), per_batch),
                  pl.BlockSpec((1, n_new, KV_LORA), per_batch),
                  pl.BlockSpec((1, n_new, QK_ROPE), per_batch),
                  pl.BlockSpec(memory_space=pl.ANY),
                  pl.BlockSpec(memory_space=pl.ANY)],
        out_specs=pl.BlockSpec((1, rows, KV_LORA), per_batch),
        scratch_shapes=[pltpu.VMEM((SAMPLE_SLOTS, pg * PAGE_SIZE, KV_LORA), F32),
                        pltpu.VMEM((SAMPLE_SLOTS, QK_ROPE, pg * PAGE_SIZE), F32),
                        pltpu.SemaphoreType.DMA((2, SAMPLE_SLOTS)),
                        pltpu.VMEM((rows, KV_LORA), F32),
                        pltpu.VMEM((rows, QK_ROPE), F32),
                        pltpu.VMEM((new_pad, KV_LORA), F32),
                        pltpu.VMEM((new_pad, QK_ROPE), F32),
                        pltpu.VMEM((rows, LANES), F32),
                        pltpu.VMEM((rows, LANES), F32),
                        pltpu.VMEM((rows, KV_LORA), F32)])
    return pl.pallas_call(
        functools.partial(_attn_sample_kernel, n_pages=n_pages, n_new=n_new),
        out_shape=jax.ShapeDtypeStruct((nb, rows, KV_LORA), F32),
        grid_spec=grid_spec,
        compiler_params=pltpu.CompilerParams(dimension_semantics=("arbitrary",),
                                             vmem_limit_bytes=V7X_VMEM_LIMIT_BYTES),
        name="attn_sample",
    )(page_table.reshape(-1), qlat.reshape(nb, n_new, LAT_LANES), qrope.reshape(nb, n_new, ROPE_LANES),
      ckv_new.reshape(nb, n_new, KV_LORA), krope_new.reshape(nb, n_new, QK_ROPE), cache_latent, cache_k_rope)


def _post_mixer_kernel(h1_ref, ya_ref, ol_ref, gm_ref, wgt_ref, wuv_ref, wbo_ref, wout_ref,
                       g2_ref, wg_ref, wu_ref, wd_ref, gf_ref, y_ref):
    h1 = h1_ref[...]
    xm = _rms(h1, gm_ref[...]).astype(BF16)
    gates = _dot(xm, wgt_ref[...])
    ol = ol_ref[...]
    k = UV_HEADS_PER_TILE * KV_LORA
    o = jnp.concatenate([_dot(ol[:, k * j:k * (j + 1)], wuv_ref[j])
                         for j in range(MLA_HEADS // UV_HEADS_PER_TILE)], axis=1)
    yb = _dot(o.astype(BF16), wbo_ref[...])
    merged = jax.nn.sigmoid(gates[:, :D_MODEL]) * ya_ref[...] + jax.nn.sigmoid(gates[:, D_MODEL:]) * yb
    h2 = h1 + _dot(merged.astype(BF16), wout_ref[...])
    h3 = _ffn_half_step(h2, g2_ref, wg_ref, wu_ref, wd_ref)
    y_ref[...] = _rms(h3, gf_ref[...])


def _post_mixer(h1, ya, olat, w, *, seq_major_ya):
    m = h1.shape[0]
    tm = TOKEN_TILE
    row = lambda i: (i, 0)
    if seq_major_ya:
        blocks_per_seq = seq_major_ya // tm
        ya_spec = pl.BlockSpec((tm, D_MODEL), lambda i: (i % blocks_per_seq, i // blocks_per_seq))
    else:
        ya_spec = pl.BlockSpec((tm, D_MODEL), row)
    weights = (w['g_mix'], w['w_gates'], w['w_uv_pad'], w['w_b_out'], w['w_out'],
               w['g_ffn2'], w['wg2'], w['wu2'], w['wd2'], w['g_final'])
    return pl.pallas_call(
        _post_mixer_kernel,
        out_shape=jax.ShapeDtypeStruct((m, D_MODEL), F32),
        grid=(m // tm,),
        in_specs=[pl.BlockSpec((tm, D_MODEL), row), ya_spec, pl.BlockSpec((tm, LAT_LANES), row)]
        + [_resident(a.shape) for a in weights],
        out_specs=pl.BlockSpec((tm, D_MODEL), row),
        compiler_params=pltpu.CompilerParams(dimension_semantics=("parallel",),
                                             vmem_limit_bytes=V7X_VMEM_LIMIT_BYTES),
        name="post_mixer",
    )(h1, ya, olat, *weights)


def _rope_tables(pos):
    inv_freq = 1.0 / (ROPE_BASE ** (jnp.arange(0, QK_ROPE, 2, dtype=F32) / QK_ROPE))
    ang = pos.astype(F32)[:, None] * inv_freq[None, :]
    cos, sin = jnp.cos(ang), jnp.sin(ang)
    cos_tab = jnp.tile(jnp.concatenate([cos, cos], axis=-1), (1, MLA_HEADS))
    sin_tab = jnp.tile(jnp.concatenate([-sin, sin], axis=-1), (1, MLA_HEADS))
    return cos_tab, sin_tab


def _swap_halves(a):
    half = QK_ROPE // 2
    return jnp.concatenate([a[..., half:], a[..., :half]], axis=-1)


def _block_diag_runs(blocks, run):
    g, r, c = blocks.shape
    eye = jnp.eye(run, dtype=blocks.dtype)
    tiles = jnp.einsum('tgrc,gk->tgrkc', blocks.reshape(g // run, run, r, c), eye)
    return tiles.reshape(g // run, run * r, run * c)


def _prepare_weights(p):
    w = {}
    vec = lambda a: a.reshape(1, -1).astype(F32)
    for name in ('g_ffn1', 'g_mix', 'g_q', 'g_kv', 'g_ffn2', 'g_final', 'b_glu'):
        w[name] = vec(p[name])
    for src, dst in (('w_ffn1_gate', 'wg1'), ('w_ffn1_up', 'wu1'), ('w_ffn1_down', 'wd1'),
                     ('w_ffn2_gate', 'wg2'), ('w_ffn2_up', 'wu2'), ('w_ffn2_down', 'wd2'),
                     ('w_glu', 'w_glu'), ('w_a_out', 'w_a_out'), ('w_b_out', 'w_b_out'), ('w_out', 'w_out')):
        w[dst] = p[src].astype(BF16)
    w_in = p['w_in']
    off_kr = S5_WIDTH + Q_LORA + KV_LORA
    k_r = w_in[:, off_kr:off_kr + QK_ROPE]
    reps = LANES // 2 // QK_ROPE
    w['w_in_ext'] = jnp.concatenate(
        [w_in[:, :off_kr], jnp.tile(k_r, (1, reps)), jnp.tile(_swap_halves(k_r), (1, reps))],
        axis=1).astype(BF16)
    w['w_gates'] = w_in[:, off_kr + QK_ROPE:].astype(BF16)
    w_uq = p['w_uq']
    uq_rope = w_uq[:, :, QK_NOPE:]
    w['w_uq_ext'] = jnp.concatenate(
        [w_uq[:, :, :QK_NOPE].reshape(Q_LORA, -1), uq_rope.reshape(Q_LORA, -1),
         _swap_halves(uq_rope).reshape(Q_LORA, -1)], axis=1).astype(BF16)
    uk = jnp.transpose(p['w_uk'], (1, 2, 0))
    uk_pad = jnp.zeros((MLA_HEADS, 2, QK_NOPE, KV_LORA), F32)
    uk_pad = uk_pad.at[jnp.arange(MLA_HEADS), jnp.arange(MLA_HEADS) % 2].set(uk)
    w['w_uk_pad'] = uk_pad.reshape(MLA_HEADS, 2 * QK_NOPE, KV_LORA).astype(BF16)
    uv = jnp.transpose(p['w_uv'], (1, 0, 2))
    w['w_uv_pad'] = _block_diag_runs(uv, UV_HEADS_PER_TILE).astype(BF16)
    lam_re, lam_im = p['s5_a_re'].astype(F32), p['s5_a_im'].astype(F32)
    dt = jnp.exp(p['s5_log_dt'].astype(F32))[:, None]
    mag = jnp.exp(lam_re * dt)
    ab_re, ab_im = mag * jnp.cos(lam_im * dt), mag * jnp.sin(lam_im * dt)
    den = lam_re * lam_re + lam_im * lam_im
    num_re, num_im = ab_re - 1.0, ab_im
    k_re = (num_re * lam_re + num_im * lam_im) / den
    k_im = (num_im * lam_re - num_re * lam_im) / den
    w['a_re'] = ab_re.reshape(1, S5_COLS)
    w['a_im'] = ab_im.reshape(1, S5_COLS)
    b_re, b_im = p['s5_b_re'].astype(F32), p['s5_b_im'].astype(F32)
    kb_re = k_re[..., None] * b_re - k_im[..., None] * b_im
    kb_im = k_re[..., None] * b_im + k_im[..., None] * b_re

    def to_in(a):
        tiles = _block_diag_runs(jnp.transpose(a, (0, 2, 1)), S5_GROUPS_PER_STATE_TILE)
        n, r, c = tiles.shape
        pos = jnp.arange(n) % S5_TILES_PER_SLAB
        slab = jnp.zeros((n, S5_TILES_PER_SLAB, r, c), F32).at[jnp.arange(n), pos].set(tiles)
        return slab.reshape(n, S5_TILES_PER_SLAB * r, c)

    w['bpack'] = jnp.concatenate([to_in(kb_re), to_in(kb_im)], axis=0).astype(BF16)
    to_out = lambda a: _block_diag_runs(jnp.transpose(a, (0, 2, 1)), S5_GROUPS_PER_OUT_TILE)
    w['cpack'] = jnp.concatenate([to_out(p['s5_c_re'].astype(F32)), -to_out(p['s5_c_im'].astype(F32))],
                                 axis=1).astype(BF16)
    w['s5_d'] = vec(p['s5_d'])
    return w


def kernel(x_prompt, x_sample, cache_latent, cache_k_rope, state_ssm_re, state_ssm_im, page_table, g_ffn1, w_ffn1_gate, w_ffn1_up, w_ffn1_down, g_mix, w_in, s5_a_re, s5_a_im, s5_log_dt, s5_b_re, s5_b_im, s5_c_re, s5_c_im, s5_d, w_glu, b_glu, w_a_out, g_q, w_uq, g_kv, w_uk, w_uv, w_b_out, w_out, g_ffn2, w_ffn2_gate, w_ffn2_up, w_ffn2_down, g_final):
    layer = 0
    params = dict(
        g_ffn1=g_ffn1, w_ffn1_gate=w_ffn1_gate, w_ffn1_up=w_ffn1_up, w_ffn1_down=w_ffn1_down,
        g_mix=g_mix, w_in=w_in, s5_a_re=s5_a_re, s5_a_im=s5_a_im, s5_log_dt=s5_log_dt,
        s5_b_re=s5_b_re, s5_b_im=s5_b_im, s5_c_re=s5_c_re, s5_c_im=s5_c_im, s5_d=s5_d,
        w_glu=w_glu, b_glu=b_glu, w_a_out=w_a_out, g_q=g_q, w_uq=w_uq, g_kv=g_kv, w_uk=w_uk,
        w_uv=w_uv, w_b_out=w_b_out, w_out=w_out, g_ffn2=g_ffn2, w_ffn2_gate=w_ffn2_gate,
        w_ffn2_up=w_ffn2_up, w_ffn2_down=w_ffn2_down)
    p = {k: v[layer] for k, v in params.items()}
    p['g_final'] = g_final
    w = _prepare_weights(p)

    nb, seq, _ = x_prompt.shape
    cos_p, sin_p = _rope_tables(jnp.arange(seq))
    h1, u, ckv, krope, kcat, qlat, qrope = _pre_mixer(
        x_prompt.reshape(nb * seq, D_MODEL), cos_p, sin_p, w, seq_major_u=seq)
    olat = _attn_prompt(qlat, qrope, kcat, batch=nb, seq=seq)
    ya, sre_p, sim_p = _s5_prompt(u.reshape(seq, nb, S5_WIDTH), w, run_after=olat)
    y_prompt = _post_mixer(h1, ya.reshape(seq, nb * D_MODEL), olat, w, seq_major_ya=seq).reshape(nb, seq, D_MODEL)
    ckv_prompt = ckv.reshape(1, nb, seq, KV_LORA)
    krope_prompt = krope.reshape(1, nb, seq, QK_ROPE)
    group = lambda a: a.reshape(1, a.shape[0], S5_GROUPS, S5_STATE)

    sb, n_new, _ = x_sample.shape
    n_pages = page_table.shape[1]
    past_len = n_pages * PAGE_SIZE
    cos_s, sin_s = _rope_tables(past_len + jnp.arange(n_new))
    cos_s, sin_s = jnp.tile(cos_s, (sb, 1)), jnp.tile(sin_s, (sb, 1))
    h1s, us, ckvs, kropes, _, qlats, qropes = _pre_mixer(
        x_sample.reshape(sb * n_new, D_MODEL), cos_s, sin_s, w, seq_major_u=0)
    us_t = jnp.transpose(us.reshape(sb, n_new, S5_WIDTH), (1, 0, 2))
    yas_t, sre_s, sim_s = _s5_sample(us_t, state_ssm_re[layer].reshape(sb, S5_COLS),
                                     state_ssm_im[layer].reshape(sb, S5_COLS), w)
    yas = jnp.transpose(yas_t, (1, 0, 2)).reshape(sb * n_new, D_MODEL)
    rope_pages = jnp.swapaxes(cache_k_rope[layer], 1, 2)
    ol = _attn_sample(page_table, qlats, qropes, ckvs, kropes, cache_latent[layer], rope_pages)
    olats = jnp.transpose(ol.reshape(sb, MLA_HEADS, n_new, KV_LORA), (0, 2, 1, 3))
    olats = olats.reshape(sb * n_new, LAT_LANES).astype(BF16)
    y_sample = _post_mixer(h1s, yas, olats, w, seq_major_ya=0).reshape(sb, n_new, D_MODEL)

    return (y_prompt, y_sample, ckv_prompt, krope_prompt,
            ckvs.reshape(1, sb, n_new, KV_LORA), kropes.reshape(1, sb, n_new, QK_ROPE),
            group(sre_p), group(sim_p), group(sre_s), group(sim_s))
```

```python
import functools

import jax
import jax.numpy as jnp
from jax import lax
from jax.experimental import pallas as pl
from jax.experimental.pallas import tpu as pltpu

F32 = jnp.float32
BF16 = jnp.bfloat16

D_MODEL = 1024
D_FF = 2816
S5_WIDTH = 512
S5_GROUP = 16
S5_GROUPS = 32
S5_STATE = 64
S5_COLS = S5_GROUPS * S5_STATE
MLA_HEADS = 8
QK_NOPE = 64
QK_ROPE = 32
V_DIM = 64
Q_LORA = 384
KV_LORA = 256
ROPE_BASE = 10000.0
NORM_EPS = 1e-6
PAGE_SIZE = 128
ATTN_SCALE = (QK_NOPE + QK_ROPE) ** -0.5
Q_PRESCALE = ATTN_SCALE * 1.4426950408889634
LANES = 128
MXU_TILE = 256
S5_STATE_TILES = S5_COLS // MXU_TILE
S5_GROUPS_PER_STATE_TILE = MXU_TILE // S5_STATE
S5_TILES_PER_SLAB = LANES // (S5_GROUPS_PER_STATE_TILE * S5_GROUP)
UV_HEADS_PER_TILE = MXU_TILE // V_DIM
S5_OUT_TILES = S5_WIDTH // MXU_TILE
S5_GROUPS_PER_OUT_TILE = MXU_TILE // S5_GROUP
ROPE_LANES = MLA_HEADS * QK_ROPE
LAT_LANES = MLA_HEADS * KV_LORA
IN_EXT = S5_WIDTH + Q_LORA + KV_LORA + LANES
OFF_CQ = S5_WIDTH
OFF_CKV = OFF_CQ + Q_LORA
OFF_KR = OFF_CKV + KV_LORA
NEG_BIG = -1e30

V7X_VMEM_LIMIT_BYTES = 56 * 1024 * 1024
TOKEN_TILE = 512
S5_TIME_TILE = 64
S5_TIME_SPLIT = 2
S5_COL_TILE = 512
ATTN_TILE = 512
ATTN_ROW_SPLIT = 8
ATTN_QUERY_RUNS = 2
ATTN_LOOKAHEAD = 2
SAMPLE_LOOKAHEAD = 3
SAMPLE_PAGES_PER_STEP = 64
SAMPLE_SUBCHUNKS = 8
SAMPLE_SLOTS = 3


def _rms(x, g):
    return x * lax.rsqrt(jnp.mean(x * x, axis=-1, keepdims=True) + NORM_EPS) * g


def _dot(a, b):
    return jnp.dot(a, b, preferred_element_type=F32)


def _dot_nt(a, b):
    return lax.dot_general(a, b, (((1,), (1,)), ((), ())), preferred_element_type=F32)


def _ffn_half_step(x, g_ref, wg_ref, wu_ref, wd_ref):
    xn = _rms(x, g_ref[...]).astype(BF16)
    gate = _dot(xn, wg_ref[...])
    up = _dot(xn, wu_ref[...])
    hid = (jax.nn.silu(gate) * up).astype(BF16)
    return x + 0.5 * _dot(hid, wd_ref[...])


def _resident(shape):
    nd = len(shape)
    return pl.BlockSpec(shape, lambda *_: (0,) * nd, pipeline_mode=pl.Buffered(1))


def _pre_mixer_kernel(x_ref, cos_ref, sin_ref, g1_ref, wg_ref, wu_ref, wd_ref, gm_ref, win_ref,
                      gq_ref, gkv_ref, wuq_ref, wuk_ref,
                      h1_ref, u_ref, ckv_ref, krope_ref, kcat_ref, qlat_ref, qrope_ref):
    h1 = _ffn_half_step(x_ref[...], g1_ref, wg_ref, wu_ref, wd_ref)
    h1_ref[...] = h1
    xm = _rms(h1, gm_ref[...]).astype(BF16)
    proj = _dot(xm, win_ref[...])
    u_ref[...] = proj[:, :S5_WIDTH]
    cqn = _rms(proj[:, OFF_CQ:OFF_CKV], gq_ref[...]).astype(BF16)
    ckv = _rms(proj[:, OFF_CKV:OFF_KR], gkv_ref[...])
    ckv_ref[...] = ckv
    cos = cos_ref[...]
    sin = sin_ref[...]
    half = LANES // 2
    lane = lax.broadcasted_iota(jnp.int32, (proj.shape[0], LANES), 1)
    prod = proj[:, OFF_KR:IN_EXT] * jnp.where(lane < half, cos[:, :LANES], sin[:, :LANES])
    kr = prod + pltpu.roll(prod, half, axis=1)
    krope_ref[...] = kr[:, :QK_ROPE]
    kcat_ref[:, :KV_LORA] = ckv.astype(BF16)
    for slab in range(ROPE_LANES // LANES):
        kcat_ref[:, KV_LORA + LANES * slab:KV_LORA + LANES * (slab + 1)] = kr.astype(BF16)
    q = _dot(cqn, wuq_ref[...])
    n_nope = MLA_HEADS * QK_NOPE
    qr = q[:, n_nope:]
    lanes = lax.broadcasted_iota(jnp.int32, qr.shape, 1)
    half_rope = QK_ROPE // 2
    partner = jnp.where((lanes & (QK_ROPE - 1)) < half_rope, lanes + half_rope, lanes - half_rope)
    fwd = pltpu.roll(qr, half_rope, axis=1)
    bwd = pltpu.roll(qr, ROPE_LANES - half_rope, axis=1)
    swapped = jnp.where(pltpu.roll(lanes, half_rope, axis=1) == partner, fwd, bwd)
    qrope_ref[...] = ((qr * cos + swapped * sin) * Q_PRESCALE).astype(BF16)
    qn = q[:, :n_nope].astype(BF16)
    for h in range(MLA_HEADS):
        pair = h // 2
        qlat_ref[:, KV_LORA * h:KV_LORA * (h + 1)] = (_dot(
            qn[:, LANES * pair:LANES * (pair + 1)], wuk_ref[h]) * Q_PRESCALE).astype(BF16)


def _pre_mixer(x, cos_tab, sin_tab, w, *, seq_major_u):
    m = x.shape[0]
    tm = TOKEN_TILE
    nsteps = m // tm
    tab_blocks = cos_tab.shape[0] // tm
    row = lambda i: (i, 0)
    tab = lambda i: (i % tab_blocks, 0)
    if seq_major_u:
        blocks_per_seq = seq_major_u // tm
        u_shape = (seq_major_u, (m // seq_major_u) * S5_WIDTH)
        u_spec = pl.BlockSpec((tm, S5_WIDTH), lambda i: (i % blocks_per_seq, i // blocks_per_seq))
    else:
        u_shape = (m, S5_WIDTH)
        u_spec = pl.BlockSpec((tm, S5_WIDTH), row)
    out_shape = (
        jax.ShapeDtypeStruct((m, D_MODEL), F32),
        jax.ShapeDtypeStruct(u_shape, F32),
        jax.ShapeDtypeStruct((m, KV_LORA), F32),
        jax.ShapeDtypeStruct((m, QK_ROPE), F32),
        jax.ShapeDtypeStruct((m, KV_LORA + ROPE_LANES), BF16),
        jax.ShapeDtypeStruct((m, LAT_LANES), BF16),
        jax.ShapeDtypeStruct((m, ROPE_LANES), BF16),
    )
    out_specs = (
        pl.BlockSpec((tm, D_MODEL), row),
        u_spec,
        pl.BlockSpec((tm, KV_LORA), row),
        pl.BlockSpec((tm, QK_ROPE), row),
        pl.BlockSpec((tm, KV_LORA + ROPE_LANES), row),
        pl.BlockSpec((tm, LAT_LANES), row),
        pl.BlockSpec((tm, ROPE_LANES), row),
    )
    weights = (w['g_ffn1'], w['wg1'], w['wu1'], w['wd1'], w['g_mix'], w['w_in_ext'],
               w['g_q'], w['g_kv'], w['w_uq_ext'], w['w_uk_pad'])
    in_specs = [pl.BlockSpec((tm, D_MODEL), row),
                pl.BlockSpec((tm, ROPE_LANES), tab),
                pl.BlockSpec((tm, ROPE_LANES), tab)] + [_resident(a.shape) for a in weights]
    return pl.pallas_call(
        _pre_mixer_kernel, out_shape=out_shape, grid=(nsteps,), in_specs=in_specs, out_specs=out_specs,
        compiler_params=pltpu.CompilerParams(dimension_semantics=("parallel",),
                                             vmem_limit_bytes=V7X_VMEM_LIMIT_BYTES),
        name="pre_mixer",
    )(x, cos_tab, sin_tab, *weights)


def _s5_readout(h_ref, rows, u, cpack_ref, d_ref, wglu_ref, bglu_ref, waout_ref):
    k = S5_COLS // S5_OUT_TILES
    tiles = []
    for n in range(S5_OUT_TILES):
        h_re = h_ref[rows, k * n:k * (n + 1)].astype(BF16)
        h_im = h_ref[rows, S5_COLS + k * n:S5_COLS + k * (n + 1)].astype(BF16)
        tiles.append(_dot(h_re, cpack_ref[n, :k]) + _dot(h_im, cpack_ref[n, k:]))
    y = jnp.concatenate(tiles, axis=1) + d_ref[...] * u
    z = jax.nn.gelu(y)
    gate = _dot(z.astype(BF16), wglu_ref[...]) + bglu_ref[...]
    return _dot((z * jax.nn.sigmoid(gate)).astype(BF16), waout_ref[...])


def _s5_drive(u, bpack_ref, xs_ref, rows):
    u_bf = u.astype(BF16)
    for j in range(2 * S5_STATE_TILES):
        slab = (j % S5_STATE_TILES) // S5_TILES_PER_SLAB
        xs_ref[rows, MXU_TILE * j:MXU_TILE * (j + 1)] = _dot(u_bf[:, LANES * slab:LANES * (slab + 1)],
                                                            bpack_ref[j])


def _s5_prompt_kernel(u_ref, order_ref, bpack_ref, are_ref, aim_ref, cpack_ref, d_ref, wglu_ref, bglu_ref,
                      waout_ref, ya_ref, sre_ref, sim_ref, xs_ref, hre_ref, him_ref):
    del order_ref
    tt, nb, _ = u_ref.shape
    tp = tt // S5_TIME_SPLIT
    rp = tp * nb

    @pl.when(pl.program_id(0) == 0)
    def _():
        hre_ref[...] = jnp.zeros_like(hre_ref)
        him_ref[...] = jnp.zeros_like(him_ref)

    u = u_ref[...].reshape(tt * nb, S5_WIDTH)
    for q in range(S5_TIME_SPLIT):
        _s5_drive(u[q * rp:(q + 1) * rp], bpack_ref, xs_ref, pl.ds(q * rp, rp))
    for q in range(S5_TIME_SPLIT):
        for cb in range(S5_COLS // S5_COL_TILE):
            re_cols = pl.ds(cb * S5_COL_TILE, S5_COL_TILE)
            im_cols = pl.ds(S5_COLS + cb * S5_COL_TILE, S5_COL_TILE)
            ar = jnp.broadcast_to(are_ref[:, re_cols], (nb, S5_COL_TILE))
            ai = jnp.broadcast_to(aim_ref[:, re_cols], (nb, S5_COL_TILE))
            hr, hi = hre_ref[:, re_cols], him_ref[:, re_cols]
            for t in range(q * tp, (q + 1) * tp):
                r = pl.ds(t * nb, nb)
                hr, hi = (ar * hr - ai * hi + xs_ref[r, re_cols], ar * hi + ai * hr + xs_ref[r, im_cols])
                xs_ref[r, re_cols] = hr
                xs_ref[r, im_cols] = hi
            hre_ref[:, re_cols] = hr
            him_ref[:, re_cols] = hi
        ya = _s5_readout(xs_ref, pl.ds(q * rp, rp), u[q * rp:(q + 1) * rp],
                         cpack_ref, d_ref, wglu_ref, bglu_ref, waout_ref)
        ya_ref[pl.ds(q * tp, tp)] = ya.reshape(tp, nb, D_MODEL)
    sre_ref[...] = hre_ref[...]
    sim_ref[...] = him_ref[...]


def _s5_prompt(u3, w, run_after):
    seq, nb, _ = u3.shape
    tt = S5_TIME_TILE
    weights = (run_after[:16, :LANES],
               w['bpack'], w['a_re'], w['a_im'], w['cpack'], w['s5_d'], w['w_glu'], w['b_glu'], w['w_a_out'])
    state = pl.BlockSpec((nb, S5_COLS), lambda i: (0, 0))
    return pl.pallas_call(
        _s5_prompt_kernel,
        out_shape=(jax.ShapeDtypeStruct((seq, nb, D_MODEL), F32),
                   jax.ShapeDtypeStruct((nb, S5_COLS), F32),
                   jax.ShapeDtypeStruct((nb, S5_COLS), F32)),
        grid=(seq // tt,),
        in_specs=[pl.BlockSpec((tt, nb, S5_WIDTH), lambda i: (i, 0, 0))] + [_resident(a.shape) for a in weights],
        out_specs=(pl.BlockSpec((tt, nb, D_MODEL), lambda i: (i, 0, 0)), state, state),
        scratch_shapes=[pltpu.VMEM((tt * nb, 2 * S5_COLS), F32),
                        pltpu.VMEM((nb, S5_COLS), F32),
                        pltpu.VMEM((nb, S5_COLS), F32)],
        compiler_params=pltpu.CompilerParams(dimension_semantics=("arbitrary",),
                                             vmem_limit_bytes=V7X_VMEM_LIMIT_BYTES),
        name="s5_prompt",
    )(u3, *weights)


def _s5_sample_kernel(u_ref, h0re_ref, h0im_ref, bpack_ref, are_ref, aim_ref, cpack_ref, d_ref, wglu_ref,
                      bglu_ref, waout_ref, ya_ref, sre_ref, sim_ref, hs_ref):
    tt, nb, _ = u_ref.shape
    u = u_ref[...].reshape(tt * nb, S5_WIDTH)
    _s5_drive(u, bpack_ref, hs_ref, pl.ds(0, tt * nb))
    ar = are_ref[...]
    ai = aim_ref[...]
    hr = h0re_ref[...]
    hi = h0im_ref[...]
    for t in range(tt):
        r = pl.ds(t * nb, nb)
        hr, hi = (ar * hr - ai * hi + hs_ref[r, :S5_COLS], ar * hi + ai * hr + hs_ref[r, S5_COLS:])
        hs_ref[r, :S5_COLS] = hr
        hs_ref[r, S5_COLS:] = hi
    sre_ref[...] = hr
    sim_ref[...] = hi
    ya = _s5_readout(hs_ref, pl.ds(0, tt * nb), u, cpack_ref, d_ref, wglu_ref, bglu_ref, waout_ref)
    ya_ref[...] = ya.reshape(tt, nb, D_MODEL)


def _s5_sample(u3, h0_re, h0_im, w):
    tt, nb, _ = u3.shape
    weights = (w['bpack'], w['a_re'], w['a_im'], w['cpack'], w['s5_d'], w['w_glu'], w['b_glu'], w['w_a_out'])
    args = (u3, h0_re, h0_im) + weights
    whole = lambda a: pl.BlockSpec(a.shape, lambda i, nd=a.ndim: (0,) * nd)
    outs = (jax.ShapeDtypeStruct((tt, nb, D_MODEL), F32),
            jax.ShapeDtypeStruct((nb, S5_COLS), F32),
            jax.ShapeDtypeStruct((nb, S5_COLS), F32))
    return pl.pallas_call(
        _s5_sample_kernel, out_shape=outs, grid=(1,),
        in_specs=[whole(a) for a in args], out_specs=tuple(whole(o) for o in outs),
        scratch_shapes=[pltpu.VMEM((tt * nb, 2 * S5_COLS), F32)],
        compiler_params=pltpu.CompilerParams(dimension_semantics=("arbitrary",),
                                             vmem_limit_bytes=V7X_VMEM_LIMIT_BYTES),
        name="s5_sample",
    )(*args)


def _lanes(stat, width):
    if width % LANES:
        return stat[:, :width]
    return jnp.tile(stat, (1, width // LANES))


def _softmax_block_update(s, v, rows, m_ref, l_ref, acc_ref, first=False):
    if first:
        m_new = jnp.broadcast_to(jnp.max(s, axis=-1, keepdims=True), (s.shape[0], LANES))
        p = jnp.exp2(s - _lanes(m_new, s.shape[1]))
        l_ref[rows, :] = jnp.broadcast_to(jnp.sum(p, axis=-1, keepdims=True), (s.shape[0], LANES))
        acc_ref[rows, :] = _dot(p.astype(BF16), v)
        m_ref[rows, :] = m_new
        return
    m_old = m_ref[rows, :]
    m_new = jnp.maximum(m_old, jnp.max(s, axis=-1, keepdims=True))
    alpha = jnp.exp2(m_old - m_new)
    p = jnp.exp2(s - _lanes(m_new, s.shape[1]))
    l_ref[rows, :] = alpha * l_ref[rows, :] + jnp.sum(p, axis=-1, keepdims=True)
    acc_ref[rows, :] = _lanes(alpha, KV_LORA) * acc_ref[rows, :] + _dot(p.astype(BF16), v)
    m_ref[rows, :] = m_new


def _attn_prompt_kernel(qlat_ref, qrope_ref, kcat_ref, o_ref, qs_ref, m_ref, l_ref, acc_ref):
    tq = ATTN_TILE
    sub = tq // ATTN_QUERY_RUNS
    qi = pl.program_id(1)
    lane_head = lax.broadcasted_iota(jnp.int32, (sub, ROPE_LANES), 1) // QK_ROPE
    for run in range(ATTN_QUERY_RUNS):
        toks = pl.ds(run * sub, sub)
        qr = qrope_ref[toks, :]
        for h in range(MLA_HEADS):
            rows = pl.ds((run * MLA_HEADS + h) * sub, sub)
            qs_ref[rows, :KV_LORA] = qlat_ref[toks, KV_LORA * h:KV_LORA * (h + 1)]
            qs_ref[rows, KV_LORA:] = jnp.where(lane_head == h, qr, jnp.zeros_like(qr))
    n = MLA_HEADS * tq // ATTN_ROW_SPLIT
    groups_per_run = ATTN_ROW_SPLIT // ATTN_QUERY_RUNS

    def block(k, causal, first=False):
        kc = kcat_ref[pl.ds(pl.multiple_of(k * tq, tq), tq), :]

        def visible(part):
            run = part // groups_per_run
            return kc[:(run + 1) * sub] if causal else kc

        def scores(part):
            s = _dot_nt(qs_ref[pl.ds(part * n, n), :], visible(part))
            if causal:
                run = part // groups_per_run
                r = (lax.broadcasted_iota(jnp.int32, s.shape, 0) & (sub - 1)) + run * sub
                c = lax.broadcasted_iota(jnp.int32, s.shape, 1)
                s = jnp.where(c <= r, s, NEG_BIG)
            return s

        ready = [scores(part) for part in range(min(ATTN_LOOKAHEAD, ATTN_ROW_SPLIT))]
        for part in range(ATTN_ROW_SPLIT):
            if part + ATTN_LOOKAHEAD < ATTN_ROW_SPLIT:
                ready.append(scores(part + ATTN_LOOKAHEAD))
            _softmax_block_update(ready.pop(0), visible(part)[:, :KV_LORA], pl.ds(part * n, n),
                                  m_ref, l_ref, acc_ref, first=first)

    def past(k, carry):
        block(k, False)
        return carry

    @pl.when(qi == 0)
    def _():
        block(0, True, first=True)

    @pl.when(qi > 0)
    def _():
        block(0, False, first=True)
        lax.fori_loop(1, qi, past, 0)
        block(qi, True)

    o = acc_ref[...] / _lanes(l_ref[...], KV_LORA)
    for run in range(ATTN_QUERY_RUNS):
        for h in range(MLA_HEADS):
            seg = run * MLA_HEADS + h
            o_ref[pl.ds(run * sub, sub), KV_LORA * h:KV_LORA * (h + 1)] = o[seg * sub:(seg + 1) * sub].astype(BF16)


def _attn_prompt(qlat, qrope, kcat, *, batch, seq):
    tq = ATTN_TILE
    nq = seq // tq
    rows = MLA_HEADS * tq
    qmap = lambda b, i: (b * nq + i, 0)
    return pl.pallas_call(
        _attn_prompt_kernel,
        out_shape=jax.ShapeDtypeStruct((batch * seq, LAT_LANES), BF16),
        grid=(batch, nq),
        in_specs=[pl.BlockSpec((tq, LAT_LANES), qmap),
                  pl.BlockSpec((tq, ROPE_LANES), qmap),
                  pl.BlockSpec((seq, KV_LORA + ROPE_LANES), lambda b, i: (b, 0))],
        out_specs=pl.BlockSpec((tq, LAT_LANES), qmap),
        scratch_shapes=[pltpu.VMEM((rows, KV_LORA + ROPE_LANES), BF16),
                        pltpu.VMEM((rows, LANES), F32),
                        pltpu.VMEM((rows, LANES), F32),
                        pltpu.VMEM((rows, KV_LORA), F32)],
        compiler_params=pltpu.CompilerParams(dimension_semantics=("parallel", "arbitrary"),
                                             vmem_limit_bytes=V7X_VMEM_LIMIT_BYTES),
        name="attn_prompt",
    )(qlat, qrope, kcat)


def _softmax_part(s, v):
    m = jnp.max(s, axis=-1, keepdims=True)
    p = jnp.exp2(s - m)
    return m, jnp.sum(p, axis=-1, keepdims=True), _dot(p.astype(BF16), v)


def _merge_softmax_parts(parts, m_ref, l_ref, acc_ref):
    m_old = m_ref[...]
    m_new = m_old
    for m, _, _ in parts:
        m_new = jnp.maximum(m_new, m)
    alpha = jnp.exp2(m_old - m_new)
    l = alpha * l_ref[...]
    acc = _lanes(alpha, KV_LORA) * acc_ref[...]
    for m, psum, pv in parts:
        wgt = jnp.exp2(m - m_new)
        l = l + wgt * psum
        acc = acc + _lanes(wgt, KV_LORA) * pv
    m_ref[...] = m_new
    l_ref[...] = l
    acc_ref[...] = acc


def _attn_sample_kernel(pt_ref, qlat_ref, qrope_ref, ckvn_ref, kropen_ref, cl_hbm, cr_hbm, o_ref,
                        kbuf, rbuf, sem, qs_ref, qr_ref, kn_ref, rn_ref, m_ref, l_ref, acc_ref,
                        *, n_pages, n_new):
    pg = SAMPLE_PAGES_PER_STEP
    chunks = n_pages // pg
    g = pl.program_id(0)
    n_steps = pl.num_programs(0)
    c = g % chunks
    last = n_steps - 1
    slot = lax.rem(g, SAMPLE_SLOTS)
    ahead = SAMPLE_SLOTS - 1

    def page_copies(step, dst_slot):
        first = (step // chunks) * n_pages + (step % chunks) * pg
        copies = []
        for p in range(pg):
            page = pt_ref[first + p]
            keys = pl.ds(p * PAGE_SIZE, PAGE_SIZE)
            copies.append(pltpu.make_async_copy(cl_hbm.at[page], kbuf.at[dst_slot, keys], sem.at[0, dst_slot]))
            copies.append(pltpu.make_async_copy(cr_hbm.at[page], rbuf.at[dst_slot, :, keys], sem.at[1, dst_slot]))
        return copies

    @pl.when(g == 0)
    def _():
        for step in range(ahead):
            for cp in page_copies(step, step):
                cp.start()

    @pl.when(c == 0)
    def _():
        ql = qlat_ref[0].astype(F32)
        qr = qrope_ref[0].astype(F32)
        for h in range(MLA_HEADS):
            rows = pl.ds(h * n_new, n_new)
            qs_ref[rows, :] = ql[:, KV_LORA * h:KV_LORA * (h + 1)]
            qr_ref[rows, :] = qr[:, QK_ROPE * h:QK_ROPE * (h + 1)]
        m_ref[...] = jnp.full_like(m_ref, NEG_BIG)
        l_ref[...] = jnp.zeros_like(l_ref)
        acc_ref[...] = jnp.zeros_like(acc_ref)

    for cp in page_copies(g, slot):
        cp.wait()

    qs = qs_ref[...].astype(BF16)
    qr = qr_ref[...].astype(BF16)
    sub = pg * PAGE_SIZE // SAMPLE_SUBCHUNKS
    def scores(j):
        keys = pl.ds(j * sub, sub)
        kc = kbuf[slot, keys, :].astype(BF16)
        return _dot_nt(qs, kc) + _dot(qr, rbuf[slot, :, keys].astype(BF16)), kc

    parts = []
    ready = [scores(j) for j in range(min(SAMPLE_LOOKAHEAD, SAMPLE_SUBCHUNKS))]
    for j in range(SAMPLE_SUBCHUNKS):
        if j + SAMPLE_LOOKAHEAD < SAMPLE_SUBCHUNKS:
            ready.append(scores(j + SAMPLE_LOOKAHEAD))
        parts.append(_softmax_part(*ready.pop(0)))
    _merge_softmax_parts(parts, m_ref, l_ref, acc_ref)

    for cp in page_copies(jnp.minimum(g + ahead, last), lax.rem(g + ahead, SAMPLE_SLOTS)):
        cp.start()

    @pl.when(g == last)
    def _():
        for extra in range(1, ahead + 1):
            for cp in page_copies(last, lax.rem(g + extra, SAMPLE_SLOTS)):
                cp.wait()

    @pl.when(c == chunks - 1)
    def _():
        kn_ref[...] = jnp.zeros_like(kn_ref)
        rn_ref[...] = jnp.zeros_like(rn_ref)
        kn_ref[pl.ds(0, n_new), :] = ckvn_ref[0]
        rn_ref[pl.ds(0, n_new), :] = kropen_ref[0]
        kn = kn_ref[...].astype(BF16)
        sn = _dot_nt(qs, kn) + _dot_nt(qr, rn_ref[...].astype(BF16))
        t = lax.broadcasted_iota(jnp.int32, sn.shape, 0) % n_new
        j = lax.broadcasted_iota(jnp.int32, sn.shape, 1)
        sn = jnp.where(j <= t, sn, NEG_BIG)
        _merge_softmax_parts([_softmax_part(sn, kn)], m_ref, l_ref, acc_ref)
        o_ref[0] = acc_ref[...] / _lanes(l_ref[...], KV_LORA)


def _attn_sample(page_table, qlat, qrope, ckv_new, krope_new, cache_latent, cache_k_rope):
    nb, n_pages = page_table.shape
    n_new = qlat.shape[0] // nb
    pg = SAMPLE_PAGES_PER_STEP
    chunks = n_pages // pg
    rows = MLA_HEADS * n_new
    new_pad = 8
    per_batch = lambda g, pt: (g // chunks, 0, 0)
    grid_spec = pltpu.PrefetchScalarGridSpec(
        num_scalar_prefetch=1,
        grid=(nb * chunks,),
        in_specs=[pl.BlockSpec((1, n_new, LAT_LANES), per_batch),
                  pl.BlockSpec((1, n_new, ROPE_LANES), per_batch),
                  pl.BlockSpec((1, n_new, KV_LORA), per_batch),
                  pl.BlockSpec((1, n_new, QK_ROPE), per_batch),
                  pl.BlockSpec(memory_space=pl.ANY),
                  pl.BlockSpec(memory_space=pl.ANY)],
        out_specs=pl.BlockSpec((1, rows, KV_LORA), per_batch),
        scratch_shapes=[pltpu.VMEM((SAMPLE_SLOTS, pg * PAGE_SIZE, KV_LORA), F32),
                        pltpu.VMEM((SAMPLE_SLOTS, QK_ROPE, pg * PAGE_SIZE), F32),
                        pltpu.SemaphoreType.DMA((2, SAMPLE_SLOTS)),
                        pltpu.VMEM((rows, KV_LORA), F32),
                        pltpu.VMEM((rows, QK_ROPE), F32),
                        pltpu.VMEM((new_pad, KV_LORA), F32),
                        pltpu.VMEM((new_pad, QK_ROPE), F32),
                        pltpu.VMEM((rows, LANES), F32),
                        pltpu.VMEM((rows, LANES), F32),
                        pltpu.VMEM((rows, KV_LORA), F32)])
    return pl.pallas_call(
        functools.partial(_attn_sample_kernel, n_pages=n_pages, n_new=n_new),
        out_shape=jax.ShapeDtypeStruct((nb, rows, KV_LORA), F32),
        grid_spec=grid_spec,
        compiler_params=pltpu.CompilerParams(dimension_semantics=("arbitrary",),
                                             vmem_limit_bytes=V7X_VMEM_LIMIT_BYTES),
        name="attn_sample",
    )(page_table.reshape(-1), qlat.reshape(nb, n_new, LAT_LANES), qrope.reshape(nb, n_new, ROPE_LANES),
      ckv_new.reshape(nb, n_new, KV_LORA), krope_new.reshape(nb, n_new, QK_ROPE), cache_latent, cache_k_rope)


def _post_mixer_kernel(h1_ref, ya_ref, ol_ref, gm_ref, wgt_ref, wuv_ref, wbo_ref, wout_ref,
                       g2_ref, wg_ref, wu_ref, wd_ref, gf_ref, y_ref):
    h1 = h1_ref[...]
    xm = _rms(h1, gm_ref[...]).astype(BF16)
    gates = _dot(xm, wgt_ref[...])
    ol = ol_ref[...]
    k = UV_HEADS_PER_TILE * KV_LORA
    o = jnp.concatenate([_dot(ol[:, k * j:k * (j + 1)], wuv_ref[j])
                         for j in range(MLA_HEADS // UV_HEADS_PER_TILE)], axis=1)
    yb = _dot(o.astype(BF16), wbo_ref[...])
    merged = jax.nn.sigmoid(gates[:, :D_MODEL]) * ya_ref[...] + jax.nn.sigmoid(gates[:, D_MODEL:]) * yb
    h2 = h1 + _dot(merged.astype(BF16), wout_ref[...])
    h3 = _ffn_half_step(h2, g2_ref, wg_ref, wu_ref, wd_ref)
    y_ref[...] = _rms(h3, gf_ref[...])


def _post_mixer(h1, ya, olat, w, *, seq_major_ya):
    m = h1.shape[0]
    tm = TOKEN_TILE
    row = lambda i: (i, 0)
    if seq_major_ya:
        blocks_per_seq = seq_major_ya // tm
        ya_spec = pl.BlockSpec((tm, D_MODEL), lambda i: (i % blocks_per_seq, i // blocks_per_seq))
    else:
        ya_spec = pl.BlockSpec((tm, D_MODEL), row)
    weights = (w['g_mix'], w['w_gates'], w['w_uv_pad'], w['w_b_out'], w['w_out'],
               w['g_ffn2'], w['wg2'], w['wu2'], w['wd2'], w['g_final'])
    return pl.pallas_call(
        _post_mixer_kernel,
        out_shape=jax.ShapeDtypeStruct((m, D_MODEL), F32),
        grid=(m // tm,),
        in_specs=[pl.BlockSpec((tm, D_MODEL), row), ya_spec, pl.BlockSpec((tm, LAT_LANES), row)]
        + [_resident(a.shape) for a in weights],
        out_specs=pl.BlockSpec((tm, D_MODEL), row),
        compiler_params=pltpu.CompilerParams(dimension_semantics=("parallel",),
                                             vmem_limit_bytes=V7X_VMEM_LIMIT_BYTES),
        name="post_mixer",
    )(h1, ya, olat, *weights)


def _rope_tables(pos):
    inv_freq = 1.0 / (ROPE_BASE ** (jnp.arange(0, QK_ROPE, 2, dtype=F32) / QK_ROPE))
    ang = pos.astype(F32)[:, None] * inv_freq[None, :]
    cos, sin = jnp.cos(ang), jnp.sin(ang)
    cos_tab = jnp.tile(jnp.concatenate([cos, cos], axis=-1), (1, MLA_HEADS))
    sin_tab = jnp.tile(jnp.concatenate([-sin, sin], axis=-1), (1, MLA_HEADS))
    return cos_tab, sin_tab


def _swap_halves(a):
    half = QK_ROPE // 2
    return jnp.concatenate([a[..., half:], a[..., :half]], axis=-1)


def _block_diag_runs(blocks, run):
    g, r, c = blocks.shape
    eye = jnp.eye(run, dtype=blocks.dtype)
    tiles = jnp.einsum('tgrc,gk->tgrkc', blocks.reshape(g // run, run, r, c), eye)
    return tiles.reshape(g // run, run * r, run * c)


def _prepare_weights(p):
    w = {}
    vec = lambda a: a.reshape(1, -1).astype(F32)
    for name in ('g_ffn1', 'g_mix', 'g_q', 'g_kv', 'g_ffn2', 'g_final', 'b_glu'):
        w[name] = vec(p[name])
    for src, dst in (('w_ffn1_gate', 'wg1'), ('w_ffn1_up', 'wu1'), ('w_ffn1_down', 'wd1'),
                     ('w_ffn2_gate', 'wg2'), ('w_ffn2_up', 'wu2'), ('w_ffn2_down', 'wd2'),
                     ('w_glu', 'w_glu'), ('w_a_out', 'w_a_out'), ('w_b_out', 'w_b_out'), ('w_out', 'w_out')):
        w[dst] = p[src].astype(BF16)
    w_in = p['w_in']
    off_kr = S5_WIDTH + Q_LORA + KV_LORA
    k_r = w_in[:, off_kr:off_kr + QK_ROPE]
    reps = LANES // 2 // QK_ROPE
    w['w_in_ext'] = jnp.concatenate(
        [w_in[:, :off_kr], jnp.tile(k_r, (1, reps)), jnp.tile(_swap_halves(k_r), (1, reps))],
        axis=1).astype(BF16)
    w['w_gates'] = w_in[:, off_kr + QK_ROPE:].astype(BF16)
    w_uq = p['w_uq']
    uq_rope = w_uq[:, :, QK_NOPE:]
    w['w_uq_ext'] = jnp.concatenate(
        [w_uq[:, :, :QK_NOPE].reshape(Q_LORA, -1), uq_rope.reshape(Q_LORA, -1)], axis=1).astype(BF16)
    uk = jnp.transpose(p['w_uk'], (1, 2, 0))
    uk_pad = jnp.zeros((MLA_HEADS, 2, QK_NOPE, KV_LORA), F32)
    uk_pad = uk_pad.at[jnp.arange(MLA_HEADS), jnp.arange(MLA_HEADS) % 2].set(uk)
    w['w_uk_pad'] = uk_pad.reshape(MLA_HEADS, 2 * QK_NOPE, KV_LORA).astype(BF16)
    uv = jnp.transpose(p['w_uv'], (1, 0, 2))
    w['w_uv_pad'] = _block_diag_runs(uv, UV_HEADS_PER_TILE).astype(BF16)
    lam_re, lam_im = p['s5_a_re'].astype(F32), p['s5_a_im'].astype(F32)
    dt = jnp.exp(p['s5_log_dt'].astype(F32))[:, None]
    mag = jnp.exp(lam_re * dt)
    ab_re, ab_im = mag * jnp.cos(lam_im * dt), mag * jnp.sin(lam_im * dt)
    den = lam_re * lam_re + lam_im * lam_im
    num_re, num_im = ab_re - 1.0, ab_im
    k_re = (num_re * lam_re + num_im * lam_im) / den
    k_im = (num_im * lam_re - num_re * lam_im) / den
    w['a_re'] = ab_re.reshape(1, S5_COLS)
    w['a_im'] = ab_im.reshape(1, S5_COLS)
    b_re, b_im = p['s5_b_re'].astype(F32), p['s5_b_im'].astype(F32)
    kb_re = k_re[..., None] * b_re - k_im[..., None] * b_im
    kb_im = k_re[..., None] * b_im + k_im[..., None] * b_re

    def to_in(a):
        tiles = _block_diag_runs(jnp.transpose(a, (0, 2, 1)), S5_GROUPS_PER_STATE_TILE)
        n, r, c = tiles.shape
        pos = jnp.arange(n) % S5_TILES_PER_SLAB
        slab = jnp.zeros((n, S5_TILES_PER_SLAB, r, c), F32).at[jnp.arange(n), pos].set(tiles)
        return slab.reshape(n, S5_TILES_PER_SLAB * r, c)

    w['bpack'] = jnp.concatenate([to_in(kb_re), to_in(kb_im)], axis=0).astype(BF16)
    to_out = lambda a: _block_diag_runs(jnp.transpose(a, (0, 2, 1)), S5_GROUPS_PER_OUT_TILE)
    w['cpack'] = jnp.concatenate([to_out(p['s5_c_re'].astype(F32)), -to_out(p['s5_c_im'].astype(F32))],
                                 axis=1).astype(BF16)
    w['s5_d'] = vec(p['s5_d'])
    return w


def kernel(x_prompt, x_sample, cache_latent, cache_k_rope, state_ssm_re, state_ssm_im, page_table, g_ffn1, w_ffn1_gate, w_ffn1_up, w_ffn1_down, g_mix, w_in, s5_a_re, s5_a_im, s5_log_dt, s5_b_re, s5_b_im, s5_c_re, s5_c_im, s5_d, w_glu, b_glu, w_a_out, g_q, w_uq, g_kv, w_uk, w_uv, w_b_out, w_out, g_ffn2, w_ffn2_gate, w_ffn2_up, w_ffn2_down, g_final):
    layer = 0
    params = dict(
        g_ffn1=g_ffn1, w_ffn1_gate=w_ffn1_gate, w_ffn1_up=w_ffn1_up, w_ffn1_down=w_ffn1_down,
        g_mix=g_mix, w_in=w_in, s5_a_re=s5_a_re, s5_a_im=s5_a_im, s5_log_dt=s5_log_dt,
        s5_b_re=s5_b_re, s5_b_im=s5_b_im, s5_c_re=s5_c_re, s5_c_im=s5_c_im, s5_d=s5_d,
        w_glu=w_glu, b_glu=b_glu, w_a_out=w_a_out, g_q=g_q, w_uq=w_uq, g_kv=g_kv, w_uk=w_uk,
        w_uv=w_uv, w_b_out=w_b_out, w_out=w_out, g_ffn2=g_ffn2, w_ffn2_gate=w_ffn2_gate,
        w_ffn2_up=w_ffn2_up, w_ffn2_down=w_ffn2_down)
    p = {k: v[layer] for k, v in params.items()}
    p['g_final'] = g_final
    w = _prepare_weights(p)

    nb, seq, _ = x_prompt.shape
    cos_p, sin_p = _rope_tables(jnp.arange(seq))
    h1, u, ckv, krope, kcat, qlat, qrope = _pre_mixer(
        x_prompt.reshape(nb * seq, D_MODEL), cos_p, sin_p, w, seq_major_u=seq)
    olat = _attn_prompt(qlat, qrope, kcat, batch=nb, seq=seq)
    ya, sre_p, sim_p = _s5_prompt(u.reshape(seq, nb, S5_WIDTH), w, run_after=olat)
    y_prompt = _post_mixer(h1, ya.reshape(seq, nb * D_MODEL), olat, w, seq_major_ya=seq).reshape(nb, seq, D_MODEL)
    ckv_prompt = ckv.reshape(1, nb, seq, KV_LORA)
    krope_prompt = krope.reshape(1, nb, seq, QK_ROPE)
    group = lambda a: a.reshape(1, a.shape[0], S5_GROUPS, S5_STATE)

    sb, n_new, _ = x_sample.shape
    n_pages = page_table.shape[1]
    past_len = n_pages * PAGE_SIZE
    cos_s, sin_s = _rope_tables(past_len + jnp.arange(n_new))
    cos_s, sin_s = jnp.tile(cos_s, (sb, 1)), jnp.tile(sin_s, (sb, 1))
    h1s, us, ckvs, kropes, _, qlats, qropes = _pre_mixer(
        x_sample.reshape(sb * n_new, D_MODEL), cos_s, sin_s, w, seq_major_u=0)
    us_t = jnp.transpose(us.reshape(sb, n_new, S5_WIDTH), (1, 0, 2))
    yas_t, sre_s, sim_s = _s5_sample(us_t, state_ssm_re[layer].reshape(sb, S5_COLS),
                                     state_ssm_im[layer].reshape(sb, S5_COLS), w)
    yas = jnp.transpose(yas_t, (1, 0, 2)).reshape(sb * n_new, D_MODEL)
    rope_pages = jnp.swapaxes(cache_k_rope[layer], 1, 2)
    ol = _attn_sample(page_table, qlats, qropes, ckvs, kropes, cache_latent[layer], rope_pages)
    olats = jnp.transpose(ol.reshape(sb, MLA_HEADS, n_new, KV_LORA), (0, 2, 1, 3))
    olats = olats.reshape(sb * n_new, LAT_LANES).astype(BF16)
    y_sample = _post_mixer(h1s, yas, olats, w, seq_major_ya=0).reshape(sb, n_new, D_MODEL)

    return (y_prompt, y_sample, ckv_prompt, krope_prompt,
            ckvs.reshape(1, sb, n_new, KV_LORA), kropes.reshape(1, sb, n_new, QK_ROPE),
            group(sre_p), group(sim_p), group(sre_s), group(sim_s))
```
